```python
import jax, jax.numpy as jnp
from jax import lax
import numpy as np

D_MODEL = 1024
BATCH = 4
SEQ = 8192
DEPTH = 2

N_MIXERS = 2
N_A_LAYERS = (DEPTH + 1) // 2
N_B_LAYERS = DEPTH // 2
NORM_EPS = 1e-6

A_WIDTH = 2 * D_MODEL
A_GROUPS = 8
A_CHUNK = 128
A_GROUP_DIM = A_WIDTH // A_GROUPS

B_HEADS = 4
B_KEY_DIM = D_MODEL // 2
B_VAL_DIM = D_MODEL
B_HEAD_K = B_KEY_DIM // B_HEADS
B_HEAD_V = B_VAL_DIM // B_HEADS
B_GATE_RANK = 16
B_GATE_NORM = 16.0
B_CHUNK = 64
B_PROJ = 2 * B_KEY_DIM + 2 * B_VAL_DIM + B_GATE_RANK

P_HEADS = 8
P_NKEYS = 128
P_EXPERTS = P_NKEYS * P_NKEYS
P_QDIM = 256
P_HALF = P_QDIM // 2
P_TOPK = 16
P_BLOCK = 128

kernel_name = "hybrid_gmlp_gla_peer_trunk"


def rmsnorm(x, g):
    xf = x.astype(jnp.float32)
    y = xf * lax.rsqrt(jnp.mean(xf * xf, axis=-1, keepdims=True) + NORM_EPS)
    return (y * g.astype(jnp.float32)).astype(x.dtype)


def layernorm(x, g, b):
    xf = x.astype(jnp.float32)
    mu = jnp.mean(xf, axis=-1, keepdims=True)
    xc = xf - mu
    y = xc * lax.rsqrt(jnp.mean(xc * xc, axis=-1, keepdims=True) + NORM_EPS)
    return (y * g.astype(jnp.float32) + b.astype(jnp.float32)).astype(x.dtype)


def mixer_a(h, w_in, b_in, ln_g, ln_b, w_s, b_s, w_out):
    B, L, _ = h.shape
    z = jax.nn.gelu(h @ w_in + b_in, approximate=False)
    u, v = jnp.split(z, 2, axis=-1)
    v = layernorm(v, ln_g, ln_b)
    v = v.reshape(B, L // A_CHUNK, A_CHUNK, A_GROUPS, A_GROUP_DIM)
    causal = jnp.tril(jnp.ones((A_CHUNK, A_CHUNK), dtype=bool))
    w = jnp.where(causal[None], w_s, 0.0)
    sv = jnp.einsum('gts,bnsgc->bntgc', w, v) + b_s.T[:, :, None]
    y = u * sv.reshape(B, L, A_WIDTH)
    return y @ w_out


def mixer_b(h, w_in, w_g2, b_g, norm_g, w_out):
    B, L, _ = h.shape
    f32 = jnp.float32
    proj = h @ w_in
    q, k, v, r, g_lr = jnp.split(
        proj, [B_KEY_DIM, 2 * B_KEY_DIM, 2 * B_KEY_DIM + B_VAL_DIM, 2 * B_KEY_DIM + 2 * B_VAL_DIM], axis=-1)
    log_a = jax.nn.log_sigmoid((g_lr @ w_g2 + b_g).astype(f32)) / B_GATE_NORM
    nc = L // B_CHUNK

    def to_chunks(t, d):
        return t.astype(f32).reshape(B, nc, B_CHUNK, B_HEADS, d).transpose(1, 0, 3, 2, 4)

    qc = to_chunks(q, B_HEAD_K) * (B_HEAD_K ** -0.5)
    kc = to_chunks(k, B_HEAD_K)
    vc = to_chunks(v, B_HEAD_V)
    gc = jnp.cumsum(to_chunks(log_a, B_HEAD_K), axis=3)
    causal = jnp.tril(jnp.ones((B_CHUNK, B_CHUNK), dtype=bool))[:, :, None]

    def step(S, inp):
        qb, kb, vb, gb = inp
        diff = gb[:, :, :, None, :] - gb[:, :, None, :, :]
        decay = jnp.where(causal, jnp.exp(jnp.where(causal, diff, 0.0)), 0.0)
        scores = jnp.einsum('bhtsk,bhsk->bhts', qb[:, :, :, None, :] * decay, kb)
        o = scores @ vb + jnp.einsum('bhtk,bhkv->bhtv', qb * jnp.exp(gb), S)
        g_last = gb[:, :, -1:, :]
        S = jnp.exp(g_last[:, :, 0, :, None]) * S + jnp.einsum(
            'bhsk,bhsv->bhkv', kb * jnp.exp(g_last - gb), vb)
        return S, o

    S0 = jnp.zeros((B, B_HEADS, B_HEAD_K, B_HEAD_V), f32)
    _, o = lax.scan(step, S0, (qc, kc, vc, gc))
    o = o.transpose(1, 0, 3, 2, 4).reshape(B, L, B_HEADS, B_HEAD_V)
    o = rmsnorm(o, norm_g).reshape(B, L, B_VAL_DIM).astype(h.dtype)
    o = o * jax.nn.silu(r)
    return o @ w_out


def peer(h, w_q, k1, k2, u_tab, v_tab):
    B, L, D = h.shape
    f32 = jnp.float32
    xs = h.reshape(-1, P_BLOCK, D)

    def block(xb):
        q = (xb @ w_q).reshape(P_BLOCK, P_HEADS, P_QDIM).astype(f32)
        s1 = jnp.einsum('thd,nd->thn', q[..., :P_HALF], k1.astype(f32))
        s2 = jnp.einsum('thd,nd->thn', q[..., P_HALF:], k2.astype(f32))
        v1, i1 = lax.top_k(s1, P_TOPK)
        v2, i2 = lax.top_k(s2, P_TOPK)
        cand = (v1[..., :, None] + v2[..., None, :]).reshape(P_BLOCK, P_HEADS, P_TOPK * P_TOPK)
        cand_id = (i1[..., :, None] * P_NKEYS + i2[..., None, :]).reshape(P_BLOCK, P_HEADS, P_TOPK * P_TOPK)
        top_s, pos = lax.top_k(cand, P_TOPK)
        eid = jnp.take_along_axis(cand_id, pos, axis=-1)
        gate = jax.nn.softmax(top_s, axis=-1)
        u_e = jnp.take(u_tab, eid, axis=0)
        act = jax.nn.gelu(jnp.einsum('thkd,td->thk', u_e, xb), approximate=False)
        v_e = jnp.take(v_tab, eid, axis=0)
        return jnp.einsum('thk,thkd->td', (gate * act).astype(xb.dtype), v_e)

    return lax.map(block, xs).reshape(B, L, D)


def setup_inputs(seed: int = 0) -> dict:
    key = jax.random.key(seed)
    ks = jax.random.split(key, 24)
    f32 = jnp.float32

    def nrm(k, shape, scale):
        return jax.random.normal(k, shape, f32) * scale

    return {
        "x": nrm(ks[0], (BATCH, SEQ, D_MODEL), 1.0),
        "norm_mix": 1.0 + nrm(ks[1], (DEPTH, D_MODEL), 0.02),
        "norm_ffn": 1.0 + nrm(ks[2], (DEPTH, D_MODEL), 0.02),
        "final_norm": 1.0 + nrm(ks[3], (D_MODEL,), 0.02),
        "a_w_in": nrm(ks[4], (N_A_LAYERS, D_MODEL, 2 * A_WIDTH), D_MODEL ** -0.5),
        "a_b_in": nrm(ks[5], (N_A_LAYERS, 2 * A_WIDTH), 0.02),
        "a_ln_g": 1.0 + nrm(ks[6], (N_A_LAYERS, A_WIDTH), 0.02),
        "a_ln_b": nrm(ks[7], (N_A_LAYERS, A_WIDTH), 0.02),
        "a_w_s": nrm(ks[8], (N_A_LAYERS, A_GROUPS, A_CHUNK, A_CHUNK), A_CHUNK ** -0.5),
        "a_b_s": 1.0 + nrm(ks[9], (N_A_LAYERS, A_GROUPS, A_CHUNK), 0.02),
        "a_w_out": nrm(ks[10], (N_A_LAYERS, A_WIDTH, D_MODEL), A_WIDTH ** -0.5),
        "b_w_in": nrm(ks[11], (N_B_LAYERS, D_MODEL, B_PROJ), D_MODEL ** -0.5),
        "b_w_g2": nrm(ks[12], (N_B_LAYERS, B_GATE_RANK, B_KEY_DIM), B_GATE_RANK ** -0.5),
        "b_b_g": nrm(ks[13], (N_B_LAYERS, B_KEY_DIM), 0.02),
        "b_norm": 1.0 + nrm(ks[14], (N_B_LAYERS, B_HEAD_V), 0.02),
        "b_w_out": nrm(ks[15], (N_B_LAYERS, B_VAL_DIM, D_MODEL), B_VAL_DIM ** -0.5),
        "p_w_q": nrm(ks[16], (DEPTH, D_MODEL, P_HEADS * P_QDIM), D_MODEL ** -0.5),
        "p_k1": nrm(ks[17], (DEPTH, P_NKEYS, P_HALF), P_HALF ** -0.5),
        "p_k2": nrm(ks[18], (DEPTH, P_NKEYS, P_HALF), P_HALF ** -0.5),
        "p_u": nrm(ks[19], (DEPTH, P_EXPERTS, D_MODEL), D_MODEL ** -0.5),
        "p_v": nrm(ks[20], (DEPTH, P_EXPERTS, D_MODEL), P_HEADS ** -0.5),
    }


def reference(x, norm_mix, norm_ffn, final_norm,
              a_w_in, a_b_in, a_ln_g, a_ln_b, a_w_s, a_b_s, a_w_out,
              b_w_in, b_w_g2, b_b_g, b_norm, b_w_out,
              p_w_q, p_k1, p_k2, p_u, p_v):
    for i in range(DEPTH):
        h = rmsnorm(x, norm_mix[i])
        j = i // N_MIXERS
        if i % N_MIXERS == 0:
            x = x + mixer_a(h, a_w_in[j], a_b_in[j], a_ln_g[j], a_ln_b[j], a_w_s[j], a_b_s[j], a_w_out[j])
        else:
            x = x + mixer_b(h, b_w_in[j], b_w_g2[j], b_b_g[j], b_norm[j], b_w_out[j])
        h = rmsnorm(x, norm_ffn[i])
        x = x + peer(h, p_w_q[i], p_k1[i], p_k2[i], p_u[i], p_v[i])
    return rmsnorm(x, final_norm)
```

```python
import functools
import math

import jax
import jax.numpy as jnp
from jax import lax
from jax.experimental import pallas as pl
from jax.experimental.pallas import tpu as pltpu

F32 = jnp.float32
BF16 = jnp.bfloat16

NORM_EPS = 1e-6

LANES = 128
SUBLANES = 8
VMEM_LIMIT_BYTES = 56 * 1024 * 1024

A_GROUPS = 8
A_CHUNK = 128
B_HEADS = 4
B_GATE_RANK = 16
B_GATE_NORM = 16.0
B_CHUNK = 64
P_HEADS = 8
P_NKEYS = 128
P_TOPK = 16
P_ROWS_PER_STEP = 8


def _rms(x, g):
    return x * lax.rsqrt(jnp.mean(x * x, axis=-1, keepdims=True) + NORM_EPS) * g


def _gelu(x):
    return 0.5 * x * (1.0 + lax.erf(x * (1.0 / math.sqrt(2.0))))


def _split_bf16(x):
    hi = x.astype(BF16)
    lo = (x - hi.astype(F32)).astype(BF16)
    return hi, lo


def _const_spec(shape):
    nd = len(shape)
    return pl.BlockSpec(shape, lambda *_: (0,) * nd)


def _mixer_a_kernel(x_ref, g_ref, win_ref, bin_ref, lng_ref, lnb_ref, ws_ref, bs_ref, wout_ref,
                    o_ref, y_sc):
    tm = x_ref.shape[0]
    width = lng_ref.shape[1]
    gdim = width // A_GROUPS
    x = x_ref[...]
    h = _rms(x, g_ref[...]).astype(BF16)
    z = _gelu(jnp.dot(h, win_ref[...], preferred_element_type=F32) + bin_ref[...])
    u = z[:, :width]
    v = z[:, width:]
    mu = jnp.mean(v, axis=-1, keepdims=True)
    vc = v - mu
    var = jnp.mean(vc * vc, axis=-1, keepdims=True)
    vn = (vc * lax.rsqrt(var + NORM_EPS) * lng_ref[...] + lnb_ref[...]).astype(BF16)
    for c in range(tm // A_CHUNK):
        rows = slice(c * A_CHUNK, (c + 1) * A_CHUNK)
        for g in range(A_GROUPS):
            cols = slice(g * gdim, (g + 1) * gdim)
            sv = jnp.dot(ws_ref[g], vn[rows, cols], preferred_element_type=F32) + bs_ref[:, cols]
            y_sc[rows, cols] = (u[rows, cols] * sv).astype(BF16)
    o_ref[...] = x + jnp.dot(y_sc[...], wout_ref[...], preferred_element_type=F32)


def _mixer_a(x2, g, w_in, b_in, ln_g, ln_b, w_s, b_s, w_out, tm=256):
    n, d = x2.shape
    width = ln_g.shape[0]
    causal = jnp.tril(jnp.ones((A_CHUNK, A_CHUNK), dtype=bool))
    ws = jnp.where(causal[None], w_s, 0.0).astype(BF16)
    bs = jnp.repeat(b_s.T, width // A_GROUPS, axis=1)
    return pl.pallas_call(
        _mixer_a_kernel,
        out_shape=jax.ShapeDtypeStruct((n, d), F32),
        grid=(n // tm,),
        in_specs=[
            pl.BlockSpec((tm, d), lambda i: (i, 0)),
            _const_spec((1, d)),
            _const_spec((d, 2 * width)),
            _const_spec((1, 2 * width)),
            _const_spec((1, width)),
            _const_spec((1, width)),
            _const_spec((A_GROUPS, A_CHUNK, A_CHUNK)),
            _const_spec((A_CHUNK, width)),
            _const_spec((width, d)),
        ],
        out_specs=pl.BlockSpec((tm, d), lambda i: (i, 0)),
        scratch_shapes=[pltpu.VMEM((tm, width), BF16)],
        compiler_params=pltpu.CompilerParams(
            dimension_semantics=("arbitrary",), vmem_limit_bytes=VMEM_LIMIT_BYTES),
        name="mixer_a",
    )(x2, g.reshape(1, d), w_in.astype(BF16), b_in.reshape(1, -1), ln_g.reshape(1, -1),
      ln_b.reshape(1, -1), ws, bs, w_out.astype(BF16))


def _gla_kernel(x_ref, g_ref, win_ref, wg2_ref, bg_ref, ng_ref, wout_ref, tri_ref,
                o_ref, st_ref, oc_sc):
    tc = x_ref.shape[1]
    kdim = bg_ref.shape[1]
    vdim = ng_ref.shape[1]
    dk = kdim // B_HEADS
    dv = vdim // B_HEADS

    @pl.when(pl.program_id(1) == 0)
    def _():
        st_ref[...] = jnp.zeros_like(st_ref)

    x = x_ref[0]
    h = _rms(x, g_ref[...]).astype(BF16)
    proj = jnp.dot(h, win_ref[...], preferred_element_type=F32)
    q = proj[:, :kdim] * (dk ** -0.5)
    k = proj[:, kdim:2 * kdim]
    v = proj[:, 2 * kdim:2 * kdim + vdim]
    r = proj[:, 2 * kdim + vdim:2 * kdim + 2 * vdim]
    glr = proj[:, 2 * kdim + 2 * vdim:]

    g_hi, g_lo = _split_bf16(glr)
    w_hi = wg2_ref[0]
    w_lo = wg2_ref[1]
    xg = (jnp.dot(g_hi, w_hi, preferred_element_type=F32)
          + jnp.dot(g_hi, w_lo, preferred_element_type=F32)
          + jnp.dot(g_lo, w_hi, preferred_element_type=F32)) + bg_ref[...]
    log_a = (jnp.minimum(xg, 0.0) - jnp.log(1.0 + jnp.exp(-jnp.abs(xg)))) * (1.0 / B_GATE_NORM)

    a_hi, a_lo = _split_bf16(log_a)
    tri = tri_ref[...]
    gcum = (jnp.dot(tri, a_hi, preferred_element_type=F32)
            + jnp.dot(tri, a_lo, preferred_element_type=F32))

    eg = jnp.exp(gcum)
    qd = (q * eg).astype(BF16)
    kd = (k * jnp.exp(-gcum)).astype(BF16)
    vb = v.astype(BF16)
    row = lax.broadcasted_iota(jnp.int32, (B_CHUNK, B_CHUNK), 0)
    col = lax.broadcasted_iota(jnp.int32, (B_CHUNK, B_CHUNK), 1)
    causal = row >= col

    for c in range(tc // B_CHUNK):
        rows = slice(c * B_CHUNK, (c + 1) * B_CHUNK)
        last = (c + 1) * B_CHUNK - 1
        for hh in range(B_HEADS):
            kc = slice(hh * dk, (hh + 1) * dk)
            vc = slice(hh * dv, (hh + 1) * dv)
            g_last = gcum[last:last + 1, kc]
            ks = (k[rows, kc] * jnp.exp(g_last - gcum[rows, kc])).astype(BF16)
            st = st_ref[hh]
            sc = lax.dot_general(qd[rows, kc], kd[rows, kc], (((1,), (1,)), ((), ())),
                                 preferred_element_type=F32)
            sc = jnp.where(causal, sc, 0.0).astype(BF16)
            o = jnp.dot(sc, vb[rows, vc], preferred_element_type=F32)
            o = o + lax.dot_general(qd[rows, kc], st.astype(BF16), (((1,), (1,)), ((), ())),
                                    preferred_element_type=F32)
            oc_sc[rows, vc] = o
            upd = lax.dot_general(vb[rows, vc], ks, (((0,), (0,)), ((), ())),
                                  preferred_element_type=F32)
            st_ref[hh] = st * jnp.exp(g_last) + upd

    ng = ng_ref[...]
    for hh in range(B_HEADS):
        vc = slice(hh * dv, (hh + 1) * dv)
        oh = oc_sc[:, vc]
        oh = oh * lax.rsqrt(jnp.mean(oh * oh, axis=-1, keepdims=True) + NORM_EPS) * ng[:, vc]
        rh = r[:, vc]
        oc_sc[:, vc] = oh * (rh / (1.0 + jnp.exp(-rh)))
    o_ref[0] = x + jnp.dot(oc_sc[...].astype(BF16), wout_ref[...], preferred_element_type=F32)


def _mixer_b(x3, g, w_in, w_g2, b_g, norm_g, w_out, tc=256):
    b, l, d = x3.shape
    kdim = w_g2.shape[1]
    vdim = w_out.shape[0]
    main = 2 * kdim + 2 * vdim
    w_in_p = jnp.concatenate(
        [w_in, jnp.zeros((d, LANES - B_GATE_RANK), w_in.dtype)], axis=1).astype(BF16)
    w_g2_p = jnp.concatenate([w_g2, jnp.zeros((LANES - B_GATE_RANK, kdim), w_g2.dtype)], axis=0)
    hi = w_g2_p.astype(BF16)
    lo = (w_g2_p - hi.astype(F32)).astype(BF16)
    wg2 = jnp.stack([hi, lo])
    ng = jnp.tile(norm_g, B_HEADS).reshape(1, vdim)
    idx = jnp.arange(tc)
    tri = ((idx[:, None] >= idx[None, :]) &
           (idx[:, None] // B_CHUNK == idx[None, :] // B_CHUNK)).astype(BF16)
    return pl.pallas_call(
        _gla_kernel,
        out_shape=jax.ShapeDtypeStruct((b, l, d), F32),
        grid=(b, l // tc),
        in_specs=[
            pl.BlockSpec((1, tc, d), lambda i, j: (i, j, 0)),
            _const_spec((1, d)),
            _const_spec((d, main + LANES)),
            _const_spec((2, LANES, kdim)),
            _const_spec((1, kdim)),
            _const_spec((1, vdim)),
            _const_spec((vdim, d)),
            _const_spec((tc, tc)),
        ],
        out_specs=pl.BlockSpec((1, tc, d), lambda i, j: (i, j, 0)),
        scratch_shapes=[pltpu.VMEM((B_HEADS, vdim // B_HEADS, kdim // B_HEADS), F32),
                        pltpu.VMEM((tc, vdim), F32)],
        compiler_params=pltpu.CompilerParams(
            dimension_semantics=("arbitrary", "arbitrary"), vmem_limit_bytes=VMEM_LIMIT_BYTES),
        name="mixer_b_gla",
    )(x3, g.reshape(1, d), w_in_p, wg2, b_g.reshape(1, kdim), ng, w_out.astype(BF16), tri)


def _top_values(s, vals_ref, want_rank):
    cur = s
    rank = jnp.full(s.shape, float(P_TOPK), F32) if want_rank else None
    for r in range(P_TOPK):
        m = jnp.max(cur, axis=0, keepdims=True)
        hit = cur == m
        if want_rank:
            rank = jnp.where(hit, float(r), rank)
        cur = jnp.where(hit, -jnp.inf, cur)
        vals_ref[r:r + 1, :] = m
    return rank


def _peer_prep_kernel(x_ref, g_ref, wq_ref, k1_ref, k2_ref,
                      ht_ref, r2_ref, e2_ref, c_ref, e1_ref,
                      qt_sc, v1_sc, v2_sc, t_sc):
    tt = x_ref.shape[0]
    half = k1_ref.shape[1]
    h = _rms(x_ref[...], g_ref[...])
    ht = h.T.astype(BF16)
    ht_ref[...] = ht
    qt_sc[...] = jnp.dot(wq_ref[...], ht, preferred_element_type=F32)
    k1 = k1_ref[...]
    k2 = k2_ref[...]

    def head(hh, carry):
        base = pl.multiple_of(hh * (2 * half), 2 * half)
        for lc in range(tt // LANES):
            lanes = slice(lc * LANES, (lc + 1) * LANES)
            q1 = qt_sc[pl.ds(base, half), lanes].astype(BF16)
            q2 = qt_sc[pl.ds(base + half, half), lanes].astype(BF16)
            s1 = jnp.dot(k1, q1, preferred_element_type=F32)
            s2 = jnp.dot(k2, q2, preferred_element_type=F32)
            _top_values(s1, v1_sc, False)
            rank2 = _top_values(s2, v2_sc, True)
            v1 = v1_sc[...]
            v2 = v2_sc[...]
            cands = [v1[0:1] + v2[0:8], v1[0:1] + v2[8:16]]
            cands += [v1[a:a + 1] + v2[0:8] for a in range(1, 8)]
            cands += [v1[8:16] + v2[0:1]]
            cand = jnp.concatenate(cands, axis=0)
            _top_values(cand, t_sc, False)
            top = t_sc[...]
            tau = top[P_TOPK - 1:P_TOPK]
            inv_z = 1.0 / jnp.sum(jnp.exp(top - top[0:1]), axis=0, keepdims=True)
            cnt = jnp.zeros(s1.shape, F32)
            for b in range(P_TOPK):
                cnt = cnt + jnp.where(s1 + v2[b:b + 1] >= tau, 1.0, 0.0)
            r2_ref[hh, :, lanes] = rank2
            e2_ref[hh, :, lanes] = jnp.exp(s2 - v2[0:1])
            c_ref[hh, :, lanes] = cnt
            e1_ref[hh, :, lanes] = jnp.exp(s1 - v1[0:1]) * inv_z
        return carry

    lax.fori_loop(0, P_HEADS, head, 0)


def _peer_prep(x2, g, w_q, k1, k2, tt=256):
    n, d = x2.shape
    qdim = w_q.shape[1]
    gshape = jax.ShapeDtypeStruct((P_HEADS, P_NKEYS, n), F32)
    gspec = pl.BlockSpec((P_HEADS, P_NKEYS, tt), lambda i: (0, 0, i))
    return pl.pallas_call(
        _peer_prep_kernel,
        out_shape=(jax.ShapeDtypeStruct((d, n), BF16), gshape, gshape, gshape, gshape),
        grid=(n // tt,),
        in_specs=[
            pl.BlockSpec((tt, d), lambda i: (i, 0)),
            _const_spec((1, d)),
            _const_spec((qdim, d)),
            _const_spec(k1.shape),
            _const_spec(k2.shape),
        ],
        out_specs=(pl.BlockSpec((d, tt), lambda i: (0, i)), gspec, gspec, gspec, gspec),
        scratch_shapes=[pltpu.VMEM((qdim, tt), F32),
                        pltpu.VMEM((P_TOPK, LANES), F32),
                        pltpu.VMEM((P_TOPK, LANES), F32),
                        pltpu.VMEM((P_TOPK, LANES), F32)],
        compiler_params=pltpu.CompilerParams(
            dimension_semantics=("arbitrary",), vmem_limit_bytes=VMEM_LIMIT_BYTES),
        name="peer_prep",
    )(x2, g.reshape(1, d), w_q.T.astype(BF16), k1.astype(BF16), k2.astype(BF16))


def _peer_dense_kernel(ht_ref, r2_ref, e2_ref, c_ref, e1_ref, u_ref, vt_ref, x_ref, fg_ref,
                       o_ref, a_sc, w_sc, acc_sc, *, final_norm):
    e = pl.program_id(1)
    tb = ht_ref.shape[1]

    @pl.when(e == 0)
    def _():
        acc_sc[...] = jnp.zeros_like(acc_sc)

    a_sc[...] = jnp.dot(u_ref[...], ht_ref[...], preferred_element_type=F32)
    for i in range(P_ROWS_PER_STEP):
        rows = slice(i * P_NKEYS, (i + 1) * P_NKEYS)
        for lc in range(tb // LANES):
            lanes = slice(lc * LANES, (lc + 1) * LANES)
            gate = jnp.zeros((P_NKEYS, LANES), F32)
            for hh in range(P_HEADS):
                sel = r2_ref[hh, :, lanes] < c_ref[hh, i:i + 1, lanes]
                gate = gate + jnp.where(sel, e2_ref[hh, :, lanes], 0.0) * e1_ref[hh, i:i + 1, lanes]
            w_sc[rows, lanes] = (gate * _gelu(a_sc[rows, lanes])).astype(BF16)
    acc_sc[...] += jnp.dot(vt_ref[...], w_sc[...], preferred_element_type=F32)

    @pl.when(e == pl.num_programs(1) - 1)
    def _():
        y = x_ref[...] + acc_sc[...].T
        if final_norm:
            y = _rms(y, fg_ref[...])
        o_ref[...] = y


def _peer_dense(x2, ht, r2, e2, cnt, e1, u_tab, v_tab, final_g, final_norm, tb=512):
    n, d = x2.shape
    n_exp = u_tab.shape[0]
    eb = P_ROWS_PER_STEP * P_NKEYS
    big = pl.BlockSpec((P_HEADS, P_NKEYS, tb), lambda t, e: (0, 0, t))
    small = pl.BlockSpec((P_HEADS, P_ROWS_PER_STEP, tb), lambda t, e: (0, e, t))
    return pl.pallas_call(
        functools.partial(_peer_dense_kernel, final_norm=final_norm),
        out_shape=jax.ShapeDtypeStruct((n, d), F32),
        grid=(n // tb, n_exp // eb),
        in_specs=[
            pl.BlockSpec((d, tb), lambda t, e: (0, t)),
            big, big, small, small,
            pl.BlockSpec((eb, d), lambda t, e: (e, 0)),
            pl.BlockSpec((d, eb), lambda t, e: (0, e)),
            pl.BlockSpec((tb, d), lambda t, e: (t, 0)),
            pl.BlockSpec((1, d), lambda t, e: (0, 0)),
        ],
        out_specs=pl.BlockSpec((tb, d), lambda t, e: (t, 0)),
        scratch_shapes=[pltpu.VMEM((eb, tb), F32),
                        pltpu.VMEM((eb, tb), BF16),
                        pltpu.VMEM((d, tb), F32)],
        compiler_params=pltpu.CompilerParams(
            dimension_semantics=("arbitrary", "arbitrary"), vmem_limit_bytes=VMEM_LIMIT_BYTES),
        name="peer_dense",
    )(ht, r2, e2, cnt, e1, u_tab.astype(BF16), v_tab.T.astype(BF16), x2, final_g.reshape(1, d))


def _peer(x2, g, w_q, k1, k2, u_tab, v_tab, final_g, final_norm):
    ht, r2, e2, cnt, e1 = _peer_prep(x2, g, w_q, k1, k2)
    return _peer_dense(x2, ht, r2, e2, cnt, e1, u_tab, v_tab, final_g, final_norm)


def kernel(x, norm_mix, norm_ffn, final_norm, a_w_in, a_b_in, a_ln_g, a_ln_b, a_w_s, a_b_s, a_w_out,
           b_w_in, b_w_g2, b_b_g, b_norm, b_w_out, p_w_q, p_k1, p_k2, p_u, p_v):
    b, l, d = x.shape
    depth = norm_mix.shape[0]
    x2 = x.reshape(b * l, d)
    for i in range(depth):
        j = i // 2
        if i % 2 == 0:
            x2 = _mixer_a(x2, norm_mix[i], a_w_in[j], a_b_in[j], a_ln_g[j], a_ln_b[j],
                          a_w_s[j], a_b_s[j], a_w_out[j])
        else:
            x2 = _mixer_b(x2.reshape(b, l, d), norm_mix[i], b_w_in[j], b_w_g2[j], b_b_g[j],
                          b_norm[j], b_w_out[j]).reshape(b * l, d)
        x2 = _peer(x2, norm_ffn[i], p_w_q[i], p_k1[i], p_k2[i], p_u[i], p_v[i],
                   final_norm, i == depth - 1)
    return x2.reshape(b, l, d)
```

```python
import functools
import math

import jax
import jax.numpy as jnp
from jax import lax
from jax.experimental import pallas as pl
from jax.experimental.pallas import tpu as pltpu

F32 = jnp.float32
BF16 = jnp.bfloat16

NORM_EPS = 1e-6

LANES = 128
SUBLANES = 8
BF16_ROWS = 2 * SUBLANES
MXU_COLS = 256
VMEM_LIMIT_BYTES = 56 * 1024 * 1024

A_GROUPS = 8
A_CHUNK = 128
B_HEADS = 4
B_GATE_RANK = 16
B_GATE_NORM = 16.0
B_CHUNK = 64
P_HEADS = 8
P_NKEYS = 128
P_TOPK = 16
P_ROWS_PER_STEP = 8
RE_ROWS_PER_CHUNK = P_HEADS * BF16_ROWS
RE_ROWS = (P_NKEYS // BF16_ROWS) * RE_ROWS_PER_CHUNK


def _rms(x, g):
    return x * lax.rsqrt(jnp.mean(x * x, axis=-1, keepdims=True) + NORM_EPS) * g


def _gelu(x):
    return 0.5 * x * (1.0 + lax.erf(x * (1.0 / math.sqrt(2.0))))


def _split_bf16(x):
    hi = x.astype(BF16)
    lo = (x - hi.astype(F32)).astype(BF16)
    return hi, lo


def _pack_row_pairs(w):
    m2, n = w.shape
    pairs = w.astype(BF16).reshape(m2 // 2, 2, n).transpose(0, 2, 1)
    return lax.bitcast_convert_type(pairs, jnp.uint32)


def _const_spec(shape):
    nd = len(shape)
    return pl.BlockSpec(shape, lambda *_: (0,) * nd)


def _mixer_a_kernel(x_ref, g_ref, win_ref, bin_ref, lng_ref, lnb_ref, ws_ref, bs_ref, wout_ref,
                    o_ref, y_sc):
    tm = x_ref.shape[0]
    width = lng_ref.shape[1]
    gdim = width // A_GROUPS
    x = x_ref[...]
    h = _rms(x, g_ref[...]).astype(BF16)
    z = _gelu(jnp.dot(h, win_ref[...], preferred_element_type=F32) + bin_ref[...])
    u = z[:, :width]
    v = z[:, width:]
    mu = jnp.mean(v, axis=-1, keepdims=True)
    vc = v - mu
    var = jnp.mean(vc * vc, axis=-1, keepdims=True)
    vn = (vc * lax.rsqrt(var + NORM_EPS) * lng_ref[...] + lnb_ref[...]).astype(BF16)
    for c in range(tm // A_CHUNK):
        rows = slice(c * A_CHUNK, (c + 1) * A_CHUNK)
        for g in range(A_GROUPS):
            cols = slice(g * gdim, (g + 1) * gdim)
            sv = jnp.dot(ws_ref[g], vn[rows, cols], preferred_element_type=F32) + bs_ref[:, cols]
            y_sc[rows, cols] = (u[rows, cols] * sv).astype(BF16)
    o_ref[...] = x + jnp.dot(y_sc[...], wout_ref[...], preferred_element_type=F32)


def _mixer_a(x2, g, w_in, b_in, ln_g, ln_b, w_s, b_s, w_out, tm=256):
    n, d = x2.shape
    width = ln_g.shape[0]
    causal = jnp.tril(jnp.ones((A_CHUNK, A_CHUNK), dtype=bool))
    ws = jnp.where(causal[None], w_s, 0.0).astype(BF16)
    bs = jnp.repeat(b_s.T, width // A_GROUPS, axis=1)
    return pl.pallas_call(
        _mixer_a_kernel,
        out_shape=jax.ShapeDtypeStruct((n, d), F32),
        grid=(n // tm,),
        in_specs=[
            pl.BlockSpec((tm, d), lambda i: (i, 0)),
            _const_spec((1, d)),
            _const_spec((d, 2 * width)),
            _const_spec((1, 2 * width)),
            _const_spec((1, width)),
            _const_spec((1, width)),
            _const_spec((A_GROUPS, A_CHUNK, A_CHUNK)),
            _const_spec((A_CHUNK, width)),
            _const_spec((width, d)),
        ],
        out_specs=pl.BlockSpec((tm, d), lambda i: (i, 0)),
        scratch_shapes=[pltpu.VMEM((tm, width), BF16)],
        compiler_params=pltpu.CompilerParams(
            dimension_semantics=("arbitrary",), vmem_limit_bytes=VMEM_LIMIT_BYTES),
        name="mixer_a",
    )(x2, g.reshape(1, d), w_in.astype(BF16), b_in.reshape(1, -1), ln_g.reshape(1, -1),
      ln_b.reshape(1, -1), ws, bs, w_out.astype(BF16))


def _gla_kernel(x_ref, g_ref, win_ref, wg2_ref, bg_ref, ng_ref, wout_ref, tri_ref,
                o_ref, st_ref, oc_sc):
    tc = x_ref.shape[1]
    kdim = bg_ref.shape[1]
    vdim = ng_ref.shape[1]
    dk = kdim // B_HEADS
    dv = vdim // B_HEADS

    @pl.when(pl.program_id(1) == 0)
    def _():
        st_ref[...] = jnp.zeros_like(st_ref)

    x = x_ref[0]
    h = _rms(x, g_ref[...]).astype(BF16)
    proj = jnp.dot(h, win_ref[...], preferred_element_type=F32)
    q = proj[:, :kdim] * (dk ** -0.5)
    k = proj[:, kdim:2 * kdim]
    v = proj[:, 2 * kdim:2 * kdim + vdim]
    r = proj[:, 2 * kdim + vdim:2 * kdim + 2 * vdim]
    glr = proj[:, 2 * kdim + 2 * vdim:]

    g_hi, g_lo = _split_bf16(glr)
    w_hi = wg2_ref[0]
    w_lo = wg2_ref[1]
    xg = (jnp.dot(g_hi, w_hi, preferred_element_type=F32)
          + jnp.dot(g_hi, w_lo, preferred_element_type=F32)
          + jnp.dot(g_lo, w_hi, preferred_element_type=F32)) + bg_ref[...]
    log_a = (jnp.minimum(xg, 0.0) - jnp.log(1.0 + jnp.exp(-jnp.abs(xg)))) * (1.0 / B_GATE_NORM)

    a_hi, a_lo = _split_bf16(log_a)
    tri = tri_ref[...]
    gcum = (jnp.dot(tri, a_hi, preferred_element_type=F32)
            + jnp.dot(tri, a_lo, preferred_element_type=F32))

    eg = jnp.exp(gcum)
    qd = (q * eg).astype(BF16)
    kd = (k * jnp.exp(-gcum)).astype(BF16)
    vb = v.astype(BF16)
    row = lax.broadcasted_iota(jnp.int32, (B_CHUNK, B_CHUNK), 0)
    col = lax.broadcasted_iota(jnp.int32, (B_CHUNK, B_CHUNK), 1)
    causal = row >= col

    for c in range(tc // B_CHUNK):
        rows = slice(c * B_CHUNK, (c + 1) * B_CHUNK)
        last = (c + 1) * B_CHUNK - 1
        for hh in range(B_HEADS):
            kc = slice(hh * dk, (hh + 1) * dk)
            vc = slice(hh * dv, (hh + 1) * dv)
            g_last = gcum[last:last + 1, kc]
            ks = (k[rows, kc] * jnp.exp(g_last - gcum[rows, kc])).astype(BF16)
            st = st_ref[hh]
            sc = lax.dot_general(qd[rows, kc], kd[rows, kc], (((1,), (1,)), ((), ())),
                                 preferred_element_type=F32)
            sc = jnp.where(causal, sc, 0.0).astype(BF16)
            o = jnp.dot(sc, vb[rows, vc], preferred_element_type=F32)
            o = o + lax.dot_general(qd[rows, kc], st.astype(BF16), (((1,), (1,)), ((), ())),
                                    preferred_element_type=F32)
            oc_sc[rows, vc] = o
            upd = lax.dot_general(vb[rows, vc], ks, (((0,), (0,)), ((), ())),
                                  preferred_element_type=F32)
            st_ref[hh] = st * jnp.exp(g_last) + upd

    ng = ng_ref[...]
    for hh in range(B_HEADS):
        vc = slice(hh * dv, (hh + 1) * dv)
        oh = oc_sc[:, vc]
        oh = oh * lax.rsqrt(jnp.mean(oh * oh, axis=-1, keepdims=True) + NORM_EPS) * ng[:, vc]
        rh = r[:, vc]
        oc_sc[:, vc] = oh * (rh / (1.0 + jnp.exp(-rh)))
    o_ref[0] = x + jnp.dot(oc_sc[...].astype(BF16), wout_ref[...], preferred_element_type=F32)


def _mixer_b(x3, g, w_in, w_g2, b_g, norm_g, w_out, tc=256):
    b, l, d = x3.shape
    kdim = w_g2.shape[1]
    vdim = w_out.shape[0]
    main = 2 * kdim + 2 * vdim
    w_in_p = jnp.concatenate(
        [w_in, jnp.zeros((d, LANES - B_GATE_RANK), w_in.dtype)], axis=1).astype(BF16)
    w_g2_p = jnp.concatenate([w_g2, jnp.zeros((LANES - B_GATE_RANK, kdim), w_g2.dtype)], axis=0)
    hi = w_g2_p.astype(BF16)
    lo = (w_g2_p - hi.astype(F32)).astype(BF16)
    wg2 = jnp.stack([hi, lo])
    ng = jnp.tile(norm_g, B_HEADS).reshape(1, vdim)
    idx = jnp.arange(tc)
    tri = ((idx[:, None] >= idx[None, :]) &
           (idx[:, None] // B_CHUNK == idx[None, :] // B_CHUNK)).astype(BF16)
    return pl.pallas_call(
        _gla_kernel,
        out_shape=jax.ShapeDtypeStruct((b, l, d), F32),
        grid=(b, l // tc),
        in_specs=[
            pl.BlockSpec((1, tc, d), lambda i, j: (i, j, 0)),
            _const_spec((1, d)),
            _const_spec((d, main + LANES)),
            _const_spec((2, LANES, kdim)),
            _const_spec((1, kdim)),
            _const_spec((1, vdim)),
            _const_spec((vdim, d)),
            _const_spec((tc, tc)),
        ],
        out_specs=pl.BlockSpec((1, tc, d), lambda i, j: (i, j, 0)),
        scratch_shapes=[pltpu.VMEM((B_HEADS, vdim // B_HEADS, kdim // B_HEADS), F32),
                        pltpu.VMEM((tc, vdim), F32)],
        compiler_params=pltpu.CompilerParams(
            dimension_semantics=("arbitrary", "arbitrary"), vmem_limit_bytes=VMEM_LIMIT_BYTES),
        name="mixer_b_gla",
    )(x3, g.reshape(1, d), w_in_p, wg2, b_g.reshape(1, kdim), ng, w_out.astype(BF16), tri)


def _top_values(s, vals_ref, want_rank):
    cur = s
    rank = jnp.full(s.shape, float(P_TOPK), F32) if want_rank else None
    for r in range(P_TOPK):
        m = jnp.max(cur, axis=0, keepdims=True)
        hit = cur == m
        if want_rank:
            rank = jnp.where(hit, float(r), rank)
        cur = jnp.where(hit, -jnp.inf, cur)
        vals_ref[r:r + 1, :] = m
    return rank


def _bf16_pair_bits(x):
    bits = pltpu.bitcast(x.astype(BF16).astype(F32), jnp.uint32)
    return bits | (bits >> 16)


def _peer_prep_kernel(x_ref, g_ref, wq_ref, k1_ref, k2_ref,
                      ht_ref, re_ref, c_ref, e1_ref,
                      qt_sc, v1_sc, v2_sc, t_sc):
    tt = x_ref.shape[0]
    half = k1_ref.shape[1]
    h = _rms(x_ref[...], g_ref[...])
    ht = h.T.astype(BF16)
    ht_ref[...] = pltpu.bitcast(ht, jnp.uint32)
    qt_sc[...] = jnp.dot(wq_ref[...], ht, preferred_element_type=F32)
    k1 = k1_ref[...]
    k2 = k2_ref[...]

    def head(hh, carry):
        base = pl.multiple_of(hh * (2 * half), 2 * half)
        for lc in range(tt // LANES):
            lanes = slice(lc * LANES, (lc + 1) * LANES)
            q1 = qt_sc[pl.ds(base, half), lanes].astype(BF16)
            q2 = qt_sc[pl.ds(base + half, half), lanes].astype(BF16)
            s1 = jnp.dot(k1, q1, preferred_element_type=F32)
            s2 = jnp.dot(k2, q2, preferred_element_type=F32)
            _top_values(s1, v1_sc, False)
            rank2 = _top_values(s2, v2_sc, True)
            v1 = v1_sc[...]
            v2 = v2_sc[...]
            cands = [v1[0:1] + v2[0:8], v1[0:1] + v2[8:16]]
            cands += [v1[a:a + 1] + v2[0:8] for a in range(1, 8)]
            cands += [v1[8:16] + v2[0:1]]
            cand = jnp.concatenate(cands, axis=0)
            _top_values(cand, t_sc, False)
            top = t_sc[...]
            tau = top[P_TOPK - 1:P_TOPK]
            inv_z = 1.0 / jnp.sum(jnp.exp(top - top[0:1]), axis=0, keepdims=True)
            cnt = jnp.zeros(s1.shape, F32)
            for b in range(P_TOPK):
                cnt = cnt + jnp.where(s1 + v2[b:b + 1] >= tau, 1.0, 0.0)
            rank_w = pltpu.bitcast(rank2.astype(BF16), jnp.uint32)
            e2_w = pltpu.bitcast(jnp.exp(s2 - v2[0:1]).astype(BF16), jnp.uint32)
            for jc in range(P_NKEYS // BF16_ROWS):
                row = pl.multiple_of(jc * RE_ROWS_PER_CHUNK + hh * BF16_ROWS, BF16_ROWS)
                re_ref[lc, pl.ds(row, SUBLANES), :] = rank_w[jc * SUBLANES:(jc + 1) * SUBLANES]
                re_ref[lc, pl.ds(row + SUBLANES, SUBLANES), :] = e2_w[jc * SUBLANES:(jc + 1) * SUBLANES]
            c_ref[hh, :, lanes] = _bf16_pair_bits(cnt)
            e1_ref[hh, :, lanes] = _bf16_pair_bits(jnp.exp(s1 - v1[0:1]) * inv_z)
        return carry

    lax.fori_loop(0, P_HEADS, head, 0)


def _peer_prep(x2, g, w_q, k1, k2, tt=256):
    n, d = x2.shape
    qdim = w_q.shape[1]
    bshape = jax.ShapeDtypeStruct((n // LANES, RE_ROWS, LANES), jnp.uint32)
    bspec = pl.BlockSpec((tt // LANES, RE_ROWS, LANES), lambda i: (i, 0, 0))
    wshape = jax.ShapeDtypeStruct((P_HEADS, P_NKEYS, n), jnp.uint32)
    gspec = pl.BlockSpec((P_HEADS, P_NKEYS, tt), lambda i: (0, 0, i))
    return pl.pallas_call(
        _peer_prep_kernel,
        out_shape=(jax.ShapeDtypeStruct((d // 2, n), jnp.uint32), bshape, wshape, wshape),
        grid=(n // tt,),
        in_specs=[
            pl.BlockSpec((tt, d), lambda i: (i, 0)),
            _const_spec((1, d)),
            _const_spec((qdim, d)),
            _const_spec(k1.shape),
            _const_spec(k2.shape),
        ],
        out_specs=(pl.BlockSpec((d // 2, tt), lambda i: (0, i)), bspec, gspec, gspec),
        scratch_shapes=[pltpu.VMEM((qdim, tt), F32),
                        pltpu.VMEM((P_TOPK, LANES), F32),
                        pltpu.VMEM((P_TOPK, LANES), F32),
                        pltpu.VMEM((P_TOPK, LANES), F32)],
        compiler_params=pltpu.CompilerParams(
            dimension_semantics=("arbitrary",), vmem_limit_bytes=VMEM_LIMIT_BYTES),
        name="peer_prep",
    )(x2, g.reshape(1, d), w_q.T.astype(BF16), k1.astype(BF16), k2.astype(BF16))


def _gate_times_act(re_ref, c_ref, e1_ref, a_ref, w_ref, blocks):
    for key1_rows, lc in blocks:
        lanes = slice(lc * LANES, (lc + 1) * LANES)
        cnt = {(i, hh): pltpu.bitcast(jnp.broadcast_to(c_ref[hh, i:i + 1, lanes], (SUBLANES, LANES)), BF16)
               for i in key1_rows for hh in range(P_HEADS)}
        e1 = {(i, hh): pltpu.bitcast(jnp.broadcast_to(e1_ref[hh, i:i + 1, lanes], (SUBLANES, LANES)), BF16)
              for i in key1_rows for hh in range(P_HEADS)}
        for jc in range(P_NKEYS // BF16_ROWS):
            gate = {i: None for i in key1_rows}
            for hh in range(P_HEADS):
                row = jc * RE_ROWS_PER_CHUNK + hh * BF16_ROWS
                rank2 = pltpu.bitcast(re_ref[lc, row:row + SUBLANES, :], BF16)
                e2 = pltpu.bitcast(re_ref[lc, row + SUBLANES:row + BF16_ROWS, :], BF16)
                for i in key1_rows:
                    term = jnp.where(rank2 < cnt[i, hh], e2, 0.0) * e1[i, hh]
                    gate[i] = term if gate[i] is None else gate[i] + term
            for i in key1_rows:
                rows = slice(i * P_NKEYS + jc * BF16_ROWS, i * P_NKEYS + (jc + 1) * BF16_ROWS)
                w_ref[rows, lanes] = gate[i] * _gelu(a_ref[rows, lanes]).astype(BF16)


def _peer_dense_kernel(ht_ref, re_ref, c_ref, e1_ref, u_ref, vt_ref, x_ref, fg_ref,
                       o_ref, a0_sc, a1_sc, w0_sc, w1_sc, acc_sc, *, n_exp_tiles, n_tiles, final_norm):
    k = pl.program_id(0)
    c_pair = jnp.clip(k - 2, 0, n_tiles - 1)
    c_exp = c_pair % n_exp_tiles

    @pl.when(k == 0)
    def _():
        a1_sc[...] = jnp.zeros_like(a1_sc)
        w0_sc[...] = jnp.zeros_like(w0_sc)

    @pl.when(c_exp == 0)
    def _():
        acc_sc[...] = jnp.zeros_like(acc_sc)

    def stages(a_out, a_in, w_out, w_in):
        tb = ht_ref.shape[1]
        eb = a_in.shape[0]
        d = acc_sc.shape[0]
        n_chunks = tb // MXU_COLS
        blocks = [((i, i + 1), lc) for i in range(0, P_ROWS_PER_STEP, 2) for lc in range(tb // LANES)]
        n_units = n_chunks * (eb // MXU_COLS + d // MXU_COLS)
        per_unit = len(blocks) // n_units
        todo = iter(range(0, len(blocks), per_unit))

        def vpu_slice():
            b0 = next(todo)
            _gate_times_act(re_ref, c_ref, e1_ref, a_in, w_out, blocks[b0:b0 + per_unit])

        for n in range(n_chunks):
            cols = slice(n * MXU_COLS, (n + 1) * MXU_COLS)
            t = acc_sc[:, cols]
            for kt in range(eb // MXU_COLS):
                ks = slice(kt * MXU_COLS, (kt + 1) * MXU_COLS)
                t = t + jnp.dot(pltpu.bitcast(vt_ref[:, ks], BF16), w_in[ks, cols],
                                preferred_element_type=F32)
                vpu_slice()
            acc_sc[:, cols] = t
            t = None
            for kt in range(d // MXU_COLS):
                ks = slice(kt * MXU_COLS, (kt + 1) * MXU_COLS)
                kw = slice(kt * MXU_COLS // 2, (kt + 1) * MXU_COLS // 2)
                p = jnp.dot(pltpu.bitcast(u_ref[:, ks], BF16), pltpu.bitcast(ht_ref[kw, cols], BF16),
                            preferred_element_type=F32)
                t = p if t is None else t + p
                vpu_slice()
            a_out[:, cols] = t

    @pl.when(k % 2 == 0)
    def _():
        stages(a0_sc, a1_sc, w1_sc, w0_sc)

    @pl.when(k % 2 == 1)
    def _():
        stages(a1_sc, a0_sc, w0_sc, w1_sc)

    @pl.when(jnp.logical_and(k >= 2, c_exp == n_exp_tiles - 1))
    def _():
        y = x_ref[...] + acc_sc[...].T
        if final_norm:
            y = _rms(y, fg_ref[...])
        o_ref[...] = y


def _peer_dense(x2, ht, re, cnt, e1, u_tab, v_tab, final_g, final_norm, tb=512):
    n, d = x2.shape
    n_exp = u_tab.shape[0]
    eb = P_ROWS_PER_STEP * P_NKEYS
    tb = min(tb, n)
    ne = n_exp // eb
    n_tiles = (n // tb) * ne
    last = n_tiles - 1

    def pair_a(k):
        return jnp.minimum(k, last)

    def pair_b(k):
        return jnp.clip(k - 1, 0, last)

    def pair_c(k):
        return jnp.clip(k - 2, 0, last)

    big = pl.BlockSpec((tb // LANES, RE_ROWS, LANES), lambda k: (pair_b(k) // ne, 0, 0))
    small = pl.BlockSpec((P_HEADS, P_ROWS_PER_STEP, tb), lambda k: (0, pair_b(k) % ne, pair_b(k) // ne))
    return pl.pallas_call(
        functools.partial(_peer_dense_kernel, n_exp_tiles=ne, n_tiles=n_tiles, final_norm=final_norm),
        out_shape=jax.ShapeDtypeStruct((n, d), F32),
        grid=(n_tiles + 2,),
        in_specs=[
            pl.BlockSpec((d // 2, tb), lambda k: (0, pair_a(k) // ne)),
            big, small, small,
            pl.BlockSpec((eb // 2, d), lambda k: (pair_a(k) % ne, 0)),
            pl.BlockSpec((d // 2, eb), lambda k: (0, pair_c(k) % ne)),
            pl.BlockSpec((tb, d), lambda k: (pair_c(k) // ne, 0)),
            pl.BlockSpec((1, d), lambda k: (0, 0)),
        ],
        out_specs=pl.BlockSpec((tb, d), lambda k: (pair_c(k) // ne, 0)),
        scratch_shapes=[pltpu.VMEM((eb, tb), F32), pltpu.VMEM((eb, tb), F32),
                        pltpu.VMEM((eb, tb), BF16), pltpu.VMEM((eb, tb), BF16),
                        pltpu.VMEM((d, tb), F32)],
        compiler_params=pltpu.CompilerParams(
            dimension_semantics=("arbitrary",), vmem_limit_bytes=VMEM_LIMIT_BYTES),
        name="peer_dense",
    )(ht, re, cnt, e1, _pack_row_pairs(u_tab), _pack_row_pairs(v_tab.T), x2, final_g.reshape(1, d))


def _peer(x2, g, w_q, k1, k2, u_tab, v_tab, final_g, final_norm):
    ht, re, cnt, e1 = _peer_prep(x2, g, w_q, k1, k2)
    return _peer_dense(x2, ht, re, cnt, e1, u_tab, v_tab, final_g, final_norm)


def kernel(x, norm_mix, norm_ffn, final_norm, a_w_in, a_b_in, a_ln_g, a_ln_b, a_w_s, a_b_s, a_w_out,
           b_w_in, b_w_g2, b_b_g, b_norm, b_w_out, p_w_q, p_k1, p_k2, p_u, p_v):
    b, l, d = x.shape
    depth = norm_mix.shape[0]
    x2 = x.reshape(b * l, d)
    for i in range(depth):
        j = i // 2
        if i % 2 == 0:
            x2 = _mixer_a(x2, norm_mix[i], a_w_in[j], a_b_in[j], a_ln_g[j], a_ln_b[j],
                          a_w_s[j], a_b_s[j], a_w_out[j])
        else:
            x2 = _mixer_b(x2.reshape(b, l, d), norm_mix[i], b_w_in[j], b_w_g2[j], b_b_g[j],
                          b_norm[j], b_w_out[j]).reshape(b * l, d)
        x2 = _peer(x2, norm_ffn[i], p_w_q[i], p_k1[i], p_k2[i], p_u[i], p_v[i],
                   final_norm, i == depth - 1)
    return x2.reshape(b, l, d)
```

```python
import functools
import math

import jax
import jax.numpy as jnp
from jax import lax
from jax.experimental import pallas as pl
from jax.experimental.pallas import tpu as pltpu

F32 = jnp.float32
BF16 = jnp.bfloat16

NORM_EPS = 1e-6

LANES = 128
SUBLANES = 8
BF16_ROWS = 2 * SUBLANES
MXU_COLS = 256
VMEM_LIMIT_BYTES = 56 * 1024 * 1024

A_GROUPS = 8
A_CHUNK = 128
B_HEADS = 4
B_GATE_RANK = 16
B_GATE_NORM = 16.0
B_CHUNK = 64
P_HEADS = 8
P_NKEYS = 128
P_TOPK = 16
P_ROWS_PER_STEP = 8
PREP_SLABS = SUBLANES


def _rms(x, g):
    return x * lax.rsqrt(jnp.mean(x * x, axis=-1, keepdims=True) + NORM_EPS) * g


def _gelu(x):
    return 0.5 * x * (1.0 + lax.erf(x * (1.0 / math.sqrt(2.0))))


def _split_bf16(x):
    hi = x.astype(BF16)
    lo = (x - hi.astype(F32)).astype(BF16)
    return hi, lo


def _pack_row_pairs(w):
    m2, n = w.shape
    pairs = w.astype(BF16).reshape(m2 // 2, 2, n).transpose(0, 2, 1)
    return lax.bitcast_convert_type(pairs, jnp.uint32)


def _const_spec(shape):
    nd = len(shape)
    return pl.BlockSpec(shape, lambda *_: (0,) * nd)


def _mixer_a_kernel(x_ref, g_ref, win_ref, bin_ref, lng_ref, lnb_ref, ws_ref, bs_ref, wout_ref,
                    o_ref, y_sc):
    tm = x_ref.shape[0]
    width = lng_ref.shape[1]
    gdim = width // A_GROUPS
    x = x_ref[...]
    h = _rms(x, g_ref[...]).astype(BF16)
    z = _gelu(jnp.dot(h, win_ref[...], preferred_element_type=F32) + bin_ref[...])
    u = z[:, :width]
    v = z[:, width:]
    mu = jnp.mean(v, axis=-1, keepdims=True)
    vc = v - mu
    var = jnp.mean(vc * vc, axis=-1, keepdims=True)
    vn = (vc * lax.rsqrt(var + NORM_EPS) * lng_ref[...] + lnb_ref[...]).astype(BF16)
    for c in range(tm // A_CHUNK):
        rows = slice(c * A_CHUNK, (c + 1) * A_CHUNK)
        for g in range(A_GROUPS):
            cols = slice(g * gdim, (g + 1) * gdim)
            sv = jnp.dot(ws_ref[g], vn[rows, cols], preferred_element_type=F32) + bs_ref[:, cols]
            y_sc[rows, cols] = (u[rows, cols] * sv).astype(BF16)
    o_ref[...] = x + jnp.dot(y_sc[...], wout_ref[...], preferred_element_type=F32)


def _mixer_a(x2, g, w_in, b_in, ln_g, ln_b, w_s, b_s, w_out, tm=256):
    n, d = x2.shape
    width = ln_g.shape[0]
    causal = jnp.tril(jnp.ones((A_CHUNK, A_CHUNK), dtype=bool))
    ws = jnp.where(causal[None], w_s, 0.0).astype(BF16)
    bs = jnp.repeat(b_s.T, width // A_GROUPS, axis=1)
    return pl.pallas_call(
        _mixer_a_kernel,
        out_shape=jax.ShapeDtypeStruct((n, d), F32),
        grid=(n // tm,),
        in_specs=[
            pl.BlockSpec((tm, d), lambda i: (i, 0)),
            _const_spec((1, d)),
            _const_spec((d, 2 * width)),
            _const_spec((1, 2 * width)),
            _const_spec((1, width)),
            _const_spec((1, width)),
            _const_spec((A_GROUPS, A_CHUNK, A_CHUNK)),
            _const_spec((A_CHUNK, width)),
            _const_spec((width, d)),
        ],
        out_specs=pl.BlockSpec((tm, d), lambda i: (i, 0)),
        scratch_shapes=[pltpu.VMEM((tm, width), BF16)],
        compiler_params=pltpu.CompilerParams(
            dimension_semantics=("arbitrary",), vmem_limit_bytes=VMEM_LIMIT_BYTES),
        name="mixer_a",
    )(x2, g.reshape(1, d), w_in.astype(BF16), b_in.reshape(1, -1), ln_g.reshape(1, -1),
      ln_b.reshape(1, -1), ws, bs, w_out.astype(BF16))


def _gla_kernel(x_ref, g_ref, win_ref, wg2_ref, bg_ref, ng_ref, wout_ref, tri_ref,
                o_ref, st_ref, oc_sc):
    tc = x_ref.shape[1]
    kdim = bg_ref.shape[1]
    vdim = ng_ref.shape[1]
    dk = kdim // B_HEADS
    dv = vdim // B_HEADS

    @pl.when(pl.program_id(1) == 0)
    def _():
        st_ref[...] = jnp.zeros_like(st_ref)

    x = x_ref[0]
    h = _rms(x, g_ref[...]).astype(BF16)
    proj = jnp.dot(h, win_ref[...], preferred_element_type=F32)
    q = proj[:, :kdim] * (dk ** -0.5)
    k = proj[:, kdim:2 * kdim]
    v = proj[:, 2 * kdim:2 * kdim + vdim]
    r = proj[:, 2 * kdim + vdim:2 * kdim + 2 * vdim]
    glr = proj[:, 2 * kdim + 2 * vdim:]

    g_hi, g_lo = _split_bf16(glr)
    w_hi = wg2_ref[0]
    w_lo = wg2_ref[1]
    xg = (jnp.dot(g_hi, w_hi, preferred_element_type=F32)
          + jnp.dot(g_hi, w_lo, preferred_element_type=F32)
          + jnp.dot(g_lo, w_hi, preferred_element_type=F32)) + bg_ref[...]
    log_a = (jnp.minimum(xg, 0.0) - jnp.log(1.0 + jnp.exp(-jnp.abs(xg)))) * (1.0 / B_GATE_NORM)

    a_hi, a_lo = _split_bf16(log_a)
    tri = tri_ref[...]
    gcum = (jnp.dot(tri, a_hi, preferred_element_type=F32)
            + jnp.dot(tri, a_lo, preferred_element_type=F32))

    eg = jnp.exp(gcum)
    qd = (q * eg).astype(BF16)
    kd = (k * jnp.exp(-gcum)).astype(BF16)
    vb = v.astype(BF16)
    row = lax.broadcasted_iota(jnp.int32, (B_CHUNK, B_CHUNK), 0)
    col = lax.broadcasted_iota(jnp.int32, (B_CHUNK, B_CHUNK), 1)
    causal = row >= col

    for c in range(tc // B_CHUNK):
        rows = slice(c * B_CHUNK, (c + 1) * B_CHUNK)
        last = (c + 1) * B_CHUNK - 1
        for hh in range(B_HEADS):
            kc = slice(hh * dk, (hh + 1) * dk)
            vc = slice(hh * dv, (hh + 1) * dv)
            g_last = gcum[last:last + 1, kc]
            ks = (k[rows, kc] * jnp.exp(g_last - gcum[rows, kc])).astype(BF16)
            st = st_ref[hh]
            sc = lax.dot_general(qd[rows, kc], kd[rows, kc], (((1,), (1,)), ((), ())),
                                 preferred_element_type=F32)
            sc = jnp.where(causal, sc, 0.0).astype(BF16)
            o = jnp.dot(sc, vb[rows, vc], preferred_element_type=F32)
            o = o + lax.dot_general(qd[rows, kc], st.astype(BF16), (((1,), (1,)), ((), ())),
                                    preferred_element_type=F32)
            oc_sc[rows, vc] = o
            upd = lax.dot_general(vb[rows, vc], ks, (((0,), (0,)), ((), ())),
                                  preferred_element_type=F32)
            st_ref[hh] = st * jnp.exp(g_last) + upd

    ng = ng_ref[...]
    for hh in range(B_HEADS):
        vc = slice(hh * dv, (hh + 1) * dv)
        oh = oc_sc[:, vc]
        oh = oh * lax.rsqrt(jnp.mean(oh * oh, axis=-1, keepdims=True) + NORM_EPS) * ng[:, vc]
        rh = r[:, vc]
        oc_sc[:, vc] = oh * (rh / (1.0 + jnp.exp(-rh)))
    o_ref[0] = x + jnp.dot(oc_sc[...].astype(BF16), wout_ref[...], preferred_element_type=F32)


def _mixer_b(x3, g, w_in, w_g2, b_g, norm_g, w_out, tc=256):
    b, l, d = x3.shape
    kdim = w_g2.shape[1]
    vdim = w_out.shape[0]
    main = 2 * kdim + 2 * vdim
    w_in_p = jnp.concatenate(
        [w_in, jnp.zeros((d, LANES - B_GATE_RANK), w_in.dtype)], axis=1).astype(BF16)
    w_g2_p = jnp.concatenate([w_g2, jnp.zeros((LANES - B_GATE_RANK, kdim), w_g2.dtype)], axis=0)
    hi = w_g2_p.astype(BF16)
    lo = (w_g2_p - hi.astype(F32)).astype(BF16)
    wg2 = jnp.stack([hi, lo])
    ng = jnp.tile(norm_g, B_HEADS).reshape(1, vdim)
    idx = jnp.arange(tc)
    tri = ((idx[:, None] >= idx[None, :]) &
           (idx[:, None] // B_CHUNK == idx[None, :] // B_CHUNK)).astype(BF16)
    return pl.pallas_call(
        _gla_kernel,
        out_shape=jax.ShapeDtypeStruct((b, l, d), F32),
        grid=(b, l // tc),
        in_specs=[
            pl.BlockSpec((1, tc, d), lambda i, j: (i, j, 0)),
            _const_spec((1, d)),
            _const_spec((d, main + LANES)),
            _const_spec((2, LANES, kdim)),
            _const_spec((1, kdim)),
            _const_spec((1, vdim)),
            _const_spec((vdim, d)),
            _const_spec((tc, tc)),
        ],
        out_specs=pl.BlockSpec((1, tc, d), lambda i, j: (i, j, 0)),
        scratch_shapes=[pltpu.VMEM((B_HEADS, vdim // B_HEADS, kdim // B_HEADS), F32),
                        pltpu.VMEM((tc, vdim), F32)],
        compiler_params=pltpu.CompilerParams(
            dimension_semantics=("arbitrary", "arbitrary"), vmem_limit_bytes=VMEM_LIMIT_BYTES),
        name="mixer_b_gla",
    )(x3, g.reshape(1, d), w_in_p, wg2, b_g.reshape(1, kdim), ng, w_out.astype(BF16), tri)


def _sort16_pairs():
    pairs = []

    def merge(lo, n, r):
        step = 2 * r
        if step < n:
            merge(lo, n, step)
            merge(lo + r, n, step)
            pairs.extend((i, i + r) for i in range(lo + r, lo + n - r, step))
        else:
            pairs.append((lo, lo + r))

    def sort(lo, n):
        if n > 1:
            sort(lo, n // 2)
            sort(lo + n // 2, n // 2)
            merge(lo, n, 1)

    sort(0, P_TOPK)
    return pairs


def _sort16(vals):
    vals = list(vals)
    for i, j in _sort16_pairs():
        vals[i], vals[j] = jnp.maximum(vals[i], vals[j]), jnp.minimum(vals[i], vals[j])
    return vals


def _merge_top16(a, b):
    c = list(a)
    for m, bv in enumerate(b):
        i = P_TOPK - 1 - m
        c[i] = jnp.maximum(c[i], bv)
    dist = P_TOPK // 2
    while dist:
        for i in range(P_TOPK):
            if not i & dist:
                c[i], c[i + dist] = jnp.maximum(c[i], c[i + dist]), jnp.minimum(c[i], c[i + dist])
        dist //= 2
    return c


def _top16_of_keys(s_sc, lo, hi):
    if hi - lo == P_TOPK:
        return _sort16([s_sc[k * SUBLANES:(k + 1) * SUBLANES, :] for k in range(lo, hi)])
    mid = (lo + hi) // 2
    return _merge_top16(_top16_of_keys(s_sc, lo, mid), _top16_of_keys(s_sc, mid, hi))


def _count_true_prefix(pred, thr):
    steps = (8, 4, 2, 1)
    masks = []

    def pick(ms, cands):
        if not ms:
            return cands[0]
        half = len(cands) // 2
        return jnp.where(ms[0], pick(ms[1:], cands[half:]), pick(ms[1:], cands[:half]))

    for lvl, step in enumerate(steps):
        cands = []
        for combo in range(2 ** lvl):
            taken = sum(s for bit, s in enumerate(steps[:lvl]) if (combo >> (lvl - 1 - bit)) & 1)
            cands.append(thr[taken + step - 1])
        masks.append(pred(pick(masks, cands)))
    count = None
    for m, step in zip(masks, steps):
        term = jnp.where(m, float(step), 0.0)
        count = term if count is None else count + term
    return jnp.where(pred(thr[P_TOPK - 1]), float(P_TOPK), count)


def _bf16_bits(x):
    return pltpu.bitcast(x.astype(BF16).astype(F32), jnp.uint32)


def _bf16_pair_bits(x):
    bits = _bf16_bits(x)
    return bits | (bits >> 16)


def _to_key_major(s, s_sc):
    for g in range(s.shape[0] // SUBLANES):
        for c in range(PREP_SLABS):
            s_sc[pl.ds(g * SUBLANES * PREP_SLABS + c, SUBLANES, stride=PREP_SLABS), :] = (
                s[g * SUBLANES:(g + 1) * SUBLANES, c * LANES:(c + 1) * LANES])


def _peer_prep_kernel(x_ref, g_ref, wq_ref, k1_ref, k2_ref,
                      ht_ref, re_ref, c_ref, e1_ref,
                      ht_sc, qt_sc, s1_sc, s2_sc, rw_sc, ew_sc):
    hh = pl.program_id(1)
    half = 2 * k1_ref.shape[0]

    @pl.when(hh == 0)
    def _():
        h = _rms(x_ref[...], g_ref[...])
        ht = h.T.astype(BF16)
        ht_ref[...] = pltpu.bitcast(ht, jnp.uint32)
        ht_sc[...] = ht

    qt_sc[...] = jnp.dot(pltpu.bitcast(wq_ref[0], BF16), ht_sc[...],
                         preferred_element_type=F32).astype(BF16)
    s1 = jnp.dot(pltpu.bitcast(k1_ref[...], BF16), qt_sc[:half, :], preferred_element_type=F32)
    s2 = jnp.dot(pltpu.bitcast(k2_ref[...], BF16), qt_sc[half:, :], preferred_element_type=F32)
    _to_key_major(s1, s1_sc)
    _to_key_major(s2, s2_sc)

    v1 = _top16_of_keys(s1_sc, 0, P_NKEYS)
    v2 = _top16_of_keys(s2_sc, 0, P_NKEYS)

    top = [v1[0] + v2[b] for b in range(P_TOPK)]
    for a in range(1, P_TOPK // 2):
        top = _merge_top16(top, [v1[a] + v2[b] for b in range(P_TOPK // (a + 1))])
    top = _merge_top16(top, [v1[a] + v2[0] for a in range(P_TOPK // 2, P_TOPK)])
    tau = top[P_TOPK - 1]
    z = None
    for m in range(P_TOPK):
        term = jnp.exp(top[m] - top[0])
        z = term if z is None else z + term
    inv_z = 1.0 / z

    for p in range(P_NKEYS // 2):
        rank_bits = []
        e2_bits = []
        for k in (2 * p, 2 * p + 1):
            tile = slice(k * SUBLANES, (k + 1) * SUBLANES)
            x2 = s2_sc[tile, :]
            rank_bits.append(_bf16_bits(_count_true_prefix(lambda t, x2=x2: t > x2, v2)))
            e2_bits.append(_bf16_bits(jnp.exp(x2 - v2[0])))
            x1 = s1_sc[tile, :]
            cnt = _count_true_prefix(lambda t, x1=x1: x1 + t >= tau, v2)
            c_ref[0, 0, k] = _bf16_pair_bits(cnt)
            e1_ref[0, 0, k] = _bf16_pair_bits(jnp.exp(x1 - v1[0]) * inv_z)
        words = slice(p * SUBLANES, (p + 1) * SUBLANES)
        rw_sc[words, :] = (rank_bits[0] >> 16) | rank_bits[1]
        ew_sc[words, :] = (e2_bits[0] >> 16) | e2_bits[1]

    for jc in range(P_NKEYS // BF16_ROWS):
        for c in range(PREP_SLABS):
            rows = pl.ds(jc * SUBLANES * PREP_SLABS + c, SUBLANES, stride=PREP_SLABS)
            re_ref[c, jc, 0, 0:SUBLANES, :] = rw_sc[rows, :]
            re_ref[c, jc, 0, SUBLANES:BF16_ROWS, :] = ew_sc[rows, :]


def _peer_prep(x2, g, w_q, k1, k2):
    n, d = x2.shape
    tt = PREP_SLABS * LANES
    qdim = w_q.shape[1] // P_HEADS
    chunks = P_NKEYS // BF16_ROWS
    re_shape = jax.ShapeDtypeStruct((n // LANES, chunks, P_HEADS, BF16_ROWS, LANES), jnp.uint32)
    re_spec = pl.BlockSpec((PREP_SLABS, chunks, 1, BF16_ROWS, LANES), lambda t, h: (t, 0, h, 0, 0))
    row_shape = jax.ShapeDtypeStruct((n // tt, P_HEADS, P_NKEYS, PREP_SLABS, LANES), jnp.uint32)
    row_spec = pl.BlockSpec((1, 1, P_NKEYS, PREP_SLABS, LANES), lambda t, h: (t, h, 0, 0, 0))
    wq = _pack_row_pairs(w_q.T).reshape(P_HEADS, qdim // 2, d)
    return pl.pallas_call(
        _peer_prep_kernel,
        out_shape=(jax.ShapeDtypeStruct((d // 2, n), jnp.uint32), re_shape, row_shape, row_shape),
        grid=(n // tt, P_HEADS),
        in_specs=[
            pl.BlockSpec((tt, d), lambda t, h: (t, 0)),
            _const_spec((1, d)),
            pl.BlockSpec((1, qdim // 2, d), lambda t, h: (h, 0, 0)),
            _const_spec((P_NKEYS // 2, k1.shape[1])),
            _const_spec((P_NKEYS // 2, k2.shape[1])),
        ],
        out_specs=(pl.BlockSpec((d // 2, tt), lambda t, h: (0, t)), re_spec, row_spec, row_spec),
        scratch_shapes=[pltpu.VMEM((d, tt), BF16),
                        pltpu.VMEM((qdim, tt), BF16),
                        pltpu.VMEM((P_NKEYS * PREP_SLABS, LANES), F32),
                        pltpu.VMEM((P_NKEYS * PREP_SLABS, LANES), F32),
                        pltpu.VMEM((P_NKEYS // 2 * PREP_SLABS, LANES), jnp.uint32),
                        pltpu.VMEM((P_NKEYS // 2 * PREP_SLABS, LANES), jnp.uint32)],
        compiler_params=pltpu.CompilerParams(
            dimension_semantics=("arbitrary", "arbitrary"), vmem_limit_bytes=VMEM_LIMIT_BYTES),
        name="peer_prep",
    )(x2, g.reshape(1, d), wq, _pack_row_pairs(k1), _pack_row_pairs(k2))


def _gate_times_act(re_ref, c_ref, e1_ref, a_ref, w_ref, slab0, blocks):
    for key1_rows, lc in blocks:
        lanes = slice(lc * LANES, (lc + 1) * LANES)
        slab = pl.ds(slab0 + lc, 1)

        def row_tile(ref, hh, i):
            return pltpu.bitcast(jnp.broadcast_to(ref[0, hh, i, slab, :], (SUBLANES, LANES)), BF16)

        cnt = {(i, hh): row_tile(c_ref, hh, i) for i in key1_rows for hh in range(P_HEADS)}
        e1 = {(i, hh): row_tile(e1_ref, hh, i) for i in key1_rows for hh in range(P_HEADS)}
        for jc in range(P_NKEYS // BF16_ROWS):
            gate = {i: None for i in key1_rows}
            for hh in range(P_HEADS):
                rank2 = pltpu.bitcast(re_ref[lc, jc, hh, 0:SUBLANES, :], BF16)
                e2 = pltpu.bitcast(re_ref[lc, jc, hh, SUBLANES:BF16_ROWS, :], BF16)
                for i in key1_rows:
                    term = jnp.where(rank2 < cnt[i, hh], e2, 0.0) * e1[i, hh]
                    gate[i] = term if gate[i] is None else gate[i] + term
            for i in key1_rows:
                rows = slice(i * P_NKEYS + jc * BF16_ROWS, i * P_NKEYS + (jc + 1) * BF16_ROWS)
                w_ref[rows, lanes] = gate[i] * _gelu(a_ref[rows, lanes]).astype(BF16)


def _peer_dense_kernel(ht_ref, re_ref, c_ref, e1_ref, u_ref, vt_ref, x_ref, fg_ref,
                       o_ref, a0_sc, a1_sc, w0_sc, w1_sc, acc_sc, *, n_exp_tiles, n_tiles, final_norm):
    k = pl.program_id(0)
    c_pair = jnp.clip(k - 2, 0, n_tiles - 1)
    c_exp = c_pair % n_exp_tiles

    @pl.when(k == 0)
    def _():
        a1_sc[...] = jnp.zeros_like(a1_sc)
        w0_sc[...] = jnp.zeros_like(w0_sc)

    @pl.when(c_exp == 0)
    def _():
        acc_sc[...] = jnp.zeros_like(acc_sc)

    def stages(a_out, a_in, w_out, w_in):
        tb = ht_ref.shape[1]
        eb = a_in.shape[0]
        d = acc_sc.shape[0]
        n_chunks = tb // MXU_COLS
        blocks = [((i, i + 1), lc) for i in range(0, P_ROWS_PER_STEP, 2) for lc in range(tb // LANES)]
        n_units = n_chunks * (eb // MXU_COLS + d // MXU_COLS)
        per_unit = len(blocks) // n_units
        todo = iter(range(0, len(blocks), per_unit))
        b_tok = jnp.clip(k - 1, 0, n_tiles - 1) // n_exp_tiles
        slab0 = (b_tok % (PREP_SLABS * LANES // tb)) * (tb // LANES)

        def vpu_slice():
            b0 = next(todo)
            _gate_times_act(re_ref, c_ref, e1_ref, a_in, w_out, slab0, blocks[b0:b0 + per_unit])

        for n in range(n_chunks):
            cols = slice(n * MXU_COLS, (n + 1) * MXU_COLS)
            t = acc_sc[:, cols]
            for kt in range(eb // MXU_COLS):
                ks = slice(kt * MXU_COLS, (kt + 1) * MXU_COLS)
                t = t + jnp.dot(pltpu.bitcast(vt_ref[:, ks], BF16), w_in[ks, cols],
                                preferred_element_type=F32)
                vpu_slice()
            acc_sc[:, cols] = t
            t = None
            for kt in range(d // MXU_COLS):
                ks = slice(kt * MXU_COLS, (kt + 1) * MXU_COLS)
                kw = slice(kt * MXU_COLS // 2, (kt + 1) * MXU_COLS // 2)
                p = jnp.dot(pltpu.bitcast(u_ref[:, ks], BF16), pltpu.bitcast(ht_ref[kw, cols], BF16),
                            preferred_element_type=F32)
                t = p if t is None else t + p
                vpu_slice()
            a_out[:, cols] = t

    @pl.when(k % 2 == 0)
    def _():
        stages(a0_sc, a1_sc, w1_sc, w0_sc)

    @pl.when(k % 2 == 1)
    def _():
        stages(a1_sc, a0_sc, w0_sc, w1_sc)

    @pl.when(jnp.logical_and(k >= 2, c_exp == n_exp_tiles - 1))
    def _():
        y = x_ref[...] + acc_sc[...].T
        if final_norm:
            y = _rms(y, fg_ref[...])
        o_ref[...] = y


def _peer_dense(x2, ht, re, cnt, e1, u_tab, v_tab, final_g, final_norm, tb=512):
    n, d = x2.shape
    n_exp = u_tab.shape[0]
    eb = P_ROWS_PER_STEP * P_NKEYS
    tb = min(tb, n)
    ne = n_exp // eb
    n_tiles = (n // tb) * ne
    last = n_tiles - 1

    def pair_a(k):
        return jnp.minimum(k, last)

    def pair_b(k):
        return jnp.clip(k - 1, 0, last)

    def pair_c(k):
        return jnp.clip(k - 2, 0, last)

    tiles_per_row_tile = PREP_SLABS * LANES // tb
    big = pl.BlockSpec((tb // LANES, P_NKEYS // BF16_ROWS, P_HEADS, BF16_ROWS, LANES),
                       lambda k: (pair_b(k) // ne, 0, 0, 0, 0))
    small = pl.BlockSpec((1, P_HEADS, P_ROWS_PER_STEP, PREP_SLABS, LANES),
                         lambda k: (pair_b(k) // ne // tiles_per_row_tile, 0, pair_b(k) % ne, 0, 0))
    return pl.pallas_call(
        functools.partial(_peer_dense_kernel, n_exp_tiles=ne, n_tiles=n_tiles, final_norm=final_norm),
        out_shape=jax.ShapeDtypeStruct((n, d), F32),
        grid=(n_tiles + 2,),
        in_specs=[
            pl.BlockSpec((d // 2, tb), lambda k: (0, pair_a(k) // ne)),
            big, small, small,
            pl.BlockSpec((eb // 2, d), lambda k: (pair_a(k) % ne, 0)),
            pl.BlockSpec((d // 2, eb), lambda k: (0, pair_c(k) % ne)),
            pl.BlockSpec((tb, d), lambda k: (pair_c(k) // ne, 0)),
            pl.BlockSpec((1, d), lambda k: (0, 0)),
        ],
        out_specs=pl.BlockSpec((tb, d), lambda k: (pair_c(k) // ne, 0)),
        scratch_shapes=[pltpu.VMEM((eb, tb), F32), pltpu.VMEM((eb, tb), F32),
                        pltpu.VMEM((eb, tb), BF16), pltpu.VMEM((eb, tb), BF16),
                        pltpu.VMEM((d, tb), F32)],
        compiler_params=pltpu.CompilerParams(
            dimension_semantics=("arbitrary",), vmem_limit_bytes=VMEM_LIMIT_BYTES),
        name="peer_dense",
    )(ht, re, cnt, e1, _pack_row_pairs(u_tab), _pack_row_pairs(v_tab.T), x2, final_g.reshape(1, d))


def _peer(x2, g, w_q, k1, k2, u_tab, v_tab, final_g, final_norm):
    ht, re, cnt, e1 = _peer_prep(x2, g, w_q, k1, k2)
    return _peer_dense(x2, ht, re, cnt, e1, u_tab, v_tab, final_g, final_norm)


def kernel(x, norm_mix, norm_ffn, final_norm, a_w_in, a_b_in, a_ln_g, a_ln_b, a_w_s, a_b_s, a_w_out,
           b_w_in, b_w_g2, b_b_g, b_norm, b_w_out, p_w_q, p_k1, p_k2, p_u, p_v):
    b, l, d = x.shape
    depth = norm_mix.shape[0]
    x2 = x.reshape(b * l, d)
    for i in range(depth):
        j = i // 2
        if i % 2 == 0:
            x2 = _mixer_a(x2, norm_mix[i], a_w_in[j], a_b_in[j], a_ln_g[j], a_ln_b[j],
                          a_w_s[j], a_b_s[j], a_w_out[j])
        else:
            x2 = _mixer_b(x2.reshape(b, l, d), norm_mix[i], b_w_in[j], b_w_g2[j], b_b_g[j],
                          b_norm[j], b_w_out[j]).reshape(b * l, d)
        x2 = _peer(x2, norm_ffn[i], p_w_q[i], p_k1[i], p_k2[i], p_u[i], p_v[i],
                   final_norm, i == depth - 1)
    return x2.reshape(b, l, d)
```

```python
import functools
import math

import jax
import jax.numpy as jnp
from jax import lax
from jax.experimental import pallas as pl
from jax.experimental.pallas import tpu as pltpu

F32 = jnp.float32
BF16 = jnp.bfloat16

NORM_EPS = 1e-6

LANES = 128
SUBLANES = 8
BF16_ROWS = 2 * SUBLANES
MXU_COLS = 256
VMEM_LIMIT_BYTES = 56 * 1024 * 1024

A_GROUPS = 8
A_CHUNK = 128
B_HEADS = 4
B_GATE_RANK = 16
B_GATE_NORM = 16.0
B_CHUNK = 64
P_HEADS = 8
P_NKEYS = 128
P_TOPK = 16
P_ROWS_PER_STEP = 8
PREP_SLABS = SUBLANES


def _rms(x, g):
    return x * lax.rsqrt(jnp.mean(x * x, axis=-1, keepdims=True) + NORM_EPS) * g


def _gelu(x):
    return 0.5 * x * (1.0 + lax.erf(x * (1.0 / math.sqrt(2.0))))


def _split_bf16(x):
    hi = x.astype(BF16)
    lo = (x - hi.astype(F32)).astype(BF16)
    return hi, lo


def _pack_row_pairs(w):
    m2, n = w.shape
    bits = lax.bitcast_convert_type(w.astype(BF16).astype(F32), jnp.uint32).reshape(m2 // 2, 2, n)
    return (bits[:, 0, :] >> 16) | bits[:, 1, :]


def _pack_col_pairs_t(w):
    m, n2 = w.shape
    words = lax.bitcast_convert_type(w.astype(BF16).reshape(m, n2 // 2, 2), jnp.uint32)
    return words.T


def _const_spec(shape):
    nd = len(shape)
    return pl.BlockSpec(shape, lambda *_: (0,) * nd)


def _mixer_a_kernel(x_ref, g_ref, win_ref, bin_ref, lng_ref, lnb_ref, ws_ref, bs_ref, wout_ref,
                    o_ref, y_sc):
    tm = x_ref.shape[0]
    width = lng_ref.shape[1]
    gdim = width // A_GROUPS
    x = x_ref[...]
    h = _rms(x, g_ref[...]).astype(BF16)
    z = _gelu(jnp.dot(h, win_ref[...], preferred_element_type=F32) + bin_ref[...])
    u = z[:, :width]
    v = z[:, width:]
    mu = jnp.mean(v, axis=-1, keepdims=True)
    vc = v - mu
    var = jnp.mean(vc * vc, axis=-1, keepdims=True)
    vn = (vc * lax.rsqrt(var + NORM_EPS) * lng_ref[...] + lnb_ref[...]).astype(BF16)
    for c in range(tm // A_CHUNK):
        rows = slice(c * A_CHUNK, (c + 1) * A_CHUNK)
        for g in range(A_GROUPS):
            cols = slice(g * gdim, (g + 1) * gdim)
            sv = jnp.dot(ws_ref[g], vn[rows, cols], preferred_element_type=F32) + bs_ref[:, cols]
            y_sc[rows, cols] = (u[rows, cols] * sv).astype(BF16)
    o_ref[...] = x + jnp.dot(y_sc[...], wout_ref[...], preferred_element_type=F32)


def _mixer_a(x2, g, w_in, b_in, ln_g, ln_b, w_s, b_s, w_out, tm=256):
    n, d = x2.shape
    width = ln_g.shape[0]
    causal = jnp.tril(jnp.ones((A_CHUNK, A_CHUNK), dtype=bool))
    ws = jnp.where(causal[None], w_s, 0.0).astype(BF16)
    bs = jnp.repeat(b_s.T, width // A_GROUPS, axis=1)
    return pl.pallas_call(
        _mixer_a_kernel,
        out_shape=jax.ShapeDtypeStruct((n, d), F32),
        grid=(n // tm,),
        in_specs=[
            pl.BlockSpec((tm, d), lambda i: (i, 0)),
            _const_spec((1, d)),
            _const_spec((d, 2 * width)),
            _const_spec((1, 2 * width)),
            _const_spec((1, width)),
            _const_spec((1, width)),
            _const_spec((A_GROUPS, A_CHUNK, A_CHUNK)),
            _const_spec((A_CHUNK, width)),
            _const_spec((width, d)),
        ],
        out_specs=pl.BlockSpec((tm, d), lambda i: (i, 0)),
        scratch_shapes=[pltpu.VMEM((tm, width), BF16)],
        compiler_params=pltpu.CompilerParams(
            dimension_semantics=("arbitrary",), vmem_limit_bytes=VMEM_LIMIT_BYTES),
        name="mixer_a",
    )(x2, g.reshape(1, d), w_in.astype(BF16), b_in.reshape(1, -1), ln_g.reshape(1, -1),
      ln_b.reshape(1, -1), ws, bs, w_out.astype(BF16))


def _gla_kernel(x_ref, g_ref, win_ref, wg2_ref, bg_ref, ng_ref, wout_ref, tri_ref,
                o_ref, st_ref, oc_sc):
    tc = x_ref.shape[1]
    kdim = bg_ref.shape[1]
    vdim = ng_ref.shape[1]
    dk = kdim // B_HEADS
    dv = vdim // B_HEADS

    @pl.when(pl.program_id(1) == 0)
    def _():
        st_ref[...] = jnp.zeros_like(st_ref)

    x = x_ref[0]
    h = _rms(x, g_ref[...]).astype(BF16)
    proj = jnp.dot(h, win_ref[...], preferred_element_type=F32)
    q = proj[:, :kdim] * (dk ** -0.5)
    k = proj[:, kdim:2 * kdim]
    v = proj[:, 2 * kdim:2 * kdim + vdim]
    r = proj[:, 2 * kdim + vdim:2 * kdim + 2 * vdim]
    glr = proj[:, 2 * kdim + 2 * vdim:]

    g_hi, g_lo = _split_bf16(glr)
    w_hi = wg2_ref[0]
    w_lo = wg2_ref[1]
    xg = (jnp.dot(g_hi, w_hi, preferred_element_type=F32)
          + jnp.dot(g_hi, w_lo, preferred_element_type=F32)
          + jnp.dot(g_lo, w_hi, preferred_element_type=F32)) + bg_ref[...]
    log_a = (jnp.minimum(xg, 0.0) - jnp.log(1.0 + jnp.exp(-jnp.abs(xg)))) * (1.0 / B_GATE_NORM)

    a_hi, a_lo = _split_bf16(log_a)
    tri = tri_ref[...]
    gcum = (jnp.dot(tri, a_hi, preferred_element_type=F32)
            + jnp.dot(tri, a_lo, preferred_element_type=F32))

    eg = jnp.exp(gcum)
    qd = (q * eg).astype(BF16)
    kd = (k * jnp.exp(-gcum)).astype(BF16)
    vb = v.astype(BF16)
    row = lax.broadcasted_iota(jnp.int32, (B_CHUNK, B_CHUNK), 0)
    col = lax.broadcasted_iota(jnp.int32, (B_CHUNK, B_CHUNK), 1)
    causal = row >= col

    for c in range(tc // B_CHUNK):
        rows = slice(c * B_CHUNK, (c + 1) * B_CHUNK)
        last = (c + 1) * B_CHUNK - 1
        for hh in range(B_HEADS):
            kc = slice(hh * dk, (hh + 1) * dk)
            vc = slice(hh * dv, (hh + 1) * dv)
            g_last = gcum[last:last + 1, kc]
            ks = (k[rows, kc] * jnp.exp(g_last - gcum[rows, kc])).astype(BF16)
            st = st_ref[hh]
            sc = lax.dot_general(qd[rows, kc], kd[rows, kc], (((1,), (1,)), ((), ())),
                                 preferred_element_type=F32)
            sc = jnp.where(causal, sc, 0.0).astype(BF16)
            o = jnp.dot(sc, vb[rows, vc], preferred_element_type=F32)
            o = o + lax.dot_general(qd[rows, kc], st.astype(BF16), (((1,), (1,)), ((), ())),
                                    preferred_element_type=F32)
            oc_sc[rows, vc] = o
            upd = lax.dot_general(vb[rows, vc], ks, (((0,), (0,)), ((), ())),
                                  preferred_element_type=F32)
            st_ref[hh] = st * jnp.exp(g_last) + upd

    ng = ng_ref[...]
    for hh in range(B_HEADS):
        vc = slice(hh * dv, (hh + 1) * dv)
        oh = oc_sc[:, vc]
        oh = oh * lax.rsqrt(jnp.mean(oh * oh, axis=-1, keepdims=True) + NORM_EPS) * ng[:, vc]
        rh = r[:, vc]
        oc_sc[:, vc] = oh * (rh / (1.0 + jnp.exp(-rh)))
    o_ref[0] = x + jnp.dot(oc_sc[...].astype(BF16), wout_ref[...], preferred_element_type=F32)


def _mixer_b(x3, g, w_in, w_g2, b_g, norm_g, w_out, tc=256):
    b, l, d = x3.shape
    kdim = w_g2.shape[1]
    vdim = w_out.shape[0]
    main = 2 * kdim + 2 * vdim
    w_in_p = jnp.concatenate(
        [w_in, jnp.zeros((d, LANES - B_GATE_RANK), w_in.dtype)], axis=1).astype(BF16)
    w_g2_p = jnp.concatenate([w_g2, jnp.zeros((LANES - B_GATE_RANK, kdim), w_g2.dtype)], axis=0)
    hi = w_g2_p.astype(BF16)
    lo = (w_g2_p - hi.astype(F32)).astype(BF16)
    wg2 = jnp.stack([hi, lo])
    ng = jnp.tile(norm_g, B_HEADS).reshape(1, vdim)
    idx = jnp.arange(tc)
    tri = ((idx[:, None] >= idx[None, :]) &
           (idx[:, None] // B_CHUNK == idx[None, :] // B_CHUNK)).astype(BF16)
    return pl.pallas_call(
        _gla_kernel,
        out_shape=jax.ShapeDtypeStruct((b, l, d), F32),
        grid=(b, l // tc),
        in_specs=[
            pl.BlockSpec((1, tc, d), lambda i, j: (i, j, 0)),
            _const_spec((1, d)),
            _const_spec((d, main + LANES)),
            _const_spec((2, LANES, kdim)),
            _const_spec((1, kdim)),
            _const_spec((1, vdim)),
            _const_spec((vdim, d)),
            _const_spec((tc, tc)),
        ],
        out_specs=pl.BlockSpec((1, tc, d), lambda i, j: (i, j, 0)),
        scratch_shapes=[pltpu.VMEM((B_HEADS, vdim // B_HEADS, kdim // B_HEADS), F32),
                        pltpu.VMEM((tc, vdim), F32)],
        compiler_params=pltpu.CompilerParams(
            dimension_semantics=("arbitrary", "arbitrary"), vmem_limit_bytes=VMEM_LIMIT_BYTES),
        name="mixer_b_gla",
    )(x3, g.reshape(1, d), w_in_p, wg2, b_g.reshape(1, kdim), ng, w_out.astype(BF16), tri)


def _sort16_pairs():
    pairs = []

    def merge(lo, n, r):
        step = 2 * r
        if step < n:
            merge(lo, n, step)
            merge(lo + r, n, step)
            pairs.extend((i, i + r) for i in range(lo + r, lo + n - r, step))
        else:
            pairs.append((lo, lo + r))

    def sort(lo, n):
        if n > 1:
            sort(lo, n // 2)
            sort(lo + n // 2, n // 2)
            merge(lo, n, 1)

    sort(0, P_TOPK)
    return pairs


def _sort16(vals):
    vals = list(vals)
    for i, j in _sort16_pairs():
        vals[i], vals[j] = jnp.maximum(vals[i], vals[j]), jnp.minimum(vals[i], vals[j])
    return vals


def _merge_top16(a, b):
    c = list(a)
    for m, bv in enumerate(b):
        i = P_TOPK - 1 - m
        c[i] = jnp.maximum(c[i], bv)
    dist = P_TOPK // 2
    while dist:
        for i in range(P_TOPK):
            if not i & dist:
                c[i], c[i + dist] = jnp.maximum(c[i], c[i + dist]), jnp.minimum(c[i], c[i + dist])
        dist //= 2
    return c


def _top16_of_keys(s_sc, lo, hi):
    if hi - lo == P_TOPK:
        return _sort16([s_sc[k * SUBLANES:(k + 1) * SUBLANES, :] for k in range(lo, hi)])
    mid = (lo + hi) // 2
    return _merge_top16(_top16_of_keys(s_sc, lo, mid), _top16_of_keys(s_sc, mid, hi))


def _count_true_prefix(pred, thr):
    steps = (8, 4, 2, 1)
    masks = []

    def pick(ms, cands):
        if not ms:
            return cands[0]
        half = len(cands) // 2
        return jnp.where(ms[0], pick(ms[1:], cands[half:]), pick(ms[1:], cands[:half]))

    for lvl, step in enumerate(steps):
        cands = []
        for combo in range(2 ** lvl):
            taken = sum(s for bit, s in enumerate(steps[:lvl]) if (combo >> (lvl - 1 - bit)) & 1)
            cands.append(thr[taken + step - 1])
        masks.append(pred(pick(masks, cands)))
    count = None
    for m, step in zip(masks, steps):
        term = jnp.where(m, float(step), 0.0)
        count = term if count is None else count + term
    return jnp.where(pred(thr[P_TOPK - 1]), float(P_TOPK), count)


def _bf16_bits(x):
    return pltpu.bitcast(x.astype(BF16).astype(F32), jnp.uint32)


def _bf16_pair_bits(x):
    bits = _bf16_bits(x)
    return bits | (bits >> 16)


def _to_key_major(s, s_sc):
    for g in range(s.shape[0] // SUBLANES):
        for c in range(PREP_SLABS):
            s_sc[pl.ds(g * SUBLANES * PREP_SLABS + c, SUBLANES, stride=PREP_SLABS), :] = (
                s[g * SUBLANES:(g + 1) * SUBLANES, c * LANES:(c + 1) * LANES])


def _peer_prep_kernel(x_ref, g_ref, wq_ref, k1_ref, k2_ref,
                      ht_ref, re_ref, c_ref, e1_ref,
                      ht_sc, qt_sc, s1_sc, s2_sc, rw_sc, ew_sc):
    hh = pl.program_id(1)
    half = 2 * k1_ref.shape[0]

    @pl.when(hh == 0)
    def _():
        h = _rms(x_ref[...], g_ref[...])
        ht = h.T.astype(BF16)
        ht_ref[...] = pltpu.bitcast(ht, jnp.uint32)
        ht_sc[...] = ht

    qt_sc[...] = jnp.dot(pltpu.bitcast(wq_ref[0], BF16), ht_sc[...],
                         preferred_element_type=F32).astype(BF16)
    s1 = jnp.dot(pltpu.bitcast(k1_ref[...], BF16), qt_sc[:half, :], preferred_element_type=F32)
    s2 = jnp.dot(pltpu.bitcast(k2_ref[...], BF16), qt_sc[half:, :], preferred_element_type=F32)
    _to_key_major(s1, s1_sc)
    _to_key_major(s2, s2_sc)

    v1 = _top16_of_keys(s1_sc, 0, P_NKEYS)
    v2 = _top16_of_keys(s2_sc, 0, P_NKEYS)

    top = [v1[0] + v2[b] for b in range(P_TOPK)]
    for a in range(1, P_TOPK // 2):
        top = _merge_top16(top, [v1[a] + v2[b] for b in range(P_TOPK // (a + 1))])
    top = _merge_top16(top, [v1[a] + v2[0] for a in range(P_TOPK // 2, P_TOPK)])
    tau = top[P_TOPK - 1]
    z = None
    for m in range(P_TOPK):
        term = jnp.exp(top[m] - top[0])
        z = term if z is None else z + term
    inv_z = 1.0 / z

    for p in range(P_NKEYS // 2):
        rank_bits = []
        e2_bits = []
        for k in (2 * p, 2 * p + 1):
            tile = slice(k * SUBLANES, (k + 1) * SUBLANES)
            x2 = s2_sc[tile, :]
            rank_bits.append(_bf16_bits(_count_true_prefix(lambda t, x2=x2: t > x2, v2)))
            e2_bits.append(_bf16_bits(jnp.exp(x2 - v2[0])))
            x1 = s1_sc[tile, :]
            cnt = _count_true_prefix(lambda t, x1=x1: x1 + t >= tau, v2)
            c_ref[0, 0, k] = _bf16_pair_bits(cnt)
            e1_ref[0, 0, k] = _bf16_pair_bits(jnp.exp(x1 - v1[0]) * inv_z)
        words = slice(p * SUBLANES, (p + 1) * SUBLANES)
        rw_sc[words, :] = (rank_bits[0] >> 16) | rank_bits[1]
        ew_sc[words, :] = (e2_bits[0] >> 16) | e2_bits[1]

    for jc in range(P_NKEYS // BF16_ROWS):
        for c in range(PREP_SLABS):
            rows = pl.ds(jc * SUBLANES * PREP_SLABS + c, SUBLANES, stride=PREP_SLABS)
            re_ref[c, jc, 0, 0:SUBLANES, :] = rw_sc[rows, :]
            re_ref[c, jc, 0, SUBLANES:BF16_ROWS, :] = ew_sc[rows, :]


def _peer_prep(x2, g, w_q, k1, k2):
    n, d = x2.shape
    tt = PREP_SLABS * LANES
    qdim = w_q.shape[1] // P_HEADS
    chunks = P_NKEYS // BF16_ROWS
    re_shape = jax.ShapeDtypeStruct((n // LANES, chunks, P_HEADS, BF16_ROWS, LANES), jnp.uint32)
    re_spec = pl.BlockSpec((PREP_SLABS, chunks, 1, BF16_ROWS, LANES), lambda t, h: (t, 0, h, 0, 0))
    row_shape = jax.ShapeDtypeStruct((n // tt, P_HEADS, P_NKEYS, PREP_SLABS, LANES), jnp.uint32)
    row_spec = pl.BlockSpec((1, 1, P_NKEYS, PREP_SLABS, LANES), lambda t, h: (t, h, 0, 0, 0))
    wq = _pack_row_pairs(w_q.T).reshape(P_HEADS, qdim // 2, d)
    return pl.pallas_call(
        _peer_prep_kernel,
        out_shape=(jax.ShapeDtypeStruct((d // 2, n), jnp.uint32), re_shape, row_shape, row_shape),
        grid=(n // tt, P_HEADS),
        in_specs=[
            pl.BlockSpec((tt, d), lambda t, h: (t, 0)),
            _const_spec((1, d)),
            pl.BlockSpec((1, qdim // 2, d), lambda t, h: (h, 0, 0)),
            _const_spec((P_NKEYS // 2, k1.shape[1])),
            _const_spec((P_NKEYS // 2, k2.shape[1])),
        ],
        out_specs=(pl.BlockSpec((d // 2, tt), lambda t, h: (0, t)), re_spec, row_spec, row_spec),
        scratch_shapes=[pltpu.VMEM((d, tt), BF16),
                        pltpu.VMEM((qdim, tt), BF16),
                        pltpu.VMEM((P_NKEYS * PREP_SLABS, LANES), F32),
                        pltpu.VMEM((P_NKEYS * PREP_SLABS, LANES), F32),
                        pltpu.VMEM((P_NKEYS // 2 * PREP_SLABS, LANES), jnp.uint32),
                        pltpu.VMEM((P_NKEYS // 2 * PREP_SLABS, LANES), jnp.uint32)],
        compiler_params=pltpu.CompilerParams(
            dimension_semantics=("arbitrary", "arbitrary"), vmem_limit_bytes=VMEM_LIMIT_BYTES),
        name="peer_prep",
    )(x2, g.reshape(1, d), wq, _pack_row_pairs(k1), _pack_row_pairs(k2))


def _gate_times_act(re_ref, c_ref, e1_ref, a_ref, w_ref, slab0, blocks):
    for key1_rows, lc in blocks:
        lanes = slice(lc * LANES, (lc + 1) * LANES)
        slab = pl.ds(slab0 + lc, 1)

        def row_tile(ref, hh, i):
            return pltpu.bitcast(jnp.broadcast_to(ref[0, hh, i, slab, :], (SUBLANES, LANES)), BF16)

        cnt = {(i, hh): row_tile(c_ref, hh, i) for i in key1_rows for hh in range(P_HEADS)}
        e1 = {(i, hh): row_tile(e1_ref, hh, i) for i in key1_rows for hh in range(P_HEADS)}
        for jc in range(P_NKEYS // BF16_ROWS):
            gate = {i: None for i in key1_rows}
            for hh in range(P_HEADS):
                rank2 = pltpu.bitcast(re_ref[lc, jc, hh, 0:SUBLANES, :], BF16)
                e2 = pltpu.bitcast(re_ref[lc, jc, hh, SUBLANES:BF16_ROWS, :], BF16)
                for i in key1_rows:
                    term = jnp.where(rank2 < cnt[i, hh], e2, 0.0) * e1[i, hh]
                    gate[i] = term if gate[i] is None else gate[i] + term
            for i in key1_rows:
                rows = slice(i * P_NKEYS + jc * BF16_ROWS, i * P_NKEYS + (jc + 1) * BF16_ROWS)
                w_ref[rows, lanes] = gate[i] * _gelu(a_ref[rows, lanes]).astype(BF16)


def _peer_dense_kernel(ht_ref, re_ref, c_ref, e1_ref, u_ref, vt_ref, x_ref, fg_ref,
                       o_ref, a0_sc, a1_sc, w0_sc, w1_sc, acc_sc, *, n_exp_tiles, n_tiles, final_norm):
    k = pl.program_id(0)
    c_pair = jnp.clip(k - 2, 0, n_tiles - 1)
    c_exp = c_pair % n_exp_tiles

    @pl.when(k == 0)
    def _():
        a1_sc[...] = jnp.zeros_like(a1_sc)
        w0_sc[...] = jnp.zeros_like(w0_sc)

    @pl.when(c_exp == 0)
    def _():
        acc_sc[...] = jnp.zeros_like(acc_sc)

    def stages(a_out, a_in, w_out, w_in):
        tb = ht_ref.shape[1]
        eb = a_in.shape[0]
        d = acc_sc.shape[0]
        n_chunks = tb // MXU_COLS
        blocks = [((i, i + 1), lc) for i in range(0, P_ROWS_PER_STEP, 2) for lc in range(tb // LANES)]
        n_units = n_chunks * (eb // MXU_COLS + d // MXU_COLS)
        per_unit = len(blocks) // n_units
        todo = iter(range(0, len(blocks), per_unit))
        tiles_per_row_tile = PREP_SLABS * LANES // tb
        if tiles_per_row_tile == 1:
            slab0 = 0
        else:
            b_tok = jnp.clip(k - 1, 0, n_tiles - 1) // n_exp_tiles
            slab0 = (b_tok % tiles_per_row_tile) * (tb // LANES)

        def vpu_slice():
            b0 = next(todo)
            _gate_times_act(re_ref, c_ref, e1_ref, a_in, w_out, slab0, blocks[b0:b0 + per_unit])

        for n in range(n_chunks):
            cols = slice(n * MXU_COLS, (n + 1) * MXU_COLS)
            t = acc_sc[:, cols]
            for kt in range(eb // MXU_COLS):
                ks = slice(kt * MXU_COLS, (kt + 1) * MXU_COLS)
                t = t + jnp.dot(pltpu.bitcast(vt_ref[:, ks], BF16), w_in[ks, cols],
                                preferred_element_type=F32)
                vpu_slice()
            acc_sc[:, cols] = t
            t = None
            for kt in range(d // MXU_COLS):
                ks = slice(kt * MXU_COLS, (kt + 1) * MXU_COLS)
                kw = slice(kt * MXU_COLS // 2, (kt + 1) * MXU_COLS // 2)
                p = jnp.dot(pltpu.bitcast(u_ref[:, ks], BF16), pltpu.bitcast(ht_ref[kw, cols], BF16),
                            preferred_element_type=F32)
                t = p if t is None else t + p
                vpu_slice()
            a_out[:, cols] = t

    @pl.when(k % 2 == 0)
    def _():
        stages(a0_sc, a1_sc, w1_sc, w0_sc)

    @pl.when(k % 2 == 1)
    def _():
        stages(a1_sc, a0_sc, w0_sc, w1_sc)

    @pl.when(jnp.logical_and(k >= 2, c_exp == n_exp_tiles - 1))
    def _():
        y = x_ref[...] + acc_sc[...].T
        if final_norm:
            y = _rms(y, fg_ref[...])
        o_ref[...] = y


def _peer_dense(x2, ht, re, cnt, e1, u_tab, v_tab, final_g, final_norm, tb=1024):
    n, d = x2.shape
    n_exp = u_tab.shape[0]
    eb = P_ROWS_PER_STEP * P_NKEYS
    tb = min(tb, n)
    ne = n_exp // eb
    n_tiles = (n // tb) * ne
    last = n_tiles - 1

    def pair_a(k):
        return jnp.minimum(k, last)

    def pair_b(k):
        return jnp.clip(k - 1, 0, last)

    def pair_c(k):
        return jnp.clip(k - 2, 0, last)

    tiles_per_row_tile = PREP_SLABS * LANES // tb
    once = pl.Buffered(1)
    big = pl.BlockSpec((tb // LANES, P_NKEYS // BF16_ROWS, P_HEADS, BF16_ROWS, LANES),
                       lambda k: (pair_b(k) // ne, 0, 0, 0, 0), pipeline_mode=once)
    small = pl.BlockSpec((1, P_HEADS, P_ROWS_PER_STEP, PREP_SLABS, LANES),
                         lambda k: (pair_b(k) // ne // tiles_per_row_tile, 0, pair_b(k) % ne, 0, 0))
    return pl.pallas_call(
        functools.partial(_peer_dense_kernel, n_exp_tiles=ne, n_tiles=n_tiles, final_norm=final_norm),
        out_shape=jax.ShapeDtypeStruct((n, d), F32),
        grid=(n_tiles + 2,),
        in_specs=[
            pl.BlockSpec((d // 2, tb), lambda k: (0, pair_a(k) // ne), pipeline_mode=once),
            big, small, small,
            pl.BlockSpec((eb // 2, d), lambda k: (pair_a(k) % ne, 0)),
            pl.BlockSpec((d // 2, eb), lambda k: (0, pair_c(k) % ne)),
            pl.BlockSpec((tb, d), lambda k: (pair_c(k) // ne, 0), pipeline_mode=once),
            pl.BlockSpec((1, d), lambda k: (0, 0)),
        ],
        out_specs=pl.BlockSpec((tb, d), lambda k: (pair_c(k) // ne, 0)),
        scratch_shapes=[pltpu.VMEM((eb, tb), F32), pltpu.VMEM((eb, tb), F32),
                        pltpu.VMEM((eb, tb), BF16), pltpu.VMEM((eb, tb), BF16),
                        pltpu.VMEM((d, tb), F32)],
        compiler_params=pltpu.CompilerParams(
            dimension_semantics=("arbitrary",), vmem_limit_bytes=VMEM_LIMIT_BYTES),
        name="peer_dense",
    )(ht, re, cnt, e1, _pack_row_pairs(u_tab), _pack_col_pairs_t(v_tab), x2, final_g.reshape(1, d))


def _peer(x2, g, w_q, k1, k2, u_tab, v_tab, final_g, final_norm):
    ht, re, cnt, e1 = _peer_prep(x2, g, w_q, k1, k2)
    return _peer_dense(x2, ht, re, cnt, e1, u_tab, v_tab, final_g, final_norm)


def kernel(x, norm_mix, norm_ffn, final_norm, a_w_in, a_b_in, a_ln_g, a_ln_b, a_w_s, a_b_s, a_w_out,
           b_w_in, b_w_g2, b_b_g, b_norm, b_w_out, p_w_q, p_k1, p_k2, p_u, p_v):
    b, l, d = x.shape
    depth = norm_mix.shape[0]
    x2 = x.reshape(b * l, d)
    for i in range(depth):
        j = i // 2
        if i % 2 == 0:
            x2 = _mixer_a(x2, norm_mix[i], a_w_in[j], a_b_in[j], a_ln_g[j], a_ln_b[j],
                          a_w_s[j], a_b_s[j], a_w_out[j])
        else:
            x2 = _mixer_b(x2.reshape(b, l, d), norm_mix[i], b_w_in[j], b_w_g2[j], b_b_g[j],
                          b_norm[j], b_w_out[j]).reshape(b * l, d)
        x2 = _peer(x2, norm_ffn[i], p_w_q[i], p_k1[i], p_k2[i], p_u[i], p_v[i],
                   final_norm, i == depth - 1)
    return x2.reshape(b, l, d)
```

```python
import functools
import math

import jax
import jax.numpy as jnp
from jax import lax
from jax.experimental import pallas as pl
from jax.experimental.pallas import tpu as pltpu

F32 = jnp.float32
BF16 = jnp.bfloat16

NORM_EPS = 1e-6

LANES = 128
SUBLANES = 8
BF16_ROWS = 2 * SUBLANES
MXU_COLS = 256
VMEM_LIMIT_BYTES = 56 * 1024 * 1024

A_GROUPS = 8
A_CHUNK = 128
B_HEADS = 4
B_GATE_RANK = 16
B_GATE_NORM = 16.0
B_CHUNK = 64
P_HEADS = 8
P_NKEYS = 128
P_TOPK = 16
P_ROWS_PER_STEP = 8
PREP_SLABS = SUBLANES


def _rms(x, g):
    return x * lax.rsqrt(jnp.mean(x * x, axis=-1, keepdims=True) + NORM_EPS) * g


def _gelu(x):
    return 0.5 * x * (1.0 + lax.erf(x * (1.0 / math.sqrt(2.0))))


def _split_bf16(x):
    hi = x.astype(BF16)
    lo = (x - hi.astype(F32)).astype(BF16)
    return hi, lo


def _pack_row_pairs(w):
    def bf16_bits(rows):
        return lax.bitcast_convert_type(rows.astype(BF16).astype(F32), jnp.uint32)

    return (bf16_bits(w[0::2]) >> 16) | bf16_bits(w[1::2])


def _const_spec(shape):
    nd = len(shape)
    return pl.BlockSpec(shape, lambda *_: (0,) * nd)


def _mixer_a_kernel(x_ref, g_ref, win_ref, bin_ref, lng_ref, lnb_ref, ws_ref, bs_ref, wout_ref,
                    o_ref, y_sc):
    tm = x_ref.shape[0]
    width = lng_ref.shape[1]
    gdim = width // A_GROUPS
    x = x_ref[...]
    h = _rms(x, g_ref[...]).astype(BF16)
    z = _gelu(jnp.dot(h, pltpu.bitcast(win_ref[...], BF16), preferred_element_type=F32) + bin_ref[...])
    u = z[:, :width]
    v = z[:, width:]
    mu = jnp.mean(v, axis=-1, keepdims=True)
    vc = v - mu
    var = jnp.mean(vc * vc, axis=-1, keepdims=True)
    vn = (vc * lax.rsqrt(var + NORM_EPS) * lng_ref[...] + lnb_ref[...]).astype(BF16)
    for c in range(tm // A_CHUNK):
        rows = slice(c * A_CHUNK, (c + 1) * A_CHUNK)
        for g in range(A_GROUPS):
            cols = slice(g * gdim, (g + 1) * gdim)
            sv = jnp.dot(ws_ref[g], vn[rows, cols], preferred_element_type=F32) + bs_ref[:, cols]
            y_sc[rows, cols] = (u[rows, cols] * sv).astype(BF16)
    o_ref[...] = x + jnp.dot(y_sc[...], pltpu.bitcast(wout_ref[...], BF16), preferred_element_type=F32)


def _mixer_a(x2, g, w_in, b_in, ln_g, ln_b, w_s, b_s, w_out, tm=256):
    n, d = x2.shape
    width = ln_g.shape[0]
    causal = jnp.tril(jnp.ones((A_CHUNK, A_CHUNK), dtype=bool))
    ws = jnp.where(causal[None], w_s, 0.0).astype(BF16)
    bs = jnp.repeat(b_s.T, width // A_GROUPS, axis=1)
    return pl.pallas_call(
        _mixer_a_kernel,
        out_shape=jax.ShapeDtypeStruct((n, d), F32),
        grid=(n // tm,),
        in_specs=[
            pl.BlockSpec((tm, d), lambda i: (i, 0)),
            _const_spec((1, d)),
            _const_spec((d // 2, 2 * width)),
            _const_spec((1, 2 * width)),
            _const_spec((1, width)),
            _const_spec((1, width)),
            _const_spec((A_GROUPS, A_CHUNK, A_CHUNK)),
            _const_spec((A_CHUNK, width)),
            _const_spec((width // 2, d)),
        ],
        out_specs=pl.BlockSpec((tm, d), lambda i: (i, 0)),
        scratch_shapes=[pltpu.VMEM((tm, width), BF16)],
        compiler_params=pltpu.CompilerParams(
            dimension_semantics=("arbitrary",), vmem_limit_bytes=VMEM_LIMIT_BYTES),
        name="mixer_a",
    )(x2, g.reshape(1, d), _pack_row_pairs(w_in), b_in.reshape(1, -1), ln_g.reshape(1, -1),
      ln_b.reshape(1, -1), ws, bs, _pack_row_pairs(w_out))


def _gla_kernel(x_ref, g_ref, win_ref, wg2_ref, bg_ref, ng_ref, wout_ref, tri_ref,
                o_ref, st_ref, oc_sc):
    nb, tc, d = x_ref.shape
    kdim = bg_ref.shape[1]
    vdim = ng_ref.shape[1]
    dk = kdim // B_HEADS
    dv = vdim // B_HEADS

    @pl.when(pl.program_id(0) == 0)
    def _():
        st_ref[...] = jnp.zeros_like(st_ref)

    x = x_ref[...].reshape(nb * tc, d)
    h = _rms(x, g_ref[...]).astype(BF16)
    proj = jnp.dot(h, pltpu.bitcast(win_ref[...], BF16), preferred_element_type=F32)
    q = proj[:, :kdim] * (dk ** -0.5)
    k = proj[:, kdim:2 * kdim]
    v = proj[:, 2 * kdim:2 * kdim + vdim]
    r = proj[:, 2 * kdim + vdim:2 * kdim + 2 * vdim]
    glr = proj[:, 2 * kdim + 2 * vdim:]

    g_hi, g_lo = _split_bf16(glr)
    w_hi = wg2_ref[0]
    w_lo = wg2_ref[1]
    xg = (jnp.dot(g_hi, w_hi, preferred_element_type=F32)
          + jnp.dot(g_hi, w_lo, preferred_element_type=F32)
          + jnp.dot(g_lo, w_hi, preferred_element_type=F32)) + bg_ref[...]
    log_a = (jnp.minimum(xg, 0.0) - jnp.log(1.0 + jnp.exp(-jnp.abs(xg)))) * (1.0 / B_GATE_NORM)

    a_hi, a_lo = _split_bf16(log_a)
    tri = tri_ref[...]
    gcum = jnp.concatenate(
        [jnp.dot(tri, a_hi[bi * tc:(bi + 1) * tc], preferred_element_type=F32)
         + jnp.dot(tri, a_lo[bi * tc:(bi + 1) * tc], preferred_element_type=F32)
         for bi in range(nb)], axis=0)

    eg = jnp.exp(gcum)
    qd = (q * eg).astype(BF16)
    kd = (k * jnp.exp(-gcum)).astype(BF16)
    vb = v.astype(BF16)
    row = lax.broadcasted_iota(jnp.int32, (B_CHUNK, B_CHUNK), 0)
    col = lax.broadcasted_iota(jnp.int32, (B_CHUNK, B_CHUNK), 1)
    causal = row >= col

    for c, bi, hh in [(c, bi, hh) for c in range(tc // B_CHUNK) for bi in range(nb) for hh in range(B_HEADS)]:
        rows = slice(bi * tc + c * B_CHUNK, bi * tc + (c + 1) * B_CHUNK)
        last = bi * tc + (c + 1) * B_CHUNK - 1
        if True:
            kc = slice(hh * dk, (hh + 1) * dk)
            vc = slice(hh * dv, (hh + 1) * dv)
            g_last = gcum[last:last + 1, kc]
            ks = (k[rows, kc] * jnp.exp(g_last - gcum[rows, kc])).astype(BF16)
            st = st_ref[bi, hh]
            sc = lax.dot_general(qd[rows, kc], kd[rows, kc], (((1,), (1,)), ((), ())),
                                 preferred_element_type=F32)
            sc = jnp.where(causal, sc, 0.0).astype(BF16)
            o = jnp.dot(sc, vb[rows, vc], preferred_element_type=F32)
            o = o + lax.dot_general(qd[rows, kc], st.astype(BF16), (((1,), (1,)), ((), ())),
                                    preferred_element_type=F32)
            oc_sc[rows, vc] = o
            upd = lax.dot_general(vb[rows, vc], ks, (((0,), (0,)), ((), ())),
                                  preferred_element_type=F32)
            st_ref[bi, hh] = st * jnp.exp(g_last) + upd

    ng = ng_ref[...]
    for hh in range(B_HEADS):
        vc = slice(hh * dv, (hh + 1) * dv)
        oh = oc_sc[:, vc]
        oh = oh * lax.rsqrt(jnp.mean(oh * oh, axis=-1, keepdims=True) + NORM_EPS) * ng[:, vc]
        rh = r[:, vc]
        oc_sc[:, vc] = oh * (rh / (1.0 + jnp.exp(-rh)))
    y = x + jnp.dot(oc_sc[...].astype(BF16), pltpu.bitcast(wout_ref[...], BF16),
                    preferred_element_type=F32)
    o_ref[...] = y.reshape(nb, tc, d)


def _mixer_b(x3, g, w_in, w_g2, b_g, norm_g, w_out, tc=128):
    b, l, d = x3.shape
    kdim = w_g2.shape[1]
    vdim = w_out.shape[0]
    main = 2 * kdim + 2 * vdim
    w_in_p = _pack_row_pairs(jnp.concatenate(
        [w_in, jnp.zeros((d, LANES - B_GATE_RANK), w_in.dtype)], axis=1))
    w_g2_p = jnp.concatenate([w_g2, jnp.zeros((LANES - B_GATE_RANK, kdim), w_g2.dtype)], axis=0)
    hi = w_g2_p.astype(BF16)
    lo = (w_g2_p - hi.astype(F32)).astype(BF16)
    wg2 = jnp.stack([hi, lo])
    ng = jnp.tile(norm_g, B_HEADS).reshape(1, vdim)
    idx = jnp.arange(tc)
    tri = ((idx[:, None] >= idx[None, :]) &
           (idx[:, None] // B_CHUNK == idx[None, :] // B_CHUNK)).astype(BF16)
    return pl.pallas_call(
        _gla_kernel,
        out_shape=jax.ShapeDtypeStruct((b, l, d), F32),
        grid=(l // tc,),
        in_specs=[
            pl.BlockSpec((b, tc, d), lambda j: (0, j, 0)),
            _const_spec((1, d)),
            _const_spec((d // 2, main + LANES)),
            _const_spec((2, LANES, kdim)),
            _const_spec((1, kdim)),
            _const_spec((1, vdim)),
            _const_spec((vdim // 2, d)),
            _const_spec((tc, tc)),
        ],
        out_specs=pl.BlockSpec((b, tc, d), lambda j: (0, j, 0)),
        scratch_shapes=[pltpu.VMEM((b, B_HEADS, vdim // B_HEADS, kdim // B_HEADS), F32),
                        pltpu.VMEM((b * tc, vdim), F32)],
        compiler_params=pltpu.CompilerParams(
            dimension_semantics=("arbitrary",), vmem_limit_bytes=VMEM_LIMIT_BYTES),
        name="mixer_b_gla",
    )(x3, g.reshape(1, d), w_in_p, wg2, b_g.reshape(1, kdim), ng, _pack_row_pairs(w_out), tri)


def _sort16_pairs():
    pairs = []

    def merge(lo, n, r):
        step = 2 * r
        if step < n:
            merge(lo, n, step)
            merge(lo + r, n, step)
            pairs.extend((i, i + r) for i in range(lo + r, lo + n - r, step))
        else:
            pairs.append((lo, lo + r))

    def sort(lo, n):
        if n > 1:
            sort(lo, n // 2)
            sort(lo + n // 2, n // 2)
            merge(lo, n, 1)

    sort(0, P_TOPK)
    return pairs


def _sort16(vals):
    vals = list(vals)
    for i, j in _sort16_pairs():
        vals[i], vals[j] = jnp.maximum(vals[i], vals[j]), jnp.minimum(vals[i], vals[j])
    return vals


def _merge_top16(a, b):
    c = list(a)
    for m, bv in enumerate(b):
        i = P_TOPK - 1 - m
        c[i] = jnp.maximum(c[i], bv)
    dist = P_TOPK // 2
    while dist:
        for i in range(P_TOPK):
            if not i & dist:
                c[i], c[i + dist] = jnp.maximum(c[i], c[i + dist]), jnp.minimum(c[i], c[i + dist])
        dist //= 2
    return c


def _top16_of_keys(s_sc, lo, hi):
    if hi - lo == P_TOPK:
        return _sort16([s_sc[k * SUBLANES:(k + 1) * SUBLANES, :] for k in range(lo, hi)])
    mid = (lo + hi) // 2
    return _merge_top16(_top16_of_keys(s_sc, lo, mid), _top16_of_keys(s_sc, mid, hi))


def _count_true_prefix(pred, thr):
    steps = (8, 4, 2, 1)
    masks = []

    def pick(ms, cands):
        if not ms:
            return cands[0]
        half = len(cands) // 2
        return jnp.where(ms[0], pick(ms[1:], cands[half:]), pick(ms[1:], cands[:half]))

    for lvl, step in enumerate(steps):
        cands = []
        for combo in range(2 ** lvl):
            taken = sum(s for bit, s in enumerate(steps[:lvl]) if (combo >> (lvl - 1 - bit)) & 1)
            cands.append(thr[taken + step - 1])
        masks.append(pred(pick(masks, cands)))
    count = None
    for m, step in zip(masks, steps):
        term = jnp.where(m, float(step), 0.0)
        count = term if count is None else count + term
    return jnp.where(pred(thr[P_TOPK - 1]), float(P_TOPK), count)


def _bf16_bits(x):
    return pltpu.bitcast(x.astype(BF16).astype(F32), jnp.uint32)


def _bf16_pair_bits(x):
    bits = _bf16_bits(x)
    return bits | (bits >> 16)


def _to_key_major(s, s_sc):
    for g in range(s.shape[0] // SUBLANES):
        for c in range(PREP_SLABS):
            s_sc[pl.ds(g * SUBLANES * PREP_SLABS + c, SUBLANES, stride=PREP_SLABS), :] = (
                s[g * SUBLANES:(g + 1) * SUBLANES, c * LANES:(c + 1) * LANES])


def _peer_prep_kernel(x_ref, g_ref, wq_ref, k1_ref, k2_ref,
                      ht_ref, re_ref, c_ref, e1_ref,
                      ht_sc, qt_sc, s1_sc, s2_sc, rw_sc, ew_sc):
    hh = pl.program_id(1)
    half = 2 * k1_ref.shape[0]

    @pl.when(hh == 0)
    def _():
        h = _rms(x_ref[...], g_ref[...])
        ht = h.T.astype(BF16)
        ht_ref[...] = pltpu.bitcast(ht, jnp.uint32)
        ht_sc[...] = ht

    qt_sc[...] = jnp.dot(pltpu.bitcast(wq_ref[0], BF16), ht_sc[...],
                         preferred_element_type=F32).astype(BF16)
    s1 = jnp.dot(pltpu.bitcast(k1_ref[...], BF16), qt_sc[:half, :], preferred_element_type=F32)
    s2 = jnp.dot(pltpu.bitcast(k2_ref[...], BF16), qt_sc[half:, :], preferred_element_type=F32)
    _to_key_major(s1, s1_sc)
    _to_key_major(s2, s2_sc)

    v1 = _top16_of_keys(s1_sc, 0, P_NKEYS)
    v2 = _top16_of_keys(s2_sc, 0, P_NKEYS)

    top = [v1[0] + v2[b] for b in range(P_TOPK)]
    for a in range(1, P_TOPK // 2):
        top = _merge_top16(top, [v1[a] + v2[b] for b in range(P_TOPK // (a + 1))])
    top = _merge_top16(top, [v1[a] + v2[0] for a in range(P_TOPK // 2, P_TOPK)])
    tau = top[P_TOPK - 1]
    z = None
    for m in range(P_TOPK):
        term = jnp.exp(top[m] - top[0])
        z = term if z is None else z + term
    inv_z = 1.0 / z

    for p in range(P_NKEYS // 2):
        rank_bits = []
        e2_bits = []
        for k in (2 * p, 2 * p + 1):
            tile = slice(k * SUBLANES, (k + 1) * SUBLANES)
            x2 = s2_sc[tile, :]
            rank_bits.append(pltpu.bitcast(_count_true_prefix(lambda t, x2=x2: t > x2, v2), jnp.uint32))
            e2_bits.append(_bf16_bits(jnp.exp(x2 - v2[0])))
            x1 = s1_sc[tile, :]
            cnt = _count_true_prefix(lambda t, x1=x1: x1 + t >= tau, v2)
            cnt_bits = pltpu.bitcast(cnt, jnp.uint32)
            c_ref[0, 0, k] = cnt_bits | (cnt_bits >> 16)
            e1_ref[0, 0, k] = _bf16_pair_bits(jnp.exp(x1 - v1[0]) * inv_z)
        words = slice(p * SUBLANES, (p + 1) * SUBLANES)
        rw_sc[words, :] = (rank_bits[0] >> 16) | rank_bits[1]
        ew_sc[words, :] = (e2_bits[0] >> 16) | e2_bits[1]

    for jc in range(P_NKEYS // BF16_ROWS):
        for c in range(PREP_SLABS):
            rows = pl.ds(jc * SUBLANES * PREP_SLABS + c, SUBLANES, stride=PREP_SLABS)
            re_ref[c, jc, 0, 0:SUBLANES, :] = rw_sc[rows, :]
            re_ref[c, jc, 0, SUBLANES:BF16_ROWS, :] = ew_sc[rows, :]


def _peer_prep(x2, g, w_q, k1, k2):
    n, d = x2.shape
    tt = PREP_SLABS * LANES
    qdim = w_q.shape[1] // P_HEADS
    chunks = P_NKEYS // BF16_ROWS
    re_shape = jax.ShapeDtypeStruct((n // LANES, chunks, P_HEADS, BF16_ROWS, LANES), jnp.uint32)
    re_spec = pl.BlockSpec((PREP_SLABS, chunks, 1, BF16_ROWS, LANES), lambda t, h: (t, 0, h, 0, 0))
    row_shape = jax.ShapeDtypeStruct((n // tt, P_HEADS, P_NKEYS, PREP_SLABS, LANES), jnp.uint32)
    row_spec = pl.BlockSpec((1, 1, P_NKEYS, PREP_SLABS, LANES), lambda t, h: (t, h, 0, 0, 0))
    wq = _pack_row_pairs(w_q.T).reshape(P_HEADS, qdim // 2, d)
    return pl.pallas_call(
        _peer_prep_kernel,
        out_shape=(jax.ShapeDtypeStruct((d // 2, n), jnp.uint32), re_shape, row_shape, row_shape),
        grid=(n // tt, P_HEADS),
        in_specs=[
            pl.BlockSpec((tt, d), lambda t, h: (t, 0)),
            _const_spec((1, d)),
            pl.BlockSpec((1, qdim // 2, d), lambda t, h: (h, 0, 0)),
            _const_spec((P_NKEYS // 2, k1.shape[1])),
            _const_spec((P_NKEYS // 2, k2.shape[1])),
        ],
        out_specs=(pl.BlockSpec((d // 2, tt), lambda t, h: (0, t)), re_spec, row_spec, row_spec),
        scratch_shapes=[pltpu.VMEM((d, tt), BF16),
                        pltpu.VMEM((qdim, tt), BF16),
                        pltpu.VMEM((P_NKEYS * PREP_SLABS, LANES), F32),
                        pltpu.VMEM((P_NKEYS * PREP_SLABS, LANES), F32),
                        pltpu.VMEM((P_NKEYS // 2 * PREP_SLABS, LANES), jnp.uint32),
                        pltpu.VMEM((P_NKEYS // 2 * PREP_SLABS, LANES), jnp.uint32)],
        compiler_params=pltpu.CompilerParams(
            dimension_semantics=("arbitrary", "arbitrary"), vmem_limit_bytes=VMEM_LIMIT_BYTES),
        name="peer_prep",
    )(x2, g.reshape(1, d), wq, _pack_row_pairs(k1), _pack_row_pairs(k2))


def _gate_times_act(re_ref, c_ref, e1_ref, a_ref, w_ref, slab0, blocks):
    for key1_rows, lc in blocks:
        lanes = slice(lc * LANES, (lc + 1) * LANES)
        slab = pl.ds(slab0 + lc, 1)

        def row_tile(ref, hh, i):
            return pltpu.bitcast(jnp.broadcast_to(ref[0, hh, i, slab, :], (SUBLANES, LANES)), BF16)

        cnt = {(i, hh): row_tile(c_ref, hh, i) for i in key1_rows for hh in range(P_HEADS)}
        e1 = {(i, hh): row_tile(e1_ref, hh, i) for i in key1_rows for hh in range(P_HEADS)}
        for jc in range(P_NKEYS // BF16_ROWS):
            gate = {i: None for i in key1_rows}
            for hh in range(P_HEADS):
                rank2 = pltpu.bitcast(re_ref[lc, jc, hh, 0:SUBLANES, :], BF16)
                e2 = pltpu.bitcast(re_ref[lc, jc, hh, SUBLANES:BF16_ROWS, :], BF16)
                for i in key1_rows:
                    term = jnp.where(rank2 < cnt[i, hh], e2, 0.0) * e1[i, hh]
                    gate[i] = term if gate[i] is None else gate[i] + term
            for i in key1_rows:
                rows = slice(i * P_NKEYS + jc * BF16_ROWS, i * P_NKEYS + (jc + 1) * BF16_ROWS)
                w_ref[rows, lanes] = gate[i] * _gelu(a_ref[rows, lanes]).astype(BF16)


def _peer_dense_kernel(ht_ref, re_ref, c_ref, e1_ref, u_ref, vt_ref, x_ref, fg_ref,
                       o_ref, a0_sc, a1_sc, w0_sc, w1_sc, acc_sc, *, n_exp_tiles, n_tiles, final_norm):
    k = pl.program_id(0)
    c_pair = jnp.clip(k - 2, 0, n_tiles - 1)
    c_exp = c_pair % n_exp_tiles

    @pl.when(k == 0)
    def _():
        a1_sc[...] = jnp.zeros_like(a1_sc)
        w0_sc[...] = jnp.zeros_like(w0_sc)

    @pl.when(c_exp == 0)
    def _():
        acc_sc[...] = jnp.zeros_like(acc_sc)

    def stages(a_out, a_in, w_out, w_in):
        tb = ht_ref.shape[1]
        eb = a_in.shape[0]
        d = acc_sc.shape[0]
        n_chunks = tb // MXU_COLS
        blocks = [((i, i + 1), lc) for i in range(0, P_ROWS_PER_STEP, 2) for lc in range(tb // LANES)]
        n_units = n_chunks * (eb // MXU_COLS + d // MXU_COLS)
        per_unit = len(blocks) // n_units
        todo = iter(range(0, len(blocks), per_unit))
        tiles_per_row_tile = PREP_SLABS * LANES // tb
        if tiles_per_row_tile == 1:
            slab0 = 0
        else:
            b_tok = jnp.clip(k - 1, 0, n_tiles - 1) // n_exp_tiles
            slab0 = (b_tok % tiles_per_row_tile) * (tb // LANES)

        def vpu_slice():
            b0 = next(todo)
            _gate_times_act(re_ref, c_ref, e1_ref, a_in, w_out, slab0, blocks[b0:b0 + per_unit])

        for n in range(n_chunks):
            cols = slice(n * MXU_COLS, (n + 1) * MXU_COLS)
            t = acc_sc[:, cols]
            for kt in range(eb // MXU_COLS):
                ks = slice(kt * MXU_COLS, (kt + 1) * MXU_COLS)
                t = t + jnp.dot(pltpu.bitcast(vt_ref[:, ks], BF16), w_in[ks, cols],
                                preferred_element_type=F32)
                vpu_slice()
            acc_sc[:, cols] = t
            t = None
            for kt in range(d // MXU_COLS):
                ks = slice(kt * MXU_COLS, (kt + 1) * MXU_COLS)
                kw = slice(kt * MXU_COLS // 2, (kt + 1) * MXU_COLS // 2)
                p = jnp.dot(pltpu.bitcast(u_ref[:, ks], BF16), pltpu.bitcast(ht_ref[kw, cols], BF16),
                            preferred_element_type=F32)
                t = p if t is None else t + p
                vpu_slice()
            a_out[:, cols] = t

    @pl.when(k % 2 == 0)
    def _():
        stages(a0_sc, a1_sc, w1_sc, w0_sc)

    @pl.when(k % 2 == 1)
    def _():
        stages(a1_sc, a0_sc, w0_sc, w1_sc)

    @pl.when(jnp.logical_and(k >= 2, c_exp == n_exp_tiles - 1))
    def _():
        y = x_ref[...] + acc_sc[...].T
        if final_norm:
            y = _rms(y, fg_ref[...])
        o_ref[...] = y


def _peer_dense(x2, ht, re, cnt, e1, u_tab, v_tab, final_g, final_norm, tb=1024):
    n, d = x2.shape
    n_exp = u_tab.shape[0]
    eb = P_ROWS_PER_STEP * P_NKEYS
    tb = min(tb, n)
    ne = n_exp // eb
    n_tiles = (n // tb) * ne
    last = n_tiles - 1

    def pair_a(k):
        return jnp.minimum(k, last)

    def pair_b(k):
        return jnp.clip(k - 1, 0, last)

    def pair_c(k):
        return jnp.clip(k - 2, 0, last)

    tiles_per_row_tile = PREP_SLABS * LANES // tb
    once = pl.Buffered(1)
    big = pl.BlockSpec((tb // LANES, P_NKEYS // BF16_ROWS, P_HEADS, BF16_ROWS, LANES),
                       lambda k: (pair_b(k) // ne, 0, 0, 0, 0), pipeline_mode=once)
    small = pl.BlockSpec((1, P_HEADS, P_ROWS_PER_STEP, PREP_SLABS, LANES),
                         lambda k: (pair_b(k) // ne // tiles_per_row_tile, 0, pair_b(k) % ne, 0, 0))
    return pl.pallas_call(
        functools.partial(_peer_dense_kernel, n_exp_tiles=ne, n_tiles=n_tiles, final_norm=final_norm),
        out_shape=jax.ShapeDtypeStruct((n, d), F32),
        grid=(n_tiles + 2,),
        in_specs=[
            pl.BlockSpec((d // 2, tb), lambda k: (0, pair_a(k) // ne), pipeline_mode=once),
            big, small, small,
            pl.BlockSpec((eb // 2, d), lambda k: (pair_a(k) % ne, 0)),
            pl.BlockSpec((d // 2, eb), lambda k: (0, pair_c(k) % ne)),
            pl.BlockSpec((tb, d), lambda k: (pair_c(k) // ne, 0), pipeline_mode=once),
            pl.BlockSpec((1, d), lambda k: (0, 0)),
        ],
        out_specs=pl.BlockSpec((tb, d), lambda k: (pair_c(k) // ne, 0)),
        scratch_shapes=[pltpu.VMEM((eb, tb), F32), pltpu.VMEM((eb, tb), F32),
                        pltpu.VMEM((eb, tb), BF16), pltpu.VMEM((eb, tb), BF16),
                        pltpu.VMEM((d, tb), F32)],
        compiler_params=pltpu.CompilerParams(
            dimension_semantics=("arbitrary",), vmem_limit_bytes=VMEM_LIMIT_BYTES),
        name="peer_dense",
    )(ht, re, cnt, e1, _pack_row_pairs(u_tab), _pack_row_pairs(v_tab.T), x2, final_g.reshape(1, d))


def _peer(x2, g, w_q, k1, k2, u_tab, v_tab, final_g, final_norm):
    ht, re, cnt, e1 = _peer_prep(x2, g, w_q, k1, k2)
    return _peer_dense(x2, ht, re, cnt, e1, u_tab, v_tab, final_g, final_norm)


def kernel(x, norm_mix, norm_ffn, final_norm, a_w_in, a_b_in, a_ln_g, a_ln_b, a_w_s, a_b_s, a_w_out,
           b_w_in, b_w_g2, b_b_g, b_norm, b_w_out, p_w_q, p_k1, p_k2, p_u, p_v):
    b, l, d = x.shape
    depth = norm_mix.shape[0]
    x2 = x.reshape(b * l, d)
    for i in range(depth):
        j = i // 2
        if i % 2 == 0:
            x2 = _mixer_a(x2, norm_mix[i], a_w_in[j], a_b_in[j], a_ln_g[j], a_ln_b[j],
                          a_w_s[j], a_b_s[j], a_w_out[j])
        else:
            x2 = _mixer_b(x2.reshape(b, l, d), norm_mix[i], b_w_in[j], b_w_g2[j], b_b_g[j],
                          b_norm[j], b_w_out[j]).reshape(b * l, d)
        x2 = _peer(x2, norm_ffn[i], p_w_q[i], p_k1[i], p_k2[i], p_u[i], p_v[i],
                   final_norm, i == depth - 1)
    return x2.reshape(b, l, d)
```

```python
import functools
import math

import jax
import jax.numpy as jnp
from jax import lax
from jax.experimental import pallas as pl
from jax.experimental.pallas import tpu as pltpu

F32 = jnp.float32
BF16 = jnp.bfloat16

NORM_EPS = 1e-6

LANES = 128
SUBLANES = 8
BF16_ROWS = 2 * SUBLANES
MXU_COLS = 256
PACK_BLOCK = 512
M_SPLIT = 2
VMEM_LIMIT_BYTES = 56 * 1024 * 1024

A_GROUPS = 8
A_CHUNK = 128
B_HEADS = 4
B_GATE_RANK = 16
B_GATE_NORM = 16.0
B_CHUNK = 64
P_HEADS = 8
P_NKEYS = 128
P_TOPK = 16
P_ROWS_PER_STEP = 8
PREP_SLABS = SUBLANES


def _rms(x, g):
    return x * lax.rsqrt(jnp.mean(x * x, axis=-1, keepdims=True) + NORM_EPS) * g


def _gelu(x):
    return 0.5 * x * (1.0 + lax.erf(x * (1.0 / math.sqrt(2.0))))


def _split_bf16(x):
    hi = x.astype(BF16)
    lo = (x - hi.astype(F32)).astype(BF16)
    return hi, lo


def _pack_kernel(x_ref, o_ref, *, transpose):
    x = x_ref[...]
    if transpose:
        x = x.T
    o_ref[...] = pltpu.bitcast(x.astype(BF16), jnp.uint32)


def _pack_row_pairs(w, transpose=False):
    rows, cols = w.shape
    blk = min(PACK_BLOCK, rows)
    if transpose:
        out_shape = jax.ShapeDtypeStruct((cols // 2, rows), jnp.uint32)
        out_spec = pl.BlockSpec((cols // 2, blk), lambda i: (0, i))
    else:
        out_shape = jax.ShapeDtypeStruct((rows // 2, cols), jnp.uint32)
        out_spec = pl.BlockSpec((blk // 2, cols), lambda i: (i, 0))
    return pl.pallas_call(
        functools.partial(_pack_kernel, transpose=transpose),
        out_shape=out_shape,
        grid=(rows // blk,),
        in_specs=[pl.BlockSpec((blk, cols), lambda i: (i, 0))],
        out_specs=out_spec,
        compiler_params=pltpu.CompilerParams(
            dimension_semantics=("arbitrary",), vmem_limit_bytes=VMEM_LIMIT_BYTES),
        name="pack_bf16_pairs",
    )(w)


def _const_spec(shape):
    nd = len(shape)
    return pl.BlockSpec(shape, lambda *_: (0,) * nd)


def _mixer_a_kernel(x_ref, g_ref, win_ref, bin_ref, lng_ref, lnb_ref, ws_ref, bs_ref, wout_ref,
                    o_ref, y_sc):
    tm = x_ref.shape[0]
    width = lng_ref.shape[1]
    gdim = width // A_GROUPS
    x = x_ref[...]
    h = _rms(x, g_ref[...]).astype(BF16)
    z = _gelu(jnp.dot(h, pltpu.bitcast(win_ref[...], BF16), preferred_element_type=F32) + bin_ref[...])
    u = z[:, :width]
    v = z[:, width:]
    mu = jnp.mean(v, axis=-1, keepdims=True)
    vc = v - mu
    var = jnp.mean(vc * vc, axis=-1, keepdims=True)
    vn = (vc * lax.rsqrt(var + NORM_EPS) * lng_ref[...] + lnb_ref[...]).astype(BF16)
    for c in range(tm // A_CHUNK):
        rows = slice(c * A_CHUNK, (c + 1) * A_CHUNK)
        for g in range(A_GROUPS):
            cols = slice(g * gdim, (g + 1) * gdim)
            sv = jnp.dot(ws_ref[g], vn[rows, cols], preferred_element_type=F32) + bs_ref[:, cols]
            y_sc[rows, cols] = (u[rows, cols] * sv).astype(BF16)
    o_ref[...] = x + jnp.dot(y_sc[...], pltpu.bitcast(wout_ref[...], BF16), preferred_element_type=F32)


def _mixer_a(x2, g, w_in, b_in, ln_g, ln_b, w_s, b_s, w_out, tm=256):
    n, d = x2.shape
    width = ln_g.shape[0]
    causal = jnp.tril(jnp.ones((A_CHUNK, A_CHUNK), dtype=bool))
    ws = jnp.where(causal[None], w_s, 0.0).astype(BF16)
    bs = jnp.repeat(b_s.T, width // A_GROUPS, axis=1)
    return pl.pallas_call(
        _mixer_a_kernel,
        out_shape=jax.ShapeDtypeStruct((n, d), F32),
        grid=(n // tm,),
        in_specs=[
            pl.BlockSpec((tm, d), lambda i: (i, 0)),
            _const_spec((1, d)),
            _const_spec((d // 2, 2 * width)),
            _const_spec((1, 2 * width)),
            _const_spec((1, width)),
            _const_spec((1, width)),
            _const_spec((A_GROUPS, A_CHUNK, A_CHUNK)),
            _const_spec((A_CHUNK, width)),
            _const_spec((width // 2, d)),
        ],
        out_specs=pl.BlockSpec((tm, d), lambda i: (i, 0)),
        scratch_shapes=[pltpu.VMEM((tm, width), BF16)],
        compiler_params=pltpu.CompilerParams(
            dimension_semantics=("arbitrary",), vmem_limit_bytes=VMEM_LIMIT_BYTES),
        name="mixer_a",
    )(x2, g.reshape(1, d), _pack_row_pairs(w_in), b_in.reshape(1, -1), ln_g.reshape(1, -1),
      ln_b.reshape(1, -1), ws, bs, _pack_row_pairs(w_out))


def _gla_kernel(x_ref, g_ref, win_ref, wg2_ref, bg_ref, ng_ref, wout_ref, tri_ref,
                o_ref, st_ref, oc_sc):
    nb, tc, d = x_ref.shape
    kdim = bg_ref.shape[1]
    vdim = ng_ref.shape[1]
    dk = kdim // B_HEADS
    dv = vdim // B_HEADS

    @pl.when(pl.program_id(0) == 0)
    def _():
        st_ref[...] = jnp.zeros_like(st_ref)

    x = x_ref[...].reshape(nb * tc, d)
    h = _rms(x, g_ref[...]).astype(BF16)
    proj = jnp.dot(h, pltpu.bitcast(win_ref[...], BF16), preferred_element_type=F32)
    q = proj[:, :kdim] * (dk ** -0.5)
    k = proj[:, kdim:2 * kdim]
    v = proj[:, 2 * kdim:2 * kdim + vdim]
    r = proj[:, 2 * kdim + vdim:2 * kdim + 2 * vdim]
    glr = proj[:, 2 * kdim + 2 * vdim:]

    g_hi, g_lo = _split_bf16(glr)
    w_hi = wg2_ref[0]
    w_lo = wg2_ref[1]
    xg = (jnp.dot(g_hi, w_hi, preferred_element_type=F32)
          + jnp.dot(g_hi, w_lo, preferred_element_type=F32)
          + jnp.dot(g_lo, w_hi, preferred_element_type=F32)) + bg_ref[...]
    log_a = (jnp.minimum(xg, 0.0) - jnp.log(1.0 + jnp.exp(-jnp.abs(xg)))) * (1.0 / B_GATE_NORM)

    a_hi, a_lo = _split_bf16(log_a)
    tri = tri_ref[...]
    gcum = jnp.concatenate(
        [jnp.dot(tri, a_hi[bi * tc:(bi + 1) * tc], preferred_element_type=F32)
         + jnp.dot(tri, a_lo[bi * tc:(bi + 1) * tc], preferred_element_type=F32)
         for bi in range(nb)], axis=0)

    eg = jnp.exp(gcum)
    qd = (q * eg).astype(BF16)
    kd = (k * jnp.exp(-gcum)).astype(BF16)
    vb = v.astype(BF16)
    row = lax.broadcasted_iota(jnp.int32, (B_CHUNK, B_CHUNK), 0)
    col = lax.broadcasted_iota(jnp.int32, (B_CHUNK, B_CHUNK), 1)
    causal = row >= col

    for c, bi, hh in [(c, bi, hh) for c in range(tc // B_CHUNK) for bi in range(nb) for hh in range(B_HEADS)]:
        rows = slice(bi * tc + c * B_CHUNK, bi * tc + (c + 1) * B_CHUNK)
        last = bi * tc + (c + 1) * B_CHUNK - 1
        if True:
            kc = slice(hh * dk, (hh + 1) * dk)
            vc = slice(hh * dv, (hh + 1) * dv)
            g_last = gcum[last:last + 1, kc]
            ks = (k[rows, kc] * jnp.exp(g_last - gcum[rows, kc])).astype(BF16)
            st = st_ref[bi, hh]
            sc = lax.dot_general(qd[rows, kc], kd[rows, kc], (((1,), (1,)), ((), ())),
                                 preferred_element_type=F32)
            sc = jnp.where(causal, sc, 0.0).astype(BF16)
            o = jnp.dot(sc, vb[rows, vc], preferred_element_type=F32)
            o = o + lax.dot_general(qd[rows, kc], st.astype(BF16), (((1,), (1,)), ((), ())),
                                    preferred_element_type=F32)
            oc_sc[rows, vc] = o
            upd = lax.dot_general(vb[rows, vc], ks, (((0,), (0,)), ((), ())),
                                  preferred_element_type=F32)
            st_ref[bi, hh] = st * jnp.exp(g_last) + upd

    ng = ng_ref[...]
    for hh in range(B_HEADS):
        vc = slice(hh * dv, (hh + 1) * dv)
        oh = oc_sc[:, vc]
        oh = oh * lax.rsqrt(jnp.mean(oh * oh, axis=-1, keepdims=True) + NORM_EPS) * ng[:, vc]
        rh = r[:, vc]
        oc_sc[:, vc] = oh * (rh / (1.0 + jnp.exp(-rh)))
    y = x + jnp.dot(oc_sc[...].astype(BF16), pltpu.bitcast(wout_ref[...], BF16),
                    preferred_element_type=F32)
    o_ref[...] = y.reshape(nb, tc, d)


def _mixer_b(x3, g, w_in, w_g2, b_g, norm_g, w_out, tc=128):
    b, l, d = x3.shape
    kdim = w_g2.shape[1]
    vdim = w_out.shape[0]
    main = 2 * kdim + 2 * vdim
    w_in_p = _pack_row_pairs(jnp.concatenate(
        [w_in, jnp.zeros((d, LANES - B_GATE_RANK), w_in.dtype)], axis=1))
    w_g2_p = jnp.concatenate([w_g2, jnp.zeros((LANES - B_GATE_RANK, kdim), w_g2.dtype)], axis=0)
    hi = w_g2_p.astype(BF16)
    lo = (w_g2_p - hi.astype(F32)).astype(BF16)
    wg2 = jnp.stack([hi, lo])
    ng = jnp.tile(norm_g, B_HEADS).reshape(1, vdim)
    idx = jnp.arange(tc)
    tri = ((idx[:, None] >= idx[None, :]) &
           (idx[:, None] // B_CHUNK == idx[None, :] // B_CHUNK)).astype(BF16)
    return pl.pallas_call(
        _gla_kernel,
        out_shape=jax.ShapeDtypeStruct((b, l, d), F32),
        grid=(l // tc,),
        in_specs=[
            pl.BlockSpec((b, tc, d), lambda j: (0, j, 0)),
            _const_spec((1, d)),
            _const_spec((d // 2, main + LANES)),
            _const_spec((2, LANES, kdim)),
            _const_spec((1, kdim)),
            _const_spec((1, vdim)),
            _const_spec((vdim // 2, d)),
            _const_spec((tc, tc)),
        ],
        out_specs=pl.BlockSpec((b, tc, d), lambda j: (0, j, 0)),
        scratch_shapes=[pltpu.VMEM((b, B_HEADS, vdim // B_HEADS, kdim // B_HEADS), F32),
                        pltpu.VMEM((b * tc, vdim), F32)],
        compiler_params=pltpu.CompilerParams(
            dimension_semantics=("arbitrary",), vmem_limit_bytes=VMEM_LIMIT_BYTES),
        name="mixer_b_gla",
    )(x3, g.reshape(1, d), w_in_p, wg2, b_g.reshape(1, kdim), ng, _pack_row_pairs(w_out), tri)


def _sort16_pairs():
    pairs = []

    def merge(lo, n, r):
        step = 2 * r
        if step < n:
            merge(lo, n, step)
            merge(lo + r, n, step)
            pairs.extend((i, i + r) for i in range(lo + r, lo + n - r, step))
        else:
            pairs.append((lo, lo + r))

    def sort(lo, n):
        if n > 1:
            sort(lo, n // 2)
            sort(lo + n // 2, n // 2)
            merge(lo, n, 1)

    sort(0, P_TOPK)
    return pairs


def _sort16(vals):
    vals = list(vals)
    for i, j in _sort16_pairs():
        vals[i], vals[j] = jnp.maximum(vals[i], vals[j]), jnp.minimum(vals[i], vals[j])
    return vals


def _merge_top16(a, b):
    c = list(a)
    for m, bv in enumerate(b):
        i = P_TOPK - 1 - m
        c[i] = jnp.maximum(c[i], bv)
    dist = P_TOPK // 2
    while dist:
        for i in range(P_TOPK):
            if not i & dist:
                c[i], c[i + dist] = jnp.maximum(c[i], c[i + dist]), jnp.minimum(c[i], c[i + dist])
        dist //= 2
    return c


def _top16_of_keys(s_sc, lo, hi):
    if hi - lo == P_TOPK:
        return _sort16([s_sc[k * SUBLANES:(k + 1) * SUBLANES, :] for k in range(lo, hi)])
    mid = (lo + hi) // 2
    return _merge_top16(_top16_of_keys(s_sc, lo, mid), _top16_of_keys(s_sc, mid, hi))


def _count_true_prefix(pred, thr):
    steps = (8, 4, 2, 1)
    masks = []

    def pick(ms, cands):
        if not ms:
            return cands[0]
        half = len(cands) // 2
        return jnp.where(ms[0], pick(ms[1:], cands[half:]), pick(ms[1:], cands[:half]))

    for lvl, step in enumerate(steps):
        cands = []
        for combo in range(2 ** lvl):
            taken = sum(s for bit, s in enumerate(steps[:lvl]) if (combo >> (lvl - 1 - bit)) & 1)
            cands.append(thr[taken + step - 1])
        masks.append(pred(pick(masks, cands)))
    count = None
    for m, step in zip(masks, steps):
        term = jnp.where(m, float(step), 0.0)
        count = term if count is None else count + term
    return jnp.where(pred(thr[P_TOPK - 1]), float(P_TOPK), count)


def _bf16_bits(x):
    return pltpu.bitcast(x.astype(BF16).astype(F32), jnp.uint32)


def _bf16_pair_bits(x):
    bits = _bf16_bits(x)
    return bits | (bits >> 16)


def _to_key_major(s, s_sc):
    for g in range(s.shape[0] // SUBLANES):
        for c in range(PREP_SLABS):
            s_sc[pl.ds(g * SUBLANES * PREP_SLABS + c, SUBLANES, stride=PREP_SLABS), :] = (
                s[g * SUBLANES:(g + 1) * SUBLANES, c * LANES:(c + 1) * LANES])


def _peer_prep_kernel(x_ref, g_ref, wq_ref, k1_ref, k2_ref,
                      ht_ref, re_ref, c_ref, e1_ref,
                      ht_sc, qt_sc, s1_sc, s2_sc, rw_sc, ew_sc):
    hh = pl.program_id(1)
    half = 2 * k1_ref.shape[0]

    @pl.when(hh == 0)
    def _():
        h = _rms(x_ref[...], g_ref[...])
        ht = h.T.astype(BF16)
        ht_ref[...] = pltpu.bitcast(ht, jnp.uint32)
        ht_sc[...] = ht

    qt_sc[...] = jnp.dot(pltpu.bitcast(wq_ref[0], BF16), ht_sc[...],
                         preferred_element_type=F32).astype(BF16)
    s1 = jnp.dot(pltpu.bitcast(k1_ref[...], BF16), qt_sc[:half, :], preferred_element_type=F32)
    s2 = jnp.dot(pltpu.bitcast(k2_ref[...], BF16), qt_sc[half:, :], preferred_element_type=F32)
    _to_key_major(s1, s1_sc)
    _to_key_major(s2, s2_sc)

    v1 = _top16_of_keys(s1_sc, 0, P_NKEYS)
    v2 = _top16_of_keys(s2_sc, 0, P_NKEYS)

    top = [v1[0] + v2[b] for b in range(P_TOPK)]
    for a in range(1, P_TOPK // 2):
        top = _merge_top16(top, [v1[a] + v2[b] for b in range(P_TOPK // (a + 1))])
    top = _merge_top16(top, [v1[a] + v2[0] for a in range(P_TOPK // 2, P_TOPK)])
    tau = top[P_TOPK - 1]
    z = None
    for m in range(P_TOPK):
        term = jnp.exp(top[m] - top[0])
        z = term if z is None else z + term
    inv_z = 1.0 / z

    for p in range(P_NKEYS // 2):
        rank_bits = []
        e2_bits = []
        for k in (2 * p, 2 * p + 1):
            tile = slice(k * SUBLANES, (k + 1) * SUBLANES)
            x2 = s2_sc[tile, :]
            rank_bits.append(pltpu.bitcast(_count_true_prefix(lambda t, x2=x2: t > x2, v2), jnp.uint32))
            e2_bits.append(_bf16_bits(jnp.exp(x2 - v2[0])))
            x1 = s1_sc[tile, :]
            cnt = _count_true_prefix(lambda t, x1=x1: x1 + t >= tau, v2)
            cnt_bits = pltpu.bitcast(cnt, jnp.uint32)
            c_ref[0, 0, k] = cnt_bits | (cnt_bits >> 16)
            e1_ref[0, 0, k] = _bf16_pair_bits(jnp.exp(x1 - v1[0]) * inv_z)
        words = slice(p * SUBLANES, (p + 1) * SUBLANES)
        rw_sc[words, :] = (rank_bits[0] >> 16) | rank_bits[1]
        ew_sc[words, :] = (e2_bits[0] >> 16) | e2_bits[1]

    for jc in range(P_NKEYS // BF16_ROWS):
        for c in range(PREP_SLABS):
            rows = pl.ds(jc * SUBLANES * PREP_SLABS + c, SUBLANES, stride=PREP_SLABS)
            re_ref[c, jc, 0, 0:SUBLANES, :] = rw_sc[rows, :]
            re_ref[c, jc, 0, SUBLANES:BF16_ROWS, :] = ew_sc[rows, :]


def _peer_prep(x2, g, w_q, k1, k2):
    n, d = x2.shape
    tt = PREP_SLABS * LANES
    qdim = w_q.shape[1] // P_HEADS
    chunks = P_NKEYS // BF16_ROWS
    re_shape = jax.ShapeDtypeStruct((n // LANES, chunks, P_HEADS, BF16_ROWS, LANES), jnp.uint32)
    re_spec = pl.BlockSpec((PREP_SLABS, chunks, 1, BF16_ROWS, LANES), lambda t, h: (t, 0, h, 0, 0))
    row_shape = jax.ShapeDtypeStruct((n // tt, P_HEADS, P_NKEYS, PREP_SLABS, LANES), jnp.uint32)
    row_spec = pl.BlockSpec((1, 1, P_NKEYS, PREP_SLABS, LANES), lambda t, h: (t, h, 0, 0, 0))
    wq = _pack_row_pairs(w_q, transpose=True).reshape(P_HEADS, qdim // 2, d)
    return pl.pallas_call(
        _peer_prep_kernel,
        out_shape=(jax.ShapeDtypeStruct((d // 2, n), jnp.uint32), re_shape, row_shape, row_shape),
        grid=(n // tt, P_HEADS),
        in_specs=[
            pl.BlockSpec((tt, d), lambda t, h: (t, 0)),
            _const_spec((1, d)),
            pl.BlockSpec((1, qdim // 2, d), lambda t, h: (h, 0, 0)),
            _const_spec((P_NKEYS // 2, k1.shape[1])),
            _const_spec((P_NKEYS // 2, k2.shape[1])),
        ],
        out_specs=(pl.BlockSpec((d // 2, tt), lambda t, h: (0, t)), re_spec, row_spec, row_spec),
        scratch_shapes=[pltpu.VMEM((d, tt), BF16),
                        pltpu.VMEM((qdim, tt), BF16),
                        pltpu.VMEM((P_NKEYS * PREP_SLABS, LANES), F32),
                        pltpu.VMEM((P_NKEYS * PREP_SLABS, LANES), F32),
                        pltpu.VMEM((P_NKEYS // 2 * PREP_SLABS, LANES), jnp.uint32),
                        pltpu.VMEM((P_NKEYS // 2 * PREP_SLABS, LANES), jnp.uint32)],
        compiler_params=pltpu.CompilerParams(
            dimension_semantics=("arbitrary", "arbitrary"), vmem_limit_bytes=VMEM_LIMIT_BYTES),
        name="peer_prep",
    )(x2, g.reshape(1, d), wq, _pack_row_pairs(k1), _pack_row_pairs(k2))


def _gate_times_act(re_ref, c_ref, e1_ref, a_ref, w_ref, slab0, blocks):
    for key1_rows, lc in blocks:
        lanes = slice(lc * LANES, (lc + 1) * LANES)
        slab = pl.ds(slab0 + lc, 1)

        def row_tile(ref, hh, i):
            return pltpu.bitcast(jnp.broadcast_to(ref[0, hh, i, slab, :], (SUBLANES, LANES)), BF16)

        cnt = {(i, hh): row_tile(c_ref, hh, i) for i in key1_rows for hh in range(P_HEADS)}
        e1 = {(i, hh): row_tile(e1_ref, hh, i) for i in key1_rows for hh in range(P_HEADS)}
        for jc in range(P_NKEYS // BF16_ROWS):
            gate = {i: None for i in key1_rows}
            for hh in range(P_HEADS):
                rank2 = pltpu.bitcast(re_ref[lc, jc, hh, 0:SUBLANES, :], BF16)
                e2 = pltpu.bitcast(re_ref[lc, jc, hh, SUBLANES:BF16_ROWS, :], BF16)
                for i in key1_rows:
                    term = jnp.where(rank2 < cnt[i, hh], e2, 0.0) * e1[i, hh]
                    gate[i] = term if gate[i] is None else gate[i] + term
            for i in key1_rows:
                rows = slice(i * P_NKEYS + jc * BF16_ROWS, i * P_NKEYS + (jc + 1) * BF16_ROWS)
                w_ref[rows, lanes] = gate[i] * _gelu(a_ref[rows, lanes]).astype(BF16)


def _peer_dense_kernel(ht_ref, re_ref, c_ref, e1_ref, u_ref, vt_ref, x_ref, fg_ref,
                       o_ref, a0_sc, a1_sc, w0_sc, w1_sc, acc_sc, *, n_exp_tiles, n_tiles, final_norm):
    k = pl.program_id(0)
    c_pair = jnp.clip(k - 2, 0, n_tiles - 1)
    c_exp = c_pair % n_exp_tiles

    @pl.when(k == 0)
    def _():
        a1_sc[...] = jnp.zeros_like(a1_sc)
        w0_sc[...] = jnp.zeros_like(w0_sc)

    @pl.when(c_exp == 0)
    def _():
        acc_sc[...] = jnp.zeros_like(acc_sc)

    def stages(a_out, a_in, w_out, w_in):
        tb = ht_ref.shape[1]
        eb = a_in.shape[0]
        d = acc_sc.shape[0]
        n_chunks = tb // MXU_COLS
        blocks = [((i, i + 1), lc) for i in range(0, P_ROWS_PER_STEP, 2) for lc in range(tb // LANES)]
        n_units = n_chunks * M_SPLIT * (eb // MXU_COLS + d // MXU_COLS)
        units_per_block = n_units // len(blocks)
        todo = iter(range(n_units))
        tiles_per_row_tile = PREP_SLABS * LANES // tb
        if tiles_per_row_tile == 1:
            slab0 = 0
        else:
            b_tok = jnp.clip(k - 1, 0, n_tiles - 1) // n_exp_tiles
            slab0 = (b_tok % tiles_per_row_tile) * (tb // LANES)

        def vpu_slice():
            unit = next(todo)
            if unit % units_per_block == 0:
                b0 = unit // units_per_block
                _gate_times_act(re_ref, c_ref, e1_ref, a_in, w_out, slab0, blocks[b0:b0 + 1])

        for n in range(n_chunks):
            cols = slice(n * MXU_COLS, (n + 1) * MXU_COLS)
            for mh in range(M_SPLIT):
                rows = slice(mh * d // M_SPLIT, (mh + 1) * d // M_SPLIT)
                words = slice(mh * d // (2 * M_SPLIT), (mh + 1) * d // (2 * M_SPLIT))
                t = acc_sc[rows, cols]
                for kt in range(eb // MXU_COLS):
                    ks = slice(kt * MXU_COLS, (kt + 1) * MXU_COLS)
                    t = t + jnp.dot(pltpu.bitcast(vt_ref[words, ks], BF16), w_in[ks, cols],
                                    preferred_element_type=F32)
                    vpu_slice()
                acc_sc[rows, cols] = t
            for mh in range(M_SPLIT):
                rows = slice(mh * eb // M_SPLIT, (mh + 1) * eb // M_SPLIT)
                words = slice(mh * eb // (2 * M_SPLIT), (mh + 1) * eb // (2 * M_SPLIT))
                t = None
                for kt in range(d // MXU_COLS):
                    ks = slice(kt * MXU_COLS, (kt + 1) * MXU_COLS)
                    kw = slice(kt * MXU_COLS // 2, (kt + 1) * MXU_COLS // 2)
                    p = jnp.dot(pltpu.bitcast(u_ref[words, ks], BF16), pltpu.bitcast(ht_ref[kw, cols], BF16),
                                preferred_element_type=F32)
                    t = p if t is None else t + p
                    vpu_slice()
                a_out[rows, cols] = t

    @pl.when(k % 2 == 0)
    def _():
        stages(a0_sc, a1_sc, w1_sc, w0_sc)

    @pl.when(k % 2 == 1)
    def _():
        stages(a1_sc, a0_sc, w0_sc, w1_sc)

    @pl.when(jnp.logical_and(k >= 2, c_exp == n_exp_tiles - 1))
    def _():
        y = x_ref[...] + acc_sc[...].T
        if final_norm:
            y = _rms(y, fg_ref[...])
        o_ref[...] = y


def _peer_dense(x2, ht, re, cnt, e1, u_tab, v_tab, final_g, final_norm, tb=1024):
    n, d = x2.shape
    n_exp = u_tab.shape[0]
    eb = P_ROWS_PER_STEP * P_NKEYS
    tb = min(tb, n)
    ne = n_exp // eb
    n_tiles = (n // tb) * ne
    last = n_tiles - 1

    def pair_a(k):
        return jnp.minimum(k, last)

    def pair_b(k):
        return jnp.clip(k - 1, 0, last)

    def pair_c(k):
        return jnp.clip(k - 2, 0, last)

    tiles_per_row_tile = PREP_SLABS * LANES // tb
    once = pl.Buffered(1)
    big = pl.BlockSpec((tb // LANES, P_NKEYS // BF16_ROWS, P_HEADS, BF16_ROWS, LANES),
                       lambda k: (pair_b(k) // ne, 0, 0, 0, 0), pipeline_mode=once)
    small = pl.BlockSpec((1, P_HEADS, P_ROWS_PER_STEP, PREP_SLABS, LANES),
                         lambda k: (pair_b(k) // ne // tiles_per_row_tile, 0, pair_b(k) % ne, 0, 0))
    return pl.pallas_call(
        functools.partial(_peer_dense_kernel, n_exp_tiles=ne, n_tiles=n_tiles, final_norm=final_norm),
        out_shape=jax.ShapeDtypeStruct((n, d), F32),
        grid=(n_tiles + 2,),
        in_specs=[
            pl.BlockSpec((d // 2, tb), lambda k: (0, pair_a(k) // ne), pipeline_mode=once),
            big, small, small,
            pl.BlockSpec((eb // 2, d), lambda k: (pair_a(k) % ne, 0)),
            pl.BlockSpec((d // 2, eb), lambda k: (0, pair_c(k) % ne)),
            pl.BlockSpec((tb, d), lambda k: (pair_c(k) // ne, 0), pipeline_mode=once),
            pl.BlockSpec((1, d), lambda k: (0, 0)),
        ],
        out_specs=pl.BlockSpec((tb, d), lambda k: (pair_c(k) // ne, 0)),
        scratch_shapes=[pltpu.VMEM((eb, tb), F32), pltpu.VMEM((eb, tb), F32),
                        pltpu.VMEM((eb, tb), BF16), pltpu.VMEM((eb, tb), BF16),
                        pltpu.VMEM((d, tb), F32)],
        compiler_params=pltpu.CompilerParams(
            dimension_semantics=("arbitrary",), vmem_limit_bytes=VMEM_LIMIT_BYTES),
        name="peer_dense",
    )(ht, re, cnt, e1, _pack_row_pairs(u_tab), _pack_row_pairs(v_tab, transpose=True), x2, final_g.reshape(1, d))


def _peer(x2, g, w_q, k1, k2, u_tab, v_tab, final_g, final_norm):
    ht, re, cnt, e1 = _peer_prep(x2, g, w_q, k1, k2)
    return _peer_dense(x2, ht, re, cnt, e1, u_tab, v_tab, final_g, final_norm)


def kernel(x, norm_mix, norm_ffn, final_norm, a_w_in, a_b_in, a_ln_g, a_ln_b, a_w_s, a_b_s, a_w_out,
           b_w_in, b_w_g2, b_b_g, b_norm, b_w_out, p_w_q, p_k1, p_k2, p_u, p_v):
    b, l, d = x.shape
    depth = norm_mix.shape[0]
    x2 = x.reshape(b * l, d)
    for i in range(depth):
        j = i // 2
        if i % 2 == 0:
            x2 = _mixer_a(x2, norm_mix[i], a_w_in[j], a_b_in[j], a_ln_g[j], a_ln_b[j],
                          a_w_s[j], a_b_s[j], a_w_out[j])
        else:
            x2 = _mixer_b(x2.reshape(b, l, d), norm_mix[i], b_w_in[j], b_w_g2[j], b_b_g[j],
                          b_norm[j], b_w_out[j]).reshape(b * l, d)
        x2 = _peer(x2, norm_ffn[i], p_w_q[i], p_k1[i], p_k2[i], p_u[i], p_v[i],
                   final_norm, i == depth - 1)
    return x2.reshape(b, l, d)
```

```python
import functools
import math

import jax
import jax.numpy as jnp
from jax import lax
from jax.experimental import pallas as pl
from jax.experimental.pallas import tpu as pltpu

F32 = jnp.float32
BF16 = jnp.bfloat16

NORM_EPS = 1e-6

LANES = 128
SUBLANES = 8
BF16_ROWS = 2 * SUBLANES
MXU_COLS = 256
PACK_BLOCK = 1024
M_SPLIT = 1
VMEM_LIMIT_BYTES = 56 * 1024 * 1024

A_GROUPS = 8
A_CHUNK = 128
B_HEADS = 4
B_GATE_RANK = 16
B_GATE_NORM = 16.0
B_CHUNK = 64
P_HEADS = 8
P_NKEYS = 128
P_TOPK = 16
P_ROWS_PER_STEP = 8
PREP_SLABS = SUBLANES


def _rms(x, g):
    return x * lax.rsqrt(jnp.mean(x * x, axis=-1, keepdims=True) + NORM_EPS) * g


def _gelu(x):
    return 0.5 * x * (1.0 + lax.erf(x * (1.0 / math.sqrt(2.0))))


def _split_bf16(x):
    hi = x.astype(BF16)
    lo = (x - hi.astype(F32)).astype(BF16)
    return hi, lo


def _pack_kernel(x_ref, o_ref, *, transpose):
    x = x_ref[...]
    if transpose:
        x = x.T
    o_ref[...] = pltpu.bitcast(x.astype(BF16), jnp.uint32)


def _pack_row_pairs(w, transpose=False, layer=None, tiled=False):
    rows, cols = w.shape[-2:]
    blk = min(PACK_BLOCK, rows)
    if layer is None:
        in_spec = pl.BlockSpec((blk, cols), lambda i: (i, 0))
    else:
        in_spec = pl.BlockSpec((None, blk, cols), lambda i: (layer, i, 0))
    if transpose and tiled:
        out_shape = jax.ShapeDtypeStruct((rows // blk, cols // 2, blk), jnp.uint32)
        out_spec = pl.BlockSpec((None, cols // 2, blk), lambda i: (i, 0, 0))
    elif transpose:
        out_shape = jax.ShapeDtypeStruct((cols // 2, rows), jnp.uint32)
        out_spec = pl.BlockSpec((cols // 2, blk), lambda i: (0, i))
    else:
        out_shape = jax.ShapeDtypeStruct((rows // 2, cols), jnp.uint32)
        out_spec = pl.BlockSpec((blk // 2, cols), lambda i: (i, 0))
    return pl.pallas_call(
        functools.partial(_pack_kernel, transpose=transpose),
        out_shape=out_shape,
        grid=(rows // blk,),
        in_specs=[in_spec],
        out_specs=out_spec,
        compiler_params=pltpu.CompilerParams(
            dimension_semantics=("arbitrary",), vmem_limit_bytes=VMEM_LIMIT_BYTES),
        name="pack_bf16_pairs",
    )(w)


def _const_spec(shape):
    nd = len(shape)
    return pl.BlockSpec(shape, lambda *_: (0,) * nd)


def _mixer_a_kernel(x_ref, g_ref, win_ref, bin_ref, lng_ref, lnb_ref, ws_ref, bs_ref, wout_ref,
                    o_ref, y_sc):
    tm = x_ref.shape[0]
    width = lng_ref.shape[1]
    gdim = width // A_GROUPS
    x = x_ref[...]
    h = _rms(x, g_ref[...]).astype(BF16)
    z = _gelu(jnp.dot(h, pltpu.bitcast(win_ref[...], BF16), preferred_element_type=F32) + bin_ref[...])
    u = z[:, :width]
    v = z[:, width:]
    mu = jnp.mean(v, axis=-1, keepdims=True)
    vc = v - mu
    var = jnp.mean(vc * vc, axis=-1, keepdims=True)
    vn = (vc * lax.rsqrt(var + NORM_EPS) * lng_ref[...] + lnb_ref[...]).astype(BF16)
    for c in range(tm // A_CHUNK):
        rows = slice(c * A_CHUNK, (c + 1) * A_CHUNK)
        for g in range(A_GROUPS):
            cols = slice(g * gdim, (g + 1) * gdim)
            sv = jnp.dot(ws_ref[g], vn[rows, cols], preferred_element_type=F32) + bs_ref[:, cols]
            y_sc[rows, cols] = (u[rows, cols] * sv).astype(BF16)
    o_ref[...] = x + jnp.dot(y_sc[...], pltpu.bitcast(wout_ref[...], BF16), preferred_element_type=F32)


def _mixer_a(x2, g, w_in, b_in, ln_g, ln_b, w_s, b_s, w_out, tm=256):
    n, d = x2.shape
    width = ln_g.shape[0]
    causal = jnp.tril(jnp.ones((A_CHUNK, A_CHUNK), dtype=bool))
    ws = jnp.where(causal[None], w_s, 0.0).astype(BF16)
    bs = jnp.repeat(b_s.T, width // A_GROUPS, axis=1)
    return pl.pallas_call(
        _mixer_a_kernel,
        out_shape=jax.ShapeDtypeStruct((n, d), F32),
        grid=(n // tm,),
        in_specs=[
            pl.BlockSpec((tm, d), lambda i: (i, 0)),
            _const_spec((1, d)),
            _const_spec((d // 2, 2 * width)),
            _const_spec((1, 2 * width)),
            _const_spec((1, width)),
            _const_spec((1, width)),
            _const_spec((A_GROUPS, A_CHUNK, A_CHUNK)),
            _const_spec((A_CHUNK, width)),
            _const_spec((width // 2, d)),
        ],
        out_specs=pl.BlockSpec((tm, d), lambda i: (i, 0)),
        scratch_shapes=[pltpu.VMEM((tm, width), BF16)],
        compiler_params=pltpu.CompilerParams(
            dimension_semantics=("arbitrary",), vmem_limit_bytes=VMEM_LIMIT_BYTES),
        name="mixer_a",
    )(x2, g.reshape(1, d), _pack_row_pairs(w_in), b_in.reshape(1, -1), ln_g.reshape(1, -1),
      ln_b.reshape(1, -1), ws, bs, _pack_row_pairs(w_out))


def _gla_kernel(x_ref, g_ref, win_ref, wg2_ref, bg_ref, ng_ref, wout_ref, tri_ref,
                o_ref, st_ref, oc_sc):
    nb, tc, d = x_ref.shape
    kdim = bg_ref.shape[1]
    vdim = ng_ref.shape[1]
    dk = kdim // B_HEADS
    dv = vdim // B_HEADS

    @pl.when(pl.program_id(0) == 0)
    def _():
        st_ref[...] = jnp.zeros_like(st_ref)

    x = x_ref[...].reshape(nb * tc, d)
    h = _rms(x, g_ref[...]).astype(BF16)
    proj = jnp.dot(h, pltpu.bitcast(win_ref[...], BF16), preferred_element_type=F32)
    q = proj[:, :kdim] * (dk ** -0.5)
    k = proj[:, kdim:2 * kdim]
    v = proj[:, 2 * kdim:2 * kdim + vdim]
    r = proj[:, 2 * kdim + vdim:2 * kdim + 2 * vdim]
    glr = proj[:, 2 * kdim + 2 * vdim:]

    g_hi, g_lo = _split_bf16(glr)
    w_hi = wg2_ref[0]
    w_lo = wg2_ref[1]
    xg = (jnp.dot(g_hi, w_hi, preferred_element_type=F32)
          + jnp.dot(g_hi, w_lo, preferred_element_type=F32)
          + jnp.dot(g_lo, w_hi, preferred_element_type=F32)) + bg_ref[...]
    log_a = (jnp.minimum(xg, 0.0) - jnp.log(1.0 + jnp.exp(-jnp.abs(xg)))) * (1.0 / B_GATE_NORM)

    a_hi, a_lo = _split_bf16(log_a)
    tri = tri_ref[...]
    gcum = jnp.concatenate(
        [jnp.dot(tri, a_hi[bi * tc:(bi + 1) * tc], preferred_element_type=F32)
         + jnp.dot(tri, a_lo[bi * tc:(bi + 1) * tc], preferred_element_type=F32)
         for bi in range(nb)], axis=0)

    eg = jnp.exp(gcum)
    qd = (q * eg).astype(BF16)
    kd = (k * jnp.exp(-gcum)).astype(BF16)
    vb = v.astype(BF16)
    row = lax.broadcasted_iota(jnp.int32, (B_CHUNK, B_CHUNK), 0)
    col = lax.broadcasted_iota(jnp.int32, (B_CHUNK, B_CHUNK), 1)
    causal = row >= col

    for c, bi, hh in [(c, bi, hh) for c in range(tc // B_CHUNK) for bi in range(nb) for hh in range(B_HEADS)]:
        rows = slice(bi * tc + c * B_CHUNK, bi * tc + (c + 1) * B_CHUNK)
        last = bi * tc + (c + 1) * B_CHUNK - 1
        if True:
            kc = slice(hh * dk, (hh + 1) * dk)
            vc = slice(hh * dv, (hh + 1) * dv)
            g_last = gcum[last:last + 1, kc]
            ks = (k[rows, kc] * jnp.exp(g_last - gcum[rows, kc])).astype(BF16)
            st = st_ref[bi, hh]
            sc = lax.dot_general(qd[rows, kc], kd[rows, kc], (((1,), (1,)), ((), ())),
                                 preferred_element_type=F32)
            sc = jnp.where(causal, sc, 0.0).astype(BF16)
            o = jnp.dot(sc, vb[rows, vc], preferred_element_type=F32)
            o = o + lax.dot_general(qd[rows, kc], st.astype(BF16), (((1,), (1,)), ((), ())),
                                    preferred_element_type=F32)
            oc_sc[rows, vc] = o
            upd = lax.dot_general(vb[rows, vc], ks, (((0,), (0,)), ((), ())),
                                  preferred_element_type=F32)
            st_ref[bi, hh] = st * jnp.exp(g_last) + upd

    ng = ng_ref[...]
    for hh in range(B_HEADS):
        vc = slice(hh * dv, (hh + 1) * dv)
        oh = oc_sc[:, vc]
        oh = oh * lax.rsqrt(jnp.mean(oh * oh, axis=-1, keepdims=True) + NORM_EPS) * ng[:, vc]
        rh = r[:, vc]
        oc_sc[:, vc] = oh * (rh / (1.0 + jnp.exp(-rh)))
    y = x + jnp.dot(oc_sc[...].astype(BF16), pltpu.bitcast(wout_ref[...], BF16),
                    preferred_element_type=F32)
    o_ref[...] = y.reshape(nb, tc, d)


def _mixer_b(x3, g, w_in, w_g2, b_g, norm_g, w_out, tc=128):
    b, l, d = x3.shape
    kdim = w_g2.shape[1]
    vdim = w_out.shape[0]
    main = 2 * kdim + 2 * vdim
    w_in_p = _pack_row_pairs(jnp.concatenate(
        [w_in, jnp.zeros((d, LANES - B_GATE_RANK), w_in.dtype)], axis=1))
    w_g2_p = jnp.concatenate([w_g2, jnp.zeros((LANES - B_GATE_RANK, kdim), w_g2.dtype)], axis=0)
    hi = w_g2_p.astype(BF16)
    lo = (w_g2_p - hi.astype(F32)).astype(BF16)
    wg2 = jnp.stack([hi, lo])
    ng = jnp.tile(norm_g, B_HEADS).reshape(1, vdim)
    idx = jnp.arange(tc)
    tri = ((idx[:, None] >= idx[None, :]) &
           (idx[:, None] // B_CHUNK == idx[None, :] // B_CHUNK)).astype(BF16)
    return pl.pallas_call(
        _gla_kernel,
        out_shape=jax.ShapeDtypeStruct((b, l, d), F32),
        grid=(l // tc,),
        in_specs=[
            pl.BlockSpec((b, tc, d), lambda j: (0, j, 0)),
            _const_spec((1, d)),
            _const_spec((d // 2, main + LANES)),
            _const_spec((2, LANES, kdim)),
            _const_spec((1, kdim)),
            _const_spec((1, vdim)),
            _const_spec((vdim // 2, d)),
            _const_spec((tc, tc)),
        ],
        out_specs=pl.BlockSpec((b, tc, d), lambda j: (0, j, 0)),
        scratch_shapes=[pltpu.VMEM((b, B_HEADS, vdim // B_HEADS, kdim // B_HEADS), F32),
                        pltpu.VMEM((b * tc, vdim), F32)],
        compiler_params=pltpu.CompilerParams(
            dimension_semantics=("arbitrary",), vmem_limit_bytes=VMEM_LIMIT_BYTES),
        name="mixer_b_gla",
    )(x3, g.reshape(1, d), w_in_p, wg2, b_g.reshape(1, kdim), ng, _pack_row_pairs(w_out), tri)


def _sort16_pairs():
    pairs = []

    def merge(lo, n, r):
        step = 2 * r
        if step < n:
            merge(lo, n, step)
            merge(lo + r, n, step)
            pairs.extend((i, i + r) for i in range(lo + r, lo + n - r, step))
        else:
            pairs.append((lo, lo + r))

    def sort(lo, n):
        if n > 1:
            sort(lo, n // 2)
            sort(lo + n // 2, n // 2)
            merge(lo, n, 1)

    sort(0, P_TOPK)
    return pairs


def _sort16(vals):
    vals = list(vals)
    for i, j in _sort16_pairs():
        vals[i], vals[j] = jnp.maximum(vals[i], vals[j]), jnp.minimum(vals[i], vals[j])
    return vals


def _merge_top16(a, b):
    c = list(a)
    for m, bv in enumerate(b):
        i = P_TOPK - 1 - m
        c[i] = jnp.maximum(c[i], bv)
    dist = P_TOPK // 2
    while dist:
        for i in range(P_TOPK):
            if not i & dist:
                c[i], c[i + dist] = jnp.maximum(c[i], c[i + dist]), jnp.minimum(c[i], c[i + dist])
        dist //= 2
    return c


def _top16_of_keys(s_sc, lo, hi):
    if hi - lo == P_TOPK:
        return _sort16([s_sc[k * SUBLANES:(k + 1) * SUBLANES, :] for k in range(lo, hi)])
    mid = (lo + hi) // 2
    return _merge_top16(_top16_of_keys(s_sc, lo, mid), _top16_of_keys(s_sc, mid, hi))


def _count_true_prefix(pred, thr):
    steps = (8, 4, 2, 1)
    masks = []

    def pick(ms, cands):
        if not ms:
            return cands[0]
        half = len(cands) // 2
        return jnp.where(ms[0], pick(ms[1:], cands[half:]), pick(ms[1:], cands[:half]))

    for lvl, step in enumerate(steps):
        cands = []
        for combo in range(2 ** lvl):
            taken = sum(s for bit, s in enumerate(steps[:lvl]) if (combo >> (lvl - 1 - bit)) & 1)
            cands.append(thr[taken + step - 1])
        masks.append(pred(pick(masks, cands)))
    count = None
    for m, step in zip(masks, steps):
        term = jnp.where(m, float(step), 0.0)
        count = term if count is None else count + term
    return jnp.where(pred(thr[P_TOPK - 1]), float(P_TOPK), count)


def _bf16_bits(x):
    return pltpu.bitcast(x.astype(BF16).astype(F32), jnp.uint32)


def _bf16_pair_bits(x):
    bits = _bf16_bits(x)
    return bits | (bits >> 16)


def _to_key_major(s, s_sc):
    for g in range(s.shape[0] // SUBLANES):
        for c in range(PREP_SLABS):
            s_sc[pl.ds(g * SUBLANES * PREP_SLABS + c, SUBLANES, stride=PREP_SLABS), :] = (
                s[g * SUBLANES:(g + 1) * SUBLANES, c * LANES:(c + 1) * LANES])


def _peer_prep_kernel(x_ref, g_ref, wq_ref, k1_ref, k2_ref,
                      ht_ref, re_ref, c_ref, e1_ref,
                      ht_sc, qt_sc, s1_sc, s2_sc, rw_sc, ew_sc):
    hh = pl.program_id(1)
    half = 2 * k1_ref.shape[0]

    @pl.when(hh == 0)
    def _():
        h = _rms(x_ref[...], g_ref[...])
        ht = h.T.astype(BF16)
        ht_ref[...] = pltpu.bitcast(ht, jnp.uint32)
        ht_sc[...] = ht

    qt_sc[...] = jnp.dot(pltpu.bitcast(wq_ref[0], BF16), ht_sc[...],
                         preferred_element_type=F32).astype(BF16)
    s1 = jnp.dot(pltpu.bitcast(k1_ref[...], BF16), qt_sc[:half, :], preferred_element_type=F32)
    s2 = jnp.dot(pltpu.bitcast(k2_ref[...], BF16), qt_sc[half:, :], preferred_element_type=F32)
    _to_key_major(s1, s1_sc)
    _to_key_major(s2, s2_sc)

    v1 = _top16_of_keys(s1_sc, 0, P_NKEYS)
    v2 = _top16_of_keys(s2_sc, 0, P_NKEYS)

    top = [v1[0] + v2[b] for b in range(P_TOPK)]
    for a in range(1, P_TOPK // 2):
        top = _merge_top16(top, [v1[a] + v2[b] for b in range(P_TOPK // (a + 1))])
    top = _merge_top16(top, [v1[a] + v2[0] for a in range(P_TOPK // 2, P_TOPK)])
    tau = top[P_TOPK - 1]
    z = None
    for m in range(P_TOPK):
        term = jnp.exp(top[m] - top[0])
        z = term if z is None else z + term
    inv_z = 1.0 / z

    for p in range(P_NKEYS // 2):
        rank_bits = []
        e2_bits = []
        for k in (2 * p, 2 * p + 1):
            tile = slice(k * SUBLANES, (k + 1) * SUBLANES)
            x2 = s2_sc[tile, :]
            rank_bits.append(pltpu.bitcast(_count_true_prefix(lambda t, x2=x2: t > x2, v2), jnp.uint32))
            e2_bits.append(_bf16_bits(jnp.exp(x2 - v2[0])))
            x1 = s1_sc[tile, :]
            cnt = _count_true_prefix(lambda t, x1=x1: x1 + t >= tau, v2)
            cnt_bits = pltpu.bitcast(cnt, jnp.uint32)
            c_ref[0, 0, k] = cnt_bits | (cnt_bits >> 16)
            e1_ref[0, 0, k] = _bf16_pair_bits(jnp.exp(x1 - v1[0]) * inv_z)
        words = slice(p * SUBLANES, (p + 1) * SUBLANES)
        rw_sc[words, :] = (rank_bits[0] >> 16) | rank_bits[1]
        ew_sc[words, :] = (e2_bits[0] >> 16) | e2_bits[1]

    for jc in range(P_NKEYS // BF16_ROWS):
        for c in range(PREP_SLABS):
            rows = pl.ds(jc * SUBLANES * PREP_SLABS + c, SUBLANES, stride=PREP_SLABS)
            re_ref[c, jc, 0, 0:SUBLANES, :] = rw_sc[rows, :]
            re_ref[c, jc, 0, SUBLANES:BF16_ROWS, :] = ew_sc[rows, :]


def _peer_prep(x2, g, w_q, k1, k2):
    n, d = x2.shape
    tt = PREP_SLABS * LANES
    qdim = w_q.shape[1] // P_HEADS
    chunks = P_NKEYS // BF16_ROWS
    re_shape = jax.ShapeDtypeStruct((n // LANES, chunks, P_HEADS, BF16_ROWS, LANES), jnp.uint32)
    re_spec = pl.BlockSpec((PREP_SLABS, chunks, 1, BF16_ROWS, LANES), lambda t, h: (t, 0, h, 0, 0))
    row_shape = jax.ShapeDtypeStruct((n // tt, P_HEADS, P_NKEYS, PREP_SLABS, LANES), jnp.uint32)
    row_spec = pl.BlockSpec((1, 1, P_NKEYS, PREP_SLABS, LANES), lambda t, h: (t, h, 0, 0, 0))
    wq = _pack_row_pairs(w_q, transpose=True).reshape(P_HEADS, qdim // 2, d)
    return pl.pallas_call(
        _peer_prep_kernel,
        out_shape=(jax.ShapeDtypeStruct((d // 2, n), jnp.uint32), re_shape, row_shape, row_shape),
        grid=(n // tt, P_HEADS),
        in_specs=[
            pl.BlockSpec((tt, d), lambda t, h: (t, 0)),
            _const_spec((1, d)),
            pl.BlockSpec((1, qdim // 2, d), lambda t, h: (h, 0, 0)),
            _const_spec((P_NKEYS // 2, k1.shape[1])),
            _const_spec((P_NKEYS // 2, k2.shape[1])),
        ],
        out_specs=(pl.BlockSpec((d // 2, tt), lambda t, h: (0, t)), re_spec, row_spec, row_spec),
        scratch_shapes=[pltpu.VMEM((d, tt), BF16),
                        pltpu.VMEM((qdim, tt), BF16),
                        pltpu.VMEM((P_NKEYS * PREP_SLABS, LANES), F32),
                        pltpu.VMEM((P_NKEYS * PREP_SLABS, LANES), F32),
                        pltpu.VMEM((P_NKEYS // 2 * PREP_SLABS, LANES), jnp.uint32),
                        pltpu.VMEM((P_NKEYS // 2 * PREP_SLABS, LANES), jnp.uint32)],
        compiler_params=pltpu.CompilerParams(
            dimension_semantics=("arbitrary", "arbitrary"), vmem_limit_bytes=VMEM_LIMIT_BYTES),
        name="peer_prep",
    )(x2, g.reshape(1, d), wq, _pack_row_pairs(k1), _pack_row_pairs(k2))


def _gate_times_act(re_ref, c_ref, e1_ref, a_ref, w_ref, slab0, blocks):
    for key1_rows, lc in blocks:
        lanes = slice(lc * LANES, (lc + 1) * LANES)
        slab = pl.ds(slab0 + lc, 1)

        def row_tile(ref, hh, i):
            return pltpu.bitcast(jnp.broadcast_to(ref[0, hh, i, slab, :], (SUBLANES, LANES)), BF16)

        cnt = {(i, hh): row_tile(c_ref, hh, i) for i in key1_rows for hh in range(P_HEADS)}
        e1 = {(i, hh): row_tile(e1_ref, hh, i) for i in key1_rows for hh in range(P_HEADS)}
        for jc in range(P_NKEYS // BF16_ROWS):
            gate = {i: None for i in key1_rows}
            for hh in range(P_HEADS):
                rank2 = pltpu.bitcast(re_ref[lc, jc, hh, 0:SUBLANES, :], BF16)
                e2 = pltpu.bitcast(re_ref[lc, jc, hh, SUBLANES:BF16_ROWS, :], BF16)
                for i in key1_rows:
                    term = jnp.where(rank2 < cnt[i, hh], e2, 0.0) * e1[i, hh]
                    gate[i] = term if gate[i] is None else gate[i] + term
            for i in key1_rows:
                rows = slice(i * P_NKEYS + jc * BF16_ROWS, i * P_NKEYS + (jc + 1) * BF16_ROWS)
                w_ref[rows, lanes] = gate[i] * _gelu(a_ref[rows, lanes]).astype(BF16)


def _peer_dense_kernel(ht_ref, re_ref, c_ref, e1_ref, u_ref, vt_ref, x_ref, fg_ref,
                       o_ref, a0_sc, a1_sc, w0_sc, w1_sc, acc_sc, *, n_exp_tiles, n_tiles, final_norm):
    k = pl.program_id(0)
    c_pair = jnp.clip(k - 2, 0, n_tiles - 1)
    c_exp = c_pair % n_exp_tiles

    @pl.when(k == 0)
    def _():
        a1_sc[...] = jnp.zeros_like(a1_sc)
        w0_sc[...] = jnp.zeros_like(w0_sc)

    @pl.when(c_exp == 0)
    def _():
        acc_sc[...] = jnp.zeros_like(acc_sc)

    def stages(a_out, a_in, w_out, w_in):
        tb = ht_ref.shape[1]
        eb = a_in.shape[0]
        d = acc_sc.shape[0]
        n_chunks = tb // MXU_COLS
        blocks = [((i, i + 1), lc) for i in range(0, P_ROWS_PER_STEP, 2) for lc in range(tb // LANES)]
        n_units = n_chunks * M_SPLIT * (eb // MXU_COLS + d // MXU_COLS)
        units_per_block = n_units // len(blocks)
        todo = iter(range(n_units))
        tiles_per_row_tile = PREP_SLABS * LANES // tb
        if tiles_per_row_tile == 1:
            slab0 = 0
        else:
            b_tok = jnp.clip(k - 1, 0, n_tiles - 1) // n_exp_tiles
            slab0 = (b_tok % tiles_per_row_tile) * (tb // LANES)

        def vpu_slice():
            unit = next(todo)
            if unit % units_per_block == 0:
                b0 = unit // units_per_block
                _gate_times_act(re_ref, c_ref, e1_ref, a_in, w_out, slab0, blocks[b0:b0 + 1])

        for n in range(n_chunks):
            cols = slice(n * MXU_COLS, (n + 1) * MXU_COLS)
            for mh in range(M_SPLIT):
                rows = slice(mh * d // M_SPLIT, (mh + 1) * d // M_SPLIT)
                words = slice(mh * d // (2 * M_SPLIT), (mh + 1) * d // (2 * M_SPLIT))
                t = acc_sc[rows, cols]
                for kt in range(eb // MXU_COLS):
                    ks = slice(kt * MXU_COLS, (kt + 1) * MXU_COLS)
                    t = t + jnp.dot(pltpu.bitcast(vt_ref[words, ks], BF16), w_in[ks, cols],
                                    preferred_element_type=F32)
                    vpu_slice()
                acc_sc[rows, cols] = t
            for mh in range(M_SPLIT):
                rows = slice(mh * eb // M_SPLIT, (mh + 1) * eb // M_SPLIT)
                words = slice(mh * eb // (2 * M_SPLIT), (mh + 1) * eb // (2 * M_SPLIT))
                t = None
                for kt in range(d // MXU_COLS):
                    ks = slice(kt * MXU_COLS, (kt + 1) * MXU_COLS)
                    kw = slice(kt * MXU_COLS // 2, (kt + 1) * MXU_COLS // 2)
                    p = jnp.dot(pltpu.bitcast(u_ref[words, ks], BF16), pltpu.bitcast(ht_ref[kw, cols], BF16),
                                preferred_element_type=F32)
                    t = p if t is None else t + p
                    vpu_slice()
                a_out[rows, cols] = t

    @pl.when(k % 2 == 0)
    def _():
        stages(a0_sc, a1_sc, w1_sc, w0_sc)

    @pl.when(k % 2 == 1)
    def _():
        stages(a1_sc, a0_sc, w0_sc, w1_sc)

    @pl.when(jnp.logical_and(k >= 2, c_exp == n_exp_tiles - 1))
    def _():
        y = x_ref[...] + acc_sc[...].T
        if final_norm:
            y = _rms(y, fg_ref[...])
        o_ref[...] = y


def _peer_dense(x2, ht, re, cnt, e1, u_all, v_all, layer, final_g, final_norm, tb=1024):
    n, d = x2.shape
    n_exp = u_all.shape[1]
    eb = P_ROWS_PER_STEP * P_NKEYS
    assert eb == PACK_BLOCK, "V^T arrives tiled by the packing kernel's block"
    tb = min(tb, n)
    ne = n_exp // eb
    n_tiles = (n // tb) * ne
    last = n_tiles - 1

    def pair_a(k):
        return jnp.minimum(k, last)

    def pair_b(k):
        return jnp.clip(k - 1, 0, last)

    def pair_c(k):
        return jnp.clip(k - 2, 0, last)

    tiles_per_row_tile = PREP_SLABS * LANES // tb
    big = pl.BlockSpec((tb // LANES, P_NKEYS // BF16_ROWS, P_HEADS, BF16_ROWS, LANES),
                       lambda k: (pair_b(k) // ne, 0, 0, 0, 0))
    small = pl.BlockSpec((1, P_HEADS, P_ROWS_PER_STEP, PREP_SLABS, LANES),
                         lambda k: (pair_b(k) // ne // tiles_per_row_tile, 0, pair_b(k) % ne, 0, 0))
    return pl.pallas_call(
        functools.partial(_peer_dense_kernel, n_exp_tiles=ne, n_tiles=n_tiles, final_norm=final_norm),
        out_shape=jax.ShapeDtypeStruct((n, d), F32),
        grid=(n_tiles + 2,),
        in_specs=[
            pl.BlockSpec((d // 2, tb), lambda k: (0, pair_a(k) // ne)),
            big, small, small,
            pl.BlockSpec((eb // 2, d), lambda k: (pair_a(k) % ne, 0)),
            pl.BlockSpec((None, d // 2, eb), lambda k: (pair_c(k) % ne, 0, 0)),
            pl.BlockSpec((tb, d), lambda k: (pair_c(k) // ne, 0), pipeline_mode=pl.Buffered(1)),
            pl.BlockSpec((1, d), lambda k: (0, 0)),
        ],
        out_specs=pl.BlockSpec((tb, d), lambda k: (pair_c(k) // ne, 0)),
        scratch_shapes=[pltpu.VMEM((eb, tb), F32), pltpu.VMEM((eb, tb), F32),
                        pltpu.VMEM((eb, tb), BF16), pltpu.VMEM((eb, tb), BF16),
                        pltpu.VMEM((d, tb), F32)],
        compiler_params=pltpu.CompilerParams(
            dimension_semantics=("arbitrary",), vmem_limit_bytes=VMEM_LIMIT_BYTES),
        name="peer_dense",
    )(ht, re, cnt, e1, _pack_row_pairs(u_all, layer=layer),
      _pack_row_pairs(v_all, transpose=True, layer=layer, tiled=True), x2, final_g.reshape(1, d))


def _peer(x2, g, w_q, k1, k2, u_all, v_all, layer, final_g, final_norm):
    ht, re, cnt, e1 = _peer_prep(x2, g, w_q, k1, k2)
    return _peer_dense(x2, ht, re, cnt, e1, u_all, v_all, layer, final_g, final_norm)


def kernel(x, norm_mix, norm_ffn, final_norm, a_w_in, a_b_in, a_ln_g, a_ln_b, a_w_s, a_b_s, a_w_out,
           b_w_in, b_w_g2, b_b_g, b_norm, b_w_out, p_w_q, p_k1, p_k2, p_u, p_v):
    b, l, d = x.shape
    depth = norm_mix.shape[0]
    x2 = x.reshape(b * l, d)
    for i in range(depth):
        j = i // 2
        if i % 2 == 0:
            x2 = _mixer_a(x2, norm_mix[i], a_w_in[j], a_b_in[j], a_ln_g[j], a_ln_b[j],
                          a_w_s[j], a_b_s[j], a_w_out[j])
        else:
            x2 = _mixer_b(x2.reshape(b, l, d), norm_mix[i], b_w_in[j], b_w_g2[j], b_b_g[j],
                          b_norm[j], b_w_out[j]).reshape(b * l, d)
        x2 = _peer(x2, norm_ffn[i], p_w_q[i], p_k1[i], p_k2[i], p_u, p_v, i,
                   final_norm, i == depth - 1)
    return x2.reshape(b, l, d)
```

```python
import functools
import math

import jax
import jax.numpy as jnp
from jax import lax
from jax.experimental import pallas as pl
from jax.experimental.pallas import tpu as pltpu

F32 = jnp.float32
BF16 = jnp.bfloat16

NORM_EPS = 1e-6

LANES = 128
SUBLANES = 8
BF16_ROWS = 2 * SUBLANES
MXU_COLS = 256
PACK_BLOCK = 1024
M_SPLIT = 1
VMEM_LIMIT_BYTES = 56 * 1024 * 1024

A_GROUPS = 8
A_CHUNK = 128
B_HEADS = 4
B_GATE_RANK = 16
B_GATE_NORM = 16.0
B_CHUNK = 64
P_HEADS = 8
P_NKEYS = 128
P_TOPK = 16
P_ROWS_PER_STEP = 8
PREP_SLABS = SUBLANES


def _rms(x, g):
    return x * lax.rsqrt(jnp.mean(x * x, axis=-1, keepdims=True) + NORM_EPS) * g


def _gelu(x):
    return 0.5 * x * (1.0 + lax.erf(x * (1.0 / math.sqrt(2.0))))


def _split_bf16(x):
    hi = x.astype(BF16)
    lo = (x - hi.astype(F32)).astype(BF16)
    return hi, lo


def _pack_kernel(x_ref, o_ref, *, transpose):
    x = x_ref[...]
    if transpose:
        x = x.T
    o_ref[...] = pltpu.bitcast(x.astype(BF16), jnp.uint32)


def _pack_row_pairs(w, transpose=False, layer=None, tiled=False):
    rows, cols = w.shape[-2:]
    blk = min(PACK_BLOCK, rows)
    if layer is None:
        in_spec = pl.BlockSpec((blk, cols), lambda i: (i, 0))
    else:
        in_spec = pl.BlockSpec((None, blk, cols), lambda i: (layer, i, 0))
    if transpose and tiled:
        out_shape = jax.ShapeDtypeStruct((rows // blk, cols // 2, blk), jnp.uint32)
        out_spec = pl.BlockSpec((None, cols // 2, blk), lambda i: (i, 0, 0))
    elif transpose:
        out_shape = jax.ShapeDtypeStruct((cols // 2, rows), jnp.uint32)
        out_spec = pl.BlockSpec((cols // 2, blk), lambda i: (0, i))
    else:
        out_shape = jax.ShapeDtypeStruct((rows // 2, cols), jnp.uint32)
        out_spec = pl.BlockSpec((blk // 2, cols), lambda i: (i, 0))
    return pl.pallas_call(
        functools.partial(_pack_kernel, transpose=transpose),
        out_shape=out_shape,
        grid=(rows // blk,),
        in_specs=[in_spec],
        out_specs=out_spec,
        compiler_params=pltpu.CompilerParams(
            dimension_semantics=("arbitrary",), vmem_limit_bytes=VMEM_LIMIT_BYTES),
        name="pack_bf16_pairs",
    )(w)


def _const_spec(shape):
    nd = len(shape)
    return pl.BlockSpec(shape, lambda *_: (0,) * nd)


def _mixer_a_kernel(x_ref, g_ref, win_ref, bin_ref, lng_ref, lnb_ref, ws_ref, bs_ref, wout_ref,
                    o_ref, y_sc):
    tm = x_ref.shape[0]
    width = lng_ref.shape[1]
    gdim = width // A_GROUPS
    x = x_ref[...]
    h = _rms(x, g_ref[...]).astype(BF16)
    z = _gelu(jnp.dot(h, pltpu.bitcast(win_ref[...], BF16), preferred_element_type=F32) + bin_ref[...])
    u = z[:, :width]
    v = z[:, width:]
    mu = jnp.mean(v, axis=-1, keepdims=True)
    vc = v - mu
    var = jnp.mean(vc * vc, axis=-1, keepdims=True)
    vn = (vc * lax.rsqrt(var + NORM_EPS) * lng_ref[...] + lnb_ref[...]).astype(BF16)
    for c in range(tm // A_CHUNK):
        rows = slice(c * A_CHUNK, (c + 1) * A_CHUNK)
        for g in range(A_GROUPS):
            cols = slice(g * gdim, (g + 1) * gdim)
            sv = jnp.dot(ws_ref[g], vn[rows, cols], preferred_element_type=F32) + bs_ref[:, cols]
            y_sc[rows, cols] = (u[rows, cols] * sv).astype(BF16)
    o_ref[...] = x + jnp.dot(y_sc[...], pltpu.bitcast(wout_ref[...], BF16), preferred_element_type=F32)


def _mixer_a(x2, g, w_in, b_in, ln_g, ln_b, w_s, b_s, w_out, tm=256):
    n, d = x2.shape
    width = ln_g.shape[0]
    causal = jnp.tril(jnp.ones((A_CHUNK, A_CHUNK), dtype=bool))
    ws = jnp.where(causal[None], w_s, 0.0).astype(BF16)
    bs = jnp.repeat(b_s.T, width // A_GROUPS, axis=1)
    return pl.pallas_call(
        _mixer_a_kernel,
        out_shape=jax.ShapeDtypeStruct((n, d), F32),
        grid=(n // tm,),
        in_specs=[
            pl.BlockSpec((tm, d), lambda i: (i, 0)),
            _const_spec((1, d)),
            _const_spec((d // 2, 2 * width)),
            _const_spec((1, 2 * width)),
            _const_spec((1, width)),
            _const_spec((1, width)),
            _const_spec((A_GROUPS, A_CHUNK, A_CHUNK)),
            _const_spec((A_CHUNK, width)),
            _const_spec((width // 2, d)),
        ],
        out_specs=pl.BlockSpec((tm, d), lambda i: (i, 0)),
        scratch_shapes=[pltpu.VMEM((tm, width), BF16)],
        compiler_params=pltpu.CompilerParams(
            dimension_semantics=("arbitrary",), vmem_limit_bytes=VMEM_LIMIT_BYTES),
        name="mixer_a",
    )(x2, g.reshape(1, d), _pack_row_pairs(w_in), b_in.reshape(1, -1), ln_g.reshape(1, -1),
      ln_b.reshape(1, -1), ws, bs, _pack_row_pairs(w_out))


def _gla_kernel(x_ref, g_ref, win_ref, wg2_ref, bg_ref, ng_ref, wout_ref, tri_ref,
                o_ref, st_ref, oc_sc):
    nb, tc, d = x_ref.shape
    kdim = bg_ref.shape[1]
    vdim = ng_ref.shape[1]
    dk = kdim // B_HEADS
    dv = vdim // B_HEADS

    @pl.when(pl.program_id(0) == 0)
    def _():
        st_ref[...] = jnp.zeros_like(st_ref)

    x = x_ref[...].reshape(nb * tc, d)
    h = _rms(x, g_ref[...]).astype(BF16)
    proj = jnp.dot(h, pltpu.bitcast(win_ref[...], BF16), preferred_element_type=F32)
    q = proj[:, :kdim] * (dk ** -0.5)
    k = proj[:, kdim:2 * kdim]
    v = proj[:, 2 * kdim:2 * kdim + vdim]
    r = proj[:, 2 * kdim + vdim:2 * kdim + 2 * vdim]
    glr = proj[:, 2 * kdim + 2 * vdim:]

    g_hi, g_lo = _split_bf16(glr)
    w_hi = wg2_ref[0]
    w_lo = wg2_ref[1]
    xg = (jnp.dot(g_hi, w_hi, preferred_element_type=F32)
          + jnp.dot(g_hi, w_lo, preferred_element_type=F32)
          + jnp.dot(g_lo, w_hi, preferred_element_type=F32)) + bg_ref[...]
    log_a = (jnp.minimum(xg, 0.0) - jnp.log(1.0 + jnp.exp(-jnp.abs(xg)))) * (1.0 / B_GATE_NORM)

    a_hi, a_lo = _split_bf16(log_a)
    tri = tri_ref[...]
    gcum = jnp.concatenate(
        [jnp.dot(tri, a_hi[bi * tc:(bi + 1) * tc], preferred_element_type=F32)
         + jnp.dot(tri, a_lo[bi * tc:(bi + 1) * tc], preferred_element_type=F32)
         for bi in range(nb)], axis=0)

    eg = jnp.exp(gcum)
    qd = (q * eg).astype(BF16)
    kd = (k * jnp.exp(-gcum)).astype(BF16)
    vb = v.astype(BF16)
    row = lax.broadcasted_iota(jnp.int32, (B_CHUNK, B_CHUNK), 0)
    col = lax.broadcasted_iota(jnp.int32, (B_CHUNK, B_CHUNK), 1)
    causal = row >= col

    nt = (((1,), (1,)), ((), ()))
    tn = (((0,), (0,)), ((), ()))
    for c in range(tc // B_CHUNK):
        units = []
        for bi in range(nb):
            rows = slice(bi * tc + c * B_CHUNK, bi * tc + (c + 1) * B_CHUNK)
            last = bi * tc + (c + 1) * B_CHUNK - 1
            for hh in range(B_HEADS):
                units.append((bi, hh, rows, last, slice(hh * dk, (hh + 1) * dk), slice(hh * dv, (hh + 1) * dv)))
        scores = [lax.dot_general(qd[rows, kc], kd[rows, kc], nt, preferred_element_type=F32)
                  for _, _, rows, _, kc, _ in units]
        inter = [lax.dot_general(qd[rows, kc], st_ref[bi, hh].astype(BF16), nt, preferred_element_type=F32)
                 for bi, hh, rows, _, kc, _ in units]
        g_last = [gcum[last:last + 1, kc] for _, _, _, last, kc, _ in units]
        ks = [(k[rows, kc] * jnp.exp(gl - gcum[rows, kc])).astype(BF16)
              for (_, _, rows, _, kc, _), gl in zip(units, g_last)]
        upd = [lax.dot_general(vb[rows, vc], kk, tn, preferred_element_type=F32)
               for (_, _, rows, _, _, vc), kk in zip(units, ks)]
        for (bi, hh, rows, _, _, vc), sc, io in zip(units, scores, inter):
            sc = jnp.where(causal, sc, 0.0).astype(BF16)
            oc_sc[rows, vc] = jnp.dot(sc, vb[rows, vc], preferred_element_type=F32) + io
        for (bi, hh, _, _, _, _), gl, up in zip(units, g_last, upd):
            st_ref[bi, hh] = st_ref[bi, hh] * jnp.exp(gl) + up

    ng = ng_ref[...]
    for hh in range(B_HEADS):
        vc = slice(hh * dv, (hh + 1) * dv)
        oh = oc_sc[:, vc]
        oh = oh * lax.rsqrt(jnp.mean(oh * oh, axis=-1, keepdims=True) + NORM_EPS) * ng[:, vc]
        rh = r[:, vc]
        oc_sc[:, vc] = oh * (rh / (1.0 + jnp.exp(-rh)))
    y = x + jnp.dot(oc_sc[...].astype(BF16), pltpu.bitcast(wout_ref[...], BF16),
                    preferred_element_type=F32)
    o_ref[...] = y.reshape(nb, tc, d)


def _mixer_b(x3, g, w_in, w_g2, b_g, norm_g, w_out, tc=128):
    b, l, d = x3.shape
    kdim = w_g2.shape[1]
    vdim = w_out.shape[0]
    main = 2 * kdim + 2 * vdim
    w_in_p = _pack_row_pairs(jnp.concatenate(
        [w_in, jnp.zeros((d, LANES - B_GATE_RANK), w_in.dtype)], axis=1))
    w_g2_p = jnp.concatenate([w_g2, jnp.zeros((LANES - B_GATE_RANK, kdim), w_g2.dtype)], axis=0)
    hi = w_g2_p.astype(BF16)
    lo = (w_g2_p - hi.astype(F32)).astype(BF16)
    wg2 = jnp.stack([hi, lo])
    ng = jnp.tile(norm_g, B_HEADS).reshape(1, vdim)
    idx = jnp.arange(tc)
    tri = ((idx[:, None] >= idx[None, :]) &
           (idx[:, None] // B_CHUNK == idx[None, :] // B_CHUNK)).astype(BF16)
    return pl.pallas_call(
        _gla_kernel,
        out_shape=jax.ShapeDtypeStruct((b, l, d), F32),
        grid=(l // tc,),
        in_specs=[
            pl.BlockSpec((b, tc, d), lambda j: (0, j, 0)),
            _const_spec((1, d)),
            _const_spec((d // 2, main + LANES)),
            _const_spec((2, LANES, kdim)),
            _const_spec((1, kdim)),
            _const_spec((1, vdim)),
            _const_spec((vdim // 2, d)),
            _const_spec((tc, tc)),
        ],
        out_specs=pl.BlockSpec((b, tc, d), lambda j: (0, j, 0)),
        scratch_shapes=[pltpu.VMEM((b, B_HEADS, vdim // B_HEADS, kdim // B_HEADS), F32),
                        pltpu.VMEM((b * tc, vdim), F32)],
        compiler_params=pltpu.CompilerParams(
            dimension_semantics=("arbitrary",), vmem_limit_bytes=VMEM_LIMIT_BYTES),
        name="mixer_b_gla",
    )(x3, g.reshape(1, d), w_in_p, wg2, b_g.reshape(1, kdim), ng, _pack_row_pairs(w_out), tri)


def _sort16_pairs():
    pairs = []

    def merge(lo, n, r):
        step = 2 * r
        if step < n:
            merge(lo, n, step)
            merge(lo + r, n, step)
            pairs.extend((i, i + r) for i in range(lo + r, lo + n - r, step))
        else:
            pairs.append((lo, lo + r))

    def sort(lo, n):
        if n > 1:
            sort(lo, n // 2)
            sort(lo + n // 2, n // 2)
            merge(lo, n, 1)

    sort(0, P_TOPK)
    return pairs


def _sort16(vals):
    vals = list(vals)
    for i, j in _sort16_pairs():
        vals[i], vals[j] = jnp.maximum(vals[i], vals[j]), jnp.minimum(vals[i], vals[j])
    return vals


def _merge_top16(a, b):
    c = list(a)
    for m, bv in enumerate(b):
        i = P_TOPK - 1 - m
        c[i] = jnp.maximum(c[i], bv)
    dist = P_TOPK // 2
    while dist:
        for i in range(P_TOPK):
            if not i & dist:
                c[i], c[i + dist] = jnp.maximum(c[i], c[i + dist]), jnp.minimum(c[i], c[i + dist])
        dist //= 2
    return c


def _top16_of_keys(s_sc, lo, hi):
    if hi - lo == P_TOPK:
        return _sort16([s_sc[k * SUBLANES:(k + 1) * SUBLANES, :] for k in range(lo, hi)])
    mid = (lo + hi) // 2
    return _merge_top16(_top16_of_keys(s_sc, lo, mid), _top16_of_keys(s_sc, mid, hi))


def _count_true_prefix(pred, thr):
    steps = (8, 4, 2, 1)
    masks = []

    def pick(ms, cands):
        if not ms:
            return cands[0]
        half = len(cands) // 2
        return jnp.where(ms[0], pick(ms[1:], cands[half:]), pick(ms[1:], cands[:half]))

    for lvl, step in enumerate(steps):
        cands = []
        for combo in range(2 ** lvl):
            taken = sum(s for bit, s in enumerate(steps[:lvl]) if (combo >> (lvl - 1 - bit)) & 1)
            cands.append(thr[taken + step - 1])
        masks.append(pred(pick(masks, cands)))
    count = None
    for m, step in zip(masks, steps):
        term = jnp.where(m, float(step), 0.0)
        count = term if count is None else count + term
    return jnp.where(pred(thr[P_TOPK - 1]), float(P_TOPK), count)


def _bf16_bits(x):
    return pltpu.bitcast(x.astype(BF16).astype(F32), jnp.uint32)


def _bf16_pair_bits(x):
    bits = _bf16_bits(x)
    return bits | (bits >> 16)


def _to_key_major(s, s_sc, slab0):
    for g in range(s.shape[0] // SUBLANES):
        for c in range(s.shape[1] // LANES):
            s_sc[pl.ds(g * SUBLANES * PREP_SLABS + slab0 + c, SUBLANES, stride=PREP_SLABS), :] = (
                s[g * SUBLANES:(g + 1) * SUBLANES, c * LANES:(c + 1) * LANES])


def _scores_to_scratch(wq_words, ht_sc, k1_ref, k2_ref, s1_sc, s2_sc, part):
    half = 2 * k1_ref.shape[0]
    cols = slice(part * MXU_COLS, (part + 1) * MXU_COLS)
    qt = jnp.dot(pltpu.bitcast(wq_words, BF16), ht_sc[:, cols],
                 preferred_element_type=F32).astype(BF16)
    s1 = jnp.dot(pltpu.bitcast(k1_ref[...], BF16), qt[:half], preferred_element_type=F32)
    s2 = jnp.dot(pltpu.bitcast(k2_ref[...], BF16), qt[half:], preferred_element_type=F32)
    _to_key_major(s1, s1_sc, part * MXU_COLS // LANES)
    _to_key_major(s2, s2_sc, part * MXU_COLS // LANES)


def _peer_prep_kernel(x_ref, g_ref, wq0_ref, wqn_ref, k1_ref, k2_ref,
                      ht_ref, re_ref, c_ref, e1_ref,
                      ht_sc, s1a_sc, s2a_sc, s1b_sc, s2b_sc, rw_sc, ew_sc):
    hh = pl.program_id(1)
    parts = ht_sc.shape[1] // MXU_COLS

    @pl.when(hh == 0)
    def _():
        h = _rms(x_ref[...], g_ref[...])
        ht = h.T.astype(BF16)
        ht_ref[...] = pltpu.bitcast(ht, jnp.uint32)
        ht_sc[...] = ht
        for part in range(parts):
            _scores_to_scratch(wq0_ref[0], ht_sc, k1_ref, k2_ref, s1a_sc, s2a_sc, part)

    @pl.when(hh % 2 == 0)
    def _():
        _select_head(wqn_ref, ht_sc, k1_ref, k2_ref, re_ref, c_ref, e1_ref,
                     s1a_sc, s2a_sc, s1b_sc, s2b_sc, rw_sc, ew_sc, parts)

    @pl.when(hh % 2 == 1)
    def _():
        _select_head(wqn_ref, ht_sc, k1_ref, k2_ref, re_ref, c_ref, e1_ref,
                     s1b_sc, s2b_sc, s1a_sc, s2a_sc, rw_sc, ew_sc, parts)


def _select_head(wqn_ref, ht_sc, k1_ref, k2_ref, re_ref, c_ref, e1_ref,
                 s1_sc, s2_sc, s1_next, s2_next, rw_sc, ew_sc, parts):
    next_parts = iter(range(parts))

    def next_head_scores():
        _scores_to_scratch(wqn_ref[0], ht_sc, k1_ref, k2_ref, s1_next, s2_next, next(next_parts))

    v1 = _top16_of_keys(s1_sc, 0, P_NKEYS)
    next_head_scores()
    v2 = _top16_of_keys(s2_sc, 0, P_NKEYS)
    next_head_scores()

    top = [v1[0] + v2[b] for b in range(P_TOPK)]
    for a in range(1, P_TOPK // 2):
        top = _merge_top16(top, [v1[a] + v2[b] for b in range(P_TOPK // (a + 1))])
    top = _merge_top16(top, [v1[a] + v2[0] for a in range(P_TOPK // 2, P_TOPK)])
    tau = top[P_TOPK - 1]
    z = None
    for m in range(P_TOPK):
        term = jnp.exp(top[m] - top[0])
        z = term if z is None else z + term
    inv_z = 1.0 / z

    pairs = P_NKEYS // 2
    later = [(j + 1) * pairs // (parts - 1) for j in range(parts - 2)]
    for p in range(pairs):
        if p in later:
            next_head_scores()
        rank_bits = []
        e2_bits = []
        for k in (2 * p, 2 * p + 1):
            tile = slice(k * SUBLANES, (k + 1) * SUBLANES)
            x2 = s2_sc[tile, :]
            rank_bits.append(pltpu.bitcast(_count_true_prefix(lambda t, x2=x2: t > x2, v2), jnp.uint32))
            e2_bits.append(_bf16_bits(jnp.exp(x2 - v2[0])))
            x1 = s1_sc[tile, :]
            cnt = _count_true_prefix(lambda t, x1=x1: x1 + t >= tau, v2)
            cnt_bits = pltpu.bitcast(cnt, jnp.uint32)
            c_ref[0, 0, k] = cnt_bits | (cnt_bits >> 16)
            e1_ref[0, 0, k] = _bf16_pair_bits(jnp.exp(x1 - v1[0]) * inv_z)
        words = slice(p * SUBLANES, (p + 1) * SUBLANES)
        rw_sc[words, :] = (rank_bits[0] >> 16) | rank_bits[1]
        ew_sc[words, :] = (e2_bits[0] >> 16) | e2_bits[1]

    for jc in range(P_NKEYS // BF16_ROWS):
        for c in range(PREP_SLABS):
            rows = pl.ds(jc * SUBLANES * PREP_SLABS + c, SUBLANES, stride=PREP_SLABS)
            re_ref[c, jc, 0, 0:SUBLANES, :] = rw_sc[rows, :]
            re_ref[c, jc, 0, SUBLANES:BF16_ROWS, :] = ew_sc[rows, :]


def _peer_prep(x2, g, w_q, k1, k2):
    n, d = x2.shape
    tt = PREP_SLABS * LANES
    qdim = w_q.shape[1] // P_HEADS
    chunks = P_NKEYS // BF16_ROWS
    re_shape = jax.ShapeDtypeStruct((n // LANES, chunks, P_HEADS, BF16_ROWS, LANES), jnp.uint32)
    re_spec = pl.BlockSpec((PREP_SLABS, chunks, 1, BF16_ROWS, LANES), lambda t, h: (t, 0, h, 0, 0))
    row_shape = jax.ShapeDtypeStruct((n // tt, P_HEADS, P_NKEYS, PREP_SLABS, LANES), jnp.uint32)
    row_spec = pl.BlockSpec((1, 1, P_NKEYS, PREP_SLABS, LANES), lambda t, h: (t, h, 0, 0, 0))
    wq = _pack_row_pairs(w_q, transpose=True).reshape(P_HEADS, qdim // 2, d)
    return pl.pallas_call(
        _peer_prep_kernel,
        out_shape=(jax.ShapeDtypeStruct((d // 2, n), jnp.uint32), re_shape, row_shape, row_shape),
        grid=(n // tt, P_HEADS),
        in_specs=[
            pl.BlockSpec((tt, d), lambda t, h: (t, 0)),
            _const_spec((1, d)),
            pl.BlockSpec((1, qdim // 2, d), lambda t, h: (0, 0, 0)),
            pl.BlockSpec((1, qdim // 2, d), lambda t, h: (jnp.minimum(h + 1, P_HEADS - 1), 0, 0)),
            _const_spec((P_NKEYS // 2, k1.shape[1])),
            _const_spec((P_NKEYS // 2, k2.shape[1])),
        ],
        out_specs=(pl.BlockSpec((d // 2, tt), lambda t, h: (0, t)), re_spec, row_spec, row_spec),
        scratch_shapes=[pltpu.VMEM((d, tt), BF16),
                        pltpu.VMEM((P_NKEYS * PREP_SLABS, LANES), F32),
                        pltpu.VMEM((P_NKEYS * PREP_SLABS, LANES), F32),
                        pltpu.VMEM((P_NKEYS * PREP_SLABS, LANES), F32),
                        pltpu.VMEM((P_NKEYS * PREP_SLABS, LANES), F32),
                        pltpu.VMEM((P_NKEYS // 2 * PREP_SLABS, LANES), jnp.uint32),
                        pltpu.VMEM((P_NKEYS // 2 * PREP_SLABS, LANES), jnp.uint32)],
        compiler_params=pltpu.CompilerParams(
            dimension_semantics=("arbitrary", "arbitrary"), vmem_limit_bytes=VMEM_LIMIT_BYTES),
        name="peer_prep",
    )(x2, g.reshape(1, d), wq, wq, _pack_row_pairs(k1), _pack_row_pairs(k2))


def _gate_times_act(re_ref, c_ref, e1_ref, a_ref, w_ref, slab0, blocks):
    for key1_rows, lc in blocks:
        lanes = slice(lc * LANES, (lc + 1) * LANES)
        slab = pl.ds(slab0 + lc, 1)

        def row_tile(ref, hh, i):
            return pltpu.bitcast(jnp.broadcast_to(ref[0, hh, i, slab, :], (SUBLANES, LANES)), BF16)

        cnt = {(i, hh): row_tile(c_ref, hh, i) for i in key1_rows for hh in range(P_HEADS)}
        e1 = {(i, hh): row_tile(e1_ref, hh, i) for i in key1_rows for hh in range(P_HEADS)}
        for jc in range(P_NKEYS // BF16_ROWS):
            gate = {i: None for i in key1_rows}
            for hh in range(P_HEADS):
                rank2 = pltpu.bitcast(re_ref[lc, jc, hh, 0:SUBLANES, :], BF16)
                e2 = pltpu.bitcast(re_ref[lc, jc, hh, SUBLANES:BF16_ROWS, :], BF16)
                for i in key1_rows:
                    term = jnp.where(rank2 < cnt[i, hh], e2, 0.0) * e1[i, hh]
                    gate[i] = term if gate[i] is None else gate[i] + term
            for i in key1_rows:
                rows = slice(i * P_NKEYS + jc * BF16_ROWS, i * P_NKEYS + (jc + 1) * BF16_ROWS)
                w_ref[rows, lanes] = gate[i] * _gelu(a_ref[rows, lanes]).astype(BF16)


def _peer_dense_kernel(ht_ref, re_ref, c_ref, e1_ref, u_ref, vt_ref, x_ref, fg_ref,
                       o_ref, a0_sc, a1_sc, w0_sc, w1_sc, acc_sc, *, n_exp_tiles, n_tiles, final_norm):
    k = pl.program_id(0)
    c_pair = jnp.clip(k - 2, 0, n_tiles - 1)
    c_exp = c_pair % n_exp_tiles

    @pl.when(k == 0)
    def _():
        a1_sc[...] = jnp.zeros_like(a1_sc)
        w0_sc[...] = jnp.zeros_like(w0_sc)

    @pl.when(c_exp == 0)
    def _():
        acc_sc[...] = jnp.zeros_like(acc_sc)

    def stages(a_out, a_in, w_out, w_in):
        tb = ht_ref.shape[1]
        eb = a_in.shape[0]
        d = acc_sc.shape[0]
        n_chunks = tb // MXU_COLS
        blocks = [((i, i + 1), lc) for i in range(0, P_ROWS_PER_STEP, 2) for lc in range(tb // LANES)]
        n_units = n_chunks * M_SPLIT * (eb // MXU_COLS + d // MXU_COLS)
        units_per_block = n_units // len(blocks)
        todo = iter(range(n_units))
        tiles_per_row_tile = PREP_SLABS * LANES // tb
        if tiles_per_row_tile == 1:
            slab0 = 0
        else:
            b_tok = jnp.clip(k - 1, 0, n_tiles - 1) // n_exp_tiles
            slab0 = (b_tok % tiles_per_row_tile) * (tb // LANES)

        def vpu_slice():
            unit = next(todo)
            if unit % units_per_block == 0:
                b0 = unit // units_per_block
                _gate_times_act(re_ref, c_ref, e1_ref, a_in, w_out, slab0, blocks[b0:b0 + 1])

        for n in range(n_chunks):
            cols = slice(n * MXU_COLS, (n + 1) * MXU_COLS)
            for mh in range(M_SPLIT):
                rows = slice(mh * d // M_SPLIT, (mh + 1) * d // M_SPLIT)
                words = slice(mh * d // (2 * M_SPLIT), (mh + 1) * d // (2 * M_SPLIT))
                t = acc_sc[rows, cols]
                for kt in range(eb // MXU_COLS):
                    ks = slice(kt * MXU_COLS, (kt + 1) * MXU_COLS)
                    t = t + jnp.dot(pltpu.bitcast(vt_ref[words, ks], BF16), w_in[ks, cols],
                                    preferred_element_type=F32)
                    vpu_slice()
                acc_sc[rows, cols] = t
            for mh in range(M_SPLIT):
                rows = slice(mh * eb // M_SPLIT, (mh + 1) * eb // M_SPLIT)
                words = slice(mh * eb // (2 * M_SPLIT), (mh + 1) * eb // (2 * M_SPLIT))
                t = None
                for kt in range(d // MXU_COLS):
                    ks = slice(kt * MXU_COLS, (kt + 1) * MXU_COLS)
                    kw = slice(kt * MXU_COLS // 2, (kt + 1) * MXU_COLS // 2)
                    p = jnp.dot(pltpu.bitcast(u_ref[words, ks], BF16), pltpu.bitcast(ht_ref[kw, cols], BF16),
                                preferred_element_type=F32)
                    t = p if t is None else t + p
                    vpu_slice()
                a_out[rows, cols] = t

    @pl.when(k % 2 == 0)
    def _():
        stages(a0_sc, a1_sc, w1_sc, w0_sc)

    @pl.when(k % 2 == 1)
    def _():
        stages(a1_sc, a0_sc, w0_sc, w1_sc)

    @pl.when(jnp.logical_and(k >= 2, c_exp == n_exp_tiles - 1))
    def _():
        y = x_ref[...] + acc_sc[...].T
        if final_norm:
            y = _rms(y, fg_ref[...])
        o_ref[...] = y


def _peer_dense(x2, ht, re, cnt, e1, u_all, v_all, layer, final_g, final_norm, tb=1024):
    n, d = x2.shape
    n_exp = u_all.shape[1]
    eb = P_ROWS_PER_STEP * P_NKEYS
    assert eb == PACK_BLOCK, "V^T arrives tiled by the packing kernel's block"
    tb = min(tb, n)
    ne = n_exp // eb
    n_tiles = (n // tb) * ne
    last = n_tiles - 1

    def pair_a(k):
        return jnp.minimum(k, last)

    def pair_b(k):
        return jnp.clip(k - 1, 0, last)

    def pair_c(k):
        return jnp.clip(k - 2, 0, last)

    tiles_per_row_tile = PREP_SLABS * LANES // tb
    big = pl.BlockSpec((tb // LANES, P_NKEYS // BF16_ROWS, P_HEADS, BF16_ROWS, LANES),
                       lambda k: (pair_b(k) // ne, 0, 0, 0, 0))
    small = pl.BlockSpec((1, P_HEADS, P_ROWS_PER_STEP, PREP_SLABS, LANES),
                         lambda k: (pair_b(k) // ne // tiles_per_row_tile, 0, pair_b(k) % ne, 0, 0))
    return pl.pallas_call(
        functools.partial(_peer_dense_kernel, n_exp_tiles=ne, n_tiles=n_tiles, final_norm=final_norm),
        out_shape=jax.ShapeDtypeStruct((n, d), F32),
        grid=(n_tiles + 2,),
        in_specs=[
            pl.BlockSpec((d // 2, tb), lambda k: (0, pair_a(k) // ne)),
            big, small, small,
            pl.BlockSpec((eb // 2, d), lambda k: (pair_a(k) % ne, 0)),
            pl.BlockSpec((None, d // 2, eb), lambda k: (pair_c(k) % ne, 0, 0)),
            pl.BlockSpec((tb, d), lambda k: (pair_c(k) // ne, 0), pipeline_mode=pl.Buffered(1)),
            pl.BlockSpec((1, d), lambda k: (0, 0)),
        ],
        out_specs=pl.BlockSpec((tb, d), lambda k: (pair_c(k) // ne, 0)),
        scratch_shapes=[pltpu.VMEM((eb, tb), F32), pltpu.VMEM((eb, tb), F32),
                        pltpu.VMEM((eb, tb), BF16), pltpu.VMEM((eb, tb), BF16),
                        pltpu.VMEM((d, tb), F32)],
        compiler_params=pltpu.CompilerParams(
            dimension_semantics=("arbitrary",), vmem_limit_bytes=VMEM_LIMIT_BYTES),
        name="peer_dense",
    )(ht, re, cnt, e1, _pack_row_pairs(u_all, layer=layer),
      _pack_row_pairs(v_all, transpose=True, layer=layer, tiled=True), x2, final_g.reshape(1, d))


def _peer(x2, g, w_q, k1, k2, u_all, v_all, layer, final_g, final_norm):
    ht, re, cnt, e1 = _peer_prep(x2, g, w_q, k1, k2)
    return _peer_dense(x2, ht, re, cnt, e1, u_all, v_all, layer, final_g, final_norm)


def kernel(x, norm_mix, norm_ffn, final_norm, a_w_in, a_b_in, a_ln_g, a_ln_b, a_w_s, a_b_s, a_w_out,
           b_w_in, b_w_g2, b_b_g, b_norm, b_w_out, p_w_q, p_k1, p_k2, p_u, p_v):
    b, l, d = x.shape
    depth = norm_mix.shape[0]
    x2 = x.reshape(b * l, d)
    for i in range(depth):
        j = i // 2
        if i % 2 == 0:
            x2 = _mixer_a(x2, norm_mix[i], a_w_in[j], a_b_in[j], a_ln_g[j], a_ln_b[j],
                          a_w_s[j], a_b_s[j], a_w_out[j])
        else:
            x2 = _mixer_b(x2.reshape(b, l, d), norm_mix[i], b_w_in[j], b_w_g2[j], b_b_g[j],
                          b_norm[j], b_w_out[j]).reshape(b * l, d)
        x2 = _peer(x2, norm_ffn[i], p_w_q[i], p_k1[i], p_k2[i], p_u, p_v, i,
                   final_norm, i == depth - 1)
    return x2.reshape(b, l, d)
```

```python
import functools
import math

import jax
import jax.numpy as jnp
from jax import lax
from jax.experimental import pallas as pl
from jax.experimental.pallas import tpu as pltpu

F32 = jnp.float32
BF16 = jnp.bfloat16

NORM_EPS = 1e-6

LANES = 128
SUBLANES = 8
BF16_ROWS = 2 * SUBLANES
MXU_COLS = 256
PACK_BLOCK = 1024
M_SPLIT = 1
VMEM_LIMIT_BYTES = 56 * 1024 * 1024

A_GROUPS = 8
A_CHUNK = 128
B_HEADS = 4
B_GATE_RANK = 16
B_GATE_NORM = 16.0
B_CHUNK = 64
P_HEADS = 8
P_NKEYS = 128
P_TOPK = 16
P_ROWS_PER_STEP = 8
PREP_SLABS = SUBLANES


def _rms(x, g):
    return x * lax.rsqrt(jnp.mean(x * x, axis=-1, keepdims=True) + NORM_EPS) * g


def _gelu(x):
    return 0.5 * x * (1.0 + lax.erf(x * (1.0 / math.sqrt(2.0))))


def _split_bf16(x):
    hi = x.astype(BF16)
    lo = (x - hi.astype(F32)).astype(BF16)
    return hi, lo


def _pack_kernel(x_ref, o_ref, *, transpose):
    x = x_ref[...]
    if transpose:
        x = x.T
    o_ref[...] = pltpu.bitcast(x.astype(BF16), jnp.uint32)


def _pack_row_pairs(w, transpose=False, layer=None, tiled=False):
    rows, cols = w.shape[-2:]
    blk = min(PACK_BLOCK, rows)
    if layer is None:
        in_spec = pl.BlockSpec((blk, cols), lambda i: (i, 0))
    else:
        in_spec = pl.BlockSpec((None, blk, cols), lambda i: (layer, i, 0))
    if transpose and tiled:
        out_shape = jax.ShapeDtypeStruct((rows // blk, cols // 2, blk), jnp.uint32)
        out_spec = pl.BlockSpec((None, cols // 2, blk), lambda i: (i, 0, 0))
    elif transpose:
        out_shape = jax.ShapeDtypeStruct((cols // 2, rows), jnp.uint32)
        out_spec = pl.BlockSpec((cols // 2, blk), lambda i: (0, i))
    else:
        out_shape = jax.ShapeDtypeStruct((rows // 2, cols), jnp.uint32)
        out_spec = pl.BlockSpec((blk // 2, cols), lambda i: (i, 0))
    return pl.pallas_call(
        functools.partial(_pack_kernel, transpose=transpose),
        out_shape=out_shape,
        grid=(rows // blk,),
        in_specs=[in_spec],
        out_specs=out_spec,
        compiler_params=pltpu.CompilerParams(
            dimension_semantics=("arbitrary",), vmem_limit_bytes=VMEM_LIMIT_BYTES),
        name="pack_bf16_pairs",
    )(w)


def _const_spec(shape):
    nd = len(shape)
    return pl.BlockSpec(shape, lambda *_: (0,) * nd)


def _mixer_a_kernel(x_ref, g_ref, win_ref, bin_ref, lng_ref, lnb_ref, ws_ref, bs_ref, wout_ref,
                    o_ref, y_sc):
    tm = x_ref.shape[0]
    width = lng_ref.shape[1]
    gdim = width // A_GROUPS
    x = x_ref[...]
    h = _rms(x, g_ref[...]).astype(BF16)
    z = _gelu(jnp.dot(h, pltpu.bitcast(win_ref[...], BF16), preferred_element_type=F32) + bin_ref[...])
    u = z[:, :width]
    v = z[:, width:]
    mu = jnp.mean(v, axis=-1, keepdims=True)
    vc = v - mu
    var = jnp.mean(vc * vc, axis=-1, keepdims=True)
    vn = (vc * lax.rsqrt(var + NORM_EPS) * lng_ref[...] + lnb_ref[...]).astype(BF16)
    for c in range(tm // A_CHUNK):
        rows = slice(c * A_CHUNK, (c + 1) * A_CHUNK)
        for g in range(A_GROUPS):
            cols = slice(g * gdim, (g + 1) * gdim)
            sv = jnp.dot(ws_ref[g], vn[rows, cols], preferred_element_type=F32) + bs_ref[:, cols]
            y_sc[rows, cols] = (u[rows, cols] * sv).astype(BF16)
    o_ref[...] = x + jnp.dot(y_sc[...], pltpu.bitcast(wout_ref[...], BF16), preferred_element_type=F32)


def _mixer_a(x2, g, w_in, b_in, ln_g, ln_b, w_s, b_s, w_out, tm=256):
    n, d = x2.shape
    width = ln_g.shape[0]
    causal = jnp.tril(jnp.ones((A_CHUNK, A_CHUNK), dtype=bool))
    ws = jnp.where(causal[None], w_s, 0.0).astype(BF16)
    bs = jnp.repeat(b_s.T, width // A_GROUPS, axis=1)
    return pl.pallas_call(
        _mixer_a_kernel,
        out_shape=jax.ShapeDtypeStruct((n, d), F32),
        grid=(n // tm,),
        in_specs=[
            pl.BlockSpec((tm, d), lambda i: (i, 0)),
            _const_spec((1, d)),
            _const_spec((d // 2, 2 * width)),
            _const_spec((1, 2 * width)),
            _const_spec((1, width)),
            _const_spec((1, width)),
            _const_spec((A_GROUPS, A_CHUNK, A_CHUNK)),
            _const_spec((A_CHUNK, width)),
            _const_spec((width // 2, d)),
        ],
        out_specs=pl.BlockSpec((tm, d), lambda i: (i, 0)),
        scratch_shapes=[pltpu.VMEM((tm, width), BF16)],
        compiler_params=pltpu.CompilerParams(
            dimension_semantics=("arbitrary",), vmem_limit_bytes=VMEM_LIMIT_BYTES),
        name="mixer_a",
    )(x2, g.reshape(1, d), _pack_row_pairs(w_in), b_in.reshape(1, -1), ln_g.reshape(1, -1),
      ln_b.reshape(1, -1), ws, bs, _pack_row_pairs(w_out))


def _gla_kernel(x_ref, g_ref, win_ref, wg2_ref, bg_ref, ng_ref, wout_ref, tri_ref,
                o_ref, st_ref, oc_sc):
    nb, tc, d = x_ref.shape
    kdim = bg_ref.shape[1]
    vdim = ng_ref.shape[1]
    dk = kdim // B_HEADS
    dv = vdim // B_HEADS

    @pl.when(pl.program_id(0) == 0)
    def _():
        st_ref[...] = jnp.zeros_like(st_ref)

    x = x_ref[...].reshape(nb * tc, d)
    h = _rms(x, g_ref[...]).astype(BF16)
    proj = jnp.dot(h, pltpu.bitcast(win_ref[...], BF16), preferred_element_type=F32)
    q = proj[:, :kdim] * (dk ** -0.5)
    k = proj[:, kdim:2 * kdim]
    v = proj[:, 2 * kdim:2 * kdim + vdim]
    r = proj[:, 2 * kdim + vdim:2 * kdim + 2 * vdim]
    glr = proj[:, 2 * kdim + 2 * vdim:]

    g_hi, g_lo = _split_bf16(glr)
    w_hi = wg2_ref[0]
    w_lo = wg2_ref[1]
    xg = (jnp.dot(g_hi, w_hi, preferred_element_type=F32)
          + jnp.dot(g_hi, w_lo, preferred_element_type=F32)
          + jnp.dot(g_lo, w_hi, preferred_element_type=F32)) + bg_ref[...]
    log_a = (jnp.minimum(xg, 0.0) - jnp.log(1.0 + jnp.exp(-jnp.abs(xg)))) * (1.0 / B_GATE_NORM)

    a_hi, a_lo = _split_bf16(log_a)
    tri = tri_ref[...]
    gcum = jnp.concatenate(
        [jnp.dot(tri, a_hi[bi * tc:(bi + 1) * tc], preferred_element_type=F32)
         + jnp.dot(tri, a_lo[bi * tc:(bi + 1) * tc], preferred_element_type=F32)
         for bi in range(nb)], axis=0)

    eg = jnp.exp(gcum)
    qd = (q * eg).astype(BF16)
    kd = (k * jnp.exp(-gcum)).astype(BF16)
    vb = v.astype(BF16)
    row = lax.broadcasted_iota(jnp.int32, (B_CHUNK, B_CHUNK), 0)
    col = lax.broadcasted_iota(jnp.int32, (B_CHUNK, B_CHUNK), 1)
    causal = row >= col

    nt = (((1,), (1,)), ((), ()))
    tn = (((0,), (0,)), ((), ()))
    for c in range(tc // B_CHUNK):
        units = []
        for bi in range(nb):
            rows = slice(bi * tc + c * B_CHUNK, bi * tc + (c + 1) * B_CHUNK)
            last = bi * tc + (c + 1) * B_CHUNK - 1
            for hh in range(B_HEADS):
                units.append((bi, hh, rows, last, slice(hh * dk, (hh + 1) * dk), slice(hh * dv, (hh + 1) * dv)))
        scores = [lax.dot_general(qd[rows, kc], kd[rows, kc], nt, preferred_element_type=F32)
                  for _, _, rows, _, kc, _ in units]
        inter = [lax.dot_general(qd[rows, kc], st_ref[bi, hh].astype(BF16), nt, preferred_element_type=F32)
                 for bi, hh, rows, _, kc, _ in units]
        g_last = [gcum[last:last + 1, kc] for _, _, _, last, kc, _ in units]
        ks = [(k[rows, kc] * jnp.exp(gl - gcum[rows, kc])).astype(BF16)
              for (_, _, rows, _, kc, _), gl in zip(units, g_last)]
        upd = [lax.dot_general(vb[rows, vc], kk, tn, preferred_element_type=F32)
               for (_, _, rows, _, _, vc), kk in zip(units, ks)]
        for (bi, hh, rows, _, _, vc), sc, io in zip(units, scores, inter):
            sc = jnp.where(causal, sc, 0.0).astype(BF16)
            oc_sc[rows, vc] = jnp.dot(sc, vb[rows, vc], preferred_element_type=F32) + io
        for (bi, hh, _, _, _, _), gl, up in zip(units, g_last, upd):
            st_ref[bi, hh] = st_ref[bi, hh] * jnp.exp(gl) + up

    ng = ng_ref[...]
    for hh in range(B_HEADS):
        vc = slice(hh * dv, (hh + 1) * dv)
        oh = oc_sc[:, vc]
        oh = oh * lax.rsqrt(jnp.mean(oh * oh, axis=-1, keepdims=True) + NORM_EPS) * ng[:, vc]
        rh = r[:, vc]
        oc_sc[:, vc] = oh * (rh / (1.0 + jnp.exp(-rh)))
    y = x + jnp.dot(oc_sc[...].astype(BF16), pltpu.bitcast(wout_ref[...], BF16),
                    preferred_element_type=F32)
    o_ref[...] = y.reshape(nb, tc, d)


def _mixer_b(x3, g, w_in, w_g2, b_g, norm_g, w_out, tc=128):
    b, l, d = x3.shape
    kdim = w_g2.shape[1]
    vdim = w_out.shape[0]
    main = 2 * kdim + 2 * vdim
    w_in_p = _pack_row_pairs(jnp.concatenate(
        [w_in, jnp.zeros((d, LANES - B_GATE_RANK), w_in.dtype)], axis=1))
    w_g2_p = jnp.concatenate([w_g2, jnp.zeros((LANES - B_GATE_RANK, kdim), w_g2.dtype)], axis=0)
    hi = w_g2_p.astype(BF16)
    lo = (w_g2_p - hi.astype(F32)).astype(BF16)
    wg2 = jnp.stack([hi, lo])
    ng = jnp.tile(norm_g, B_HEADS).reshape(1, vdim)
    idx = jnp.arange(tc)
    tri = ((idx[:, None] >= idx[None, :]) &
           (idx[:, None] // B_CHUNK == idx[None, :] // B_CHUNK)).astype(BF16)
    return pl.pallas_call(
        _gla_kernel,
        out_shape=jax.ShapeDtypeStruct((b, l, d), F32),
        grid=(l // tc,),
        in_specs=[
            pl.BlockSpec((b, tc, d), lambda j: (0, j, 0)),
            _const_spec((1, d)),
            _const_spec((d // 2, main + LANES)),
            _const_spec((2, LANES, kdim)),
            _const_spec((1, kdim)),
            _const_spec((1, vdim)),
            _const_spec((vdim // 2, d)),
            _const_spec((tc, tc)),
        ],
        out_specs=pl.BlockSpec((b, tc, d), lambda j: (0, j, 0)),
        scratch_shapes=[pltpu.VMEM((b, B_HEADS, vdim // B_HEADS, kdim // B_HEADS), F32),
                        pltpu.VMEM((b * tc, vdim), F32)],
        compiler_params=pltpu.CompilerParams(
            dimension_semantics=("arbitrary",), vmem_limit_bytes=VMEM_LIMIT_BYTES),
        name="mixer_b_gla",
    )(x3, g.reshape(1, d), w_in_p, wg2, b_g.reshape(1, kdim), ng, _pack_row_pairs(w_out), tri)


def _sort16_pairs():
    pairs = []

    def merge(lo, n, r):
        step = 2 * r
        if step < n:
            merge(lo, n, step)
            merge(lo + r, n, step)
            pairs.extend((i, i + r) for i in range(lo + r, lo + n - r, step))
        else:
            pairs.append((lo, lo + r))

    def sort(lo, n):
        if n > 1:
            sort(lo, n // 2)
            sort(lo + n // 2, n // 2)
            merge(lo, n, 1)

    sort(0, P_TOPK)
    return pairs


def _sort16(vals):
    vals = list(vals)
    for i, j in _sort16_pairs():
        vals[i], vals[j] = jnp.maximum(vals[i], vals[j]), jnp.minimum(vals[i], vals[j])
    return vals


def _merge_top16(a, b):
    c = list(a)
    for m, bv in enumerate(b):
        i = P_TOPK - 1 - m
        c[i] = jnp.maximum(c[i], bv)
    dist = P_TOPK // 2
    while dist:
        for i in range(P_TOPK):
            if not i & dist:
                c[i], c[i + dist] = jnp.maximum(c[i], c[i + dist]), jnp.minimum(c[i], c[i + dist])
        dist //= 2
    return c


def _top16_of_keys(s_sc, lo, hi):
    if hi - lo == P_TOPK:
        return _sort16([s_sc[k * SUBLANES:(k + 1) * SUBLANES, :] for k in range(lo, hi)])
    mid = (lo + hi) // 2
    return _merge_top16(_top16_of_keys(s_sc, lo, mid), _top16_of_keys(s_sc, mid, hi))


def _count_true_prefix(pred, thr):
    steps = (8, 4, 2, 1)
    masks = []

    def pick(ms, cands):
        if not ms:
            return cands[0]
        half = len(cands) // 2
        return jnp.where(ms[0], pick(ms[1:], cands[half:]), pick(ms[1:], cands[:half]))

    for lvl, step in enumerate(steps):
        cands = []
        for combo in range(2 ** lvl):
            taken = sum(s for bit, s in enumerate(steps[:lvl]) if (combo >> (lvl - 1 - bit)) & 1)
            cands.append(thr[taken + step - 1])
        masks.append(pred(pick(masks, cands)))
    count = None
    for m, step in zip(masks, steps):
        term = jnp.where(m, float(step), 0.0)
        count = term if count is None else count + term
    return jnp.where(pred(thr[P_TOPK - 1]), float(P_TOPK), count)


def _bf16_bits(x):
    return pltpu.bitcast(x.astype(BF16).astype(F32), jnp.uint32)


def _bf16_pair_bits(x):
    bits = _bf16_bits(x)
    return bits | (bits >> 16)


def _to_key_major(s, s_sc):
    for g in range(s.shape[0] // SUBLANES):
        for c in range(PREP_SLABS):
            s_sc[pl.ds(g * SUBLANES * PREP_SLABS + c, SUBLANES, stride=PREP_SLABS), :] = (
                s[g * SUBLANES:(g + 1) * SUBLANES, c * LANES:(c + 1) * LANES])


def _peer_prep_kernel(x_ref, g_ref, wq_ref, k1_ref, k2_ref,
                      ht_ref, re_ref, c_ref, e1_ref,
                      ht_sc, qt_sc, s1_sc, s2_sc, rw_sc, ew_sc):
    hh = pl.program_id(1)
    half = 2 * k1_ref.shape[0]

    @pl.when(hh == 0)
    def _():
        h = _rms(x_ref[...], g_ref[...])
        ht = h.T.astype(BF16)
        ht_ref[...] = pltpu.bitcast(ht, jnp.uint32)
        ht_sc[...] = ht

    qt_sc[...] = jnp.dot(pltpu.bitcast(wq_ref[0], BF16), ht_sc[...],
                         preferred_element_type=F32).astype(BF16)
    s1 = jnp.dot(pltpu.bitcast(k1_ref[...], BF16), qt_sc[:half, :], preferred_element_type=F32)
    s2 = jnp.dot(pltpu.bitcast(k2_ref[...], BF16), qt_sc[half:, :], preferred_element_type=F32)
    _to_key_major(s1, s1_sc)
    _to_key_major(s2, s2_sc)

    v1 = _top16_of_keys(s1_sc, 0, P_NKEYS)
    v2 = _top16_of_keys(s2_sc, 0, P_NKEYS)

    top = [v1[0] + v2[b] for b in range(P_TOPK)]
    for a in range(1, P_TOPK // 2):
        top = _merge_top16(top, [v1[a] + v2[b] for b in range(P_TOPK // (a + 1))])
    top = _merge_top16(top, [v1[a] + v2[0] for a in range(P_TOPK // 2, P_TOPK)])
    tau = top[P_TOPK - 1]
    z = None
    for m in range(P_TOPK):
        term = jnp.exp(top[m] - top[0])
        z = term if z is None else z + term
    inv_z = 1.0 / z

    for p in range(P_NKEYS // 2):
        rank_bits = []
        e2_bits = []
        for k in (2 * p, 2 * p + 1):
            tile = slice(k * SUBLANES, (k + 1) * SUBLANES)
            x2 = s2_sc[tile, :]
            rank_bits.append(pltpu.bitcast(_count_true_prefix(lambda t, x2=x2: t > x2, v2), jnp.uint32))
            e2_bits.append(_bf16_bits(jnp.exp(x2 - v2[0])))
            x1 = s1_sc[tile, :]
            cnt = _count_true_prefix(lambda t, x1=x1: x1 + t >= tau, v2)
            cnt_bits = pltpu.bitcast(cnt, jnp.uint32)
            c_ref[0, 0, k] = cnt_bits | (cnt_bits >> 16)
            e1_ref[0, 0, k] = _bf16_pair_bits(jnp.exp(x1 - v1[0]) * inv_z)
        words = slice(p * SUBLANES, (p + 1) * SUBLANES)
        rw_sc[words, :] = (rank_bits[0] >> 16) | rank_bits[1]
        ew_sc[words, :] = (e2_bits[0] >> 16) | e2_bits[1]

    for jc in range(P_NKEYS // BF16_ROWS):
        for c in range(PREP_SLABS):
            rows = pl.ds(jc * SUBLANES * PREP_SLABS + c, SUBLANES, stride=PREP_SLABS)
            re_ref[c, jc, 0, 0:SUBLANES, :] = rw_sc[rows, :]
            re_ref[c, jc, 0, SUBLANES:BF16_ROWS, :] = ew_sc[rows, :]


def _peer_prep(x2, g, w_q, k1, k2):
    n, d = x2.shape
    tt = PREP_SLABS * LANES
    qdim = w_q.shape[1] // P_HEADS
    chunks = P_NKEYS // BF16_ROWS
    re_shape = jax.ShapeDtypeStruct((n // LANES, chunks, P_HEADS, BF16_ROWS, LANES), jnp.uint32)
    re_spec = pl.BlockSpec((PREP_SLABS, chunks, 1, BF16_ROWS, LANES), lambda t, h: (t, 0, h, 0, 0))
    row_shape = jax.ShapeDtypeStruct((n // tt, P_HEADS, P_NKEYS, PREP_SLABS, LANES), jnp.uint32)
    row_spec = pl.BlockSpec((1, 1, P_NKEYS, PREP_SLABS, LANES), lambda t, h: (t, h, 0, 0, 0))
    wq = _pack_row_pairs(w_q, transpose=True).reshape(P_HEADS, qdim // 2, d)
    return pl.pallas_call(
        _peer_prep_kernel,
        out_shape=(jax.ShapeDtypeStruct((d // 2, n), jnp.uint32), re_shape, row_shape, row_shape),
        grid=(n // tt, P_HEADS),
        in_specs=[
            pl.BlockSpec((tt, d), lambda t, h: (t, 0)),
            _const_spec((1, d)),
            pl.BlockSpec((1, qdim // 2, d), lambda t, h: (h, 0, 0)),
            _const_spec((P_NKEYS // 2, k1.shape[1])),
            _const_spec((P_NKEYS // 2, k2.shape[1])),
        ],
        out_specs=(pl.BlockSpec((d // 2, tt), lambda t, h: (0, t)), re_spec, row_spec, row_spec),
        scratch_shapes=[pltpu.VMEM((d, tt), BF16),
                        pltpu.VMEM((qdim, tt), BF16),
                        pltpu.VMEM((P_NKEYS * PREP_SLABS, LANES), F32),
                        pltpu.VMEM((P_NKEYS * PREP_SLABS, LANES), F32),
                        pltpu.VMEM((P_NKEYS // 2 * PREP_SLABS, LANES), jnp.uint32),
                        pltpu.VMEM((P_NKEYS // 2 * PREP_SLABS, LANES), jnp.uint32)],
        compiler_params=pltpu.CompilerParams(
            dimension_semantics=("arbitrary", "arbitrary"), vmem_limit_bytes=VMEM_LIMIT_BYTES),
        name="peer_prep",
    )(x2, g.reshape(1, d), wq, _pack_row_pairs(k1), _pack_row_pairs(k2))


def _gate_times_act(re_ref, c_ref, e1_ref, a_ref, w_ref, slab0, blocks):
    for key1_rows, lc in blocks:
        lanes = slice(lc * LANES, (lc + 1) * LANES)
        slab = pl.ds(slab0 + lc, 1)

        def row_tile(ref, hh, i):
            return pltpu.bitcast(jnp.broadcast_to(ref[0, hh, i, slab, :], (SUBLANES, LANES)), BF16)

        cnt = {(i, hh): row_tile(c_ref, hh, i) for i in key1_rows for hh in range(P_HEADS)}
        e1 = {(i, hh): row_tile(e1_ref, hh, i) for i in key1_rows for hh in range(P_HEADS)}
        for jc in range(P_NKEYS // BF16_ROWS):
            gate = {i: None for i in key1_rows}
            for hh in range(P_HEADS):
                rank2 = pltpu.bitcast(re_ref[lc, jc, hh, 0:SUBLANES, :], BF16)
                e2 = pltpu.bitcast(re_ref[lc, jc, hh, SUBLANES:BF16_ROWS, :], BF16)
                for i in key1_rows:
                    term = jnp.where(rank2 < cnt[i, hh], e2, 0.0) * e1[i, hh]
                    gate[i] = term if gate[i] is None else gate[i] + term
            for i in key1_rows:
                rows = slice(i * P_NKEYS + jc * BF16_ROWS, i * P_NKEYS + (jc + 1) * BF16_ROWS)
                w_ref[rows, lanes] = gate[i] * _gelu(a_ref[rows, lanes]).astype(BF16)


def _peer_dense_kernel(ht_ref, re_ref, c_ref, e1_ref, u_ref, vt_ref, x_ref, fg_ref,
                       o_ref, a0_sc, a1_sc, w0_sc, w1_sc, acc_sc, *, n_exp_tiles, n_tiles, final_norm):
    k = pl.program_id(0)
    c_pair = jnp.clip(k - 2, 0, n_tiles - 1)
    c_exp = c_pair % n_exp_tiles

    @pl.when(k == 0)
    def _():
        a1_sc[...] = jnp.zeros_like(a1_sc)
        w0_sc[...] = jnp.zeros_like(w0_sc)

    @pl.when(c_exp == 0)
    def _():
        acc_sc[...] = jnp.zeros_like(acc_sc)

    def stages(a_out, a_in, w_out, w_in):
        tb = ht_ref.shape[1]
        eb = a_in.shape[0]
        d = acc_sc.shape[0]
        n_chunks = tb // MXU_COLS
        blocks = [((i, i + 1), lc) for i in range(0, P_ROWS_PER_STEP, 2) for lc in range(tb // LANES)]
        n_units = n_chunks * M_SPLIT * (eb // MXU_COLS + d // MXU_COLS)
        units_per_block = n_units // len(blocks)
        todo = iter(range(n_units))
        tiles_per_row_tile = PREP_SLABS * LANES // tb
        if tiles_per_row_tile == 1:
            slab0 = 0
        else:
            b_tok = jnp.clip(k - 1, 0, n_tiles - 1) // n_exp_tiles
            slab0 = (b_tok % tiles_per_row_tile) * (tb // LANES)

        def vpu_slice():
            unit = next(todo)
            if unit % units_per_block == 0:
                b0 = unit // units_per_block
                _gate_times_act(re_ref, c_ref, e1_ref, a_in, w_out, slab0, blocks[b0:b0 + 1])

        for n in range(n_chunks):
            cols = slice(n * MXU_COLS, (n + 1) * MXU_COLS)
            for mh in range(M_SPLIT):
                rows = slice(mh * d // M_SPLIT, (mh + 1) * d // M_SPLIT)
                words = slice(mh * d // (2 * M_SPLIT), (mh + 1) * d // (2 * M_SPLIT))
                t = acc_sc[rows, cols]
                for kt in range(eb // MXU_COLS):
                    ks = slice(kt * MXU_COLS, (kt + 1) * MXU_COLS)
                    t = t + jnp.dot(pltpu.bitcast(vt_ref[words, ks], BF16), w_in[ks, cols],
                                    preferred_element_type=F32)
                    vpu_slice()
                acc_sc[rows, cols] = t
            for mh in range(M_SPLIT):
                rows = slice(mh * eb // M_SPLIT, (mh + 1) * eb // M_SPLIT)
                words = slice(mh * eb // (2 * M_SPLIT), (mh + 1) * eb // (2 * M_SPLIT))
                t = None
                for kt in range(d // MXU_COLS):
                    ks = slice(kt * MXU_COLS, (kt + 1) * MXU_COLS)
                    kw = slice(kt * MXU_COLS // 2, (kt + 1) * MXU_COLS // 2)
                    p = jnp.dot(pltpu.bitcast(u_ref[words, ks], BF16), pltpu.bitcast(ht_ref[kw, cols], BF16),
                                preferred_element_type=F32)
                    t = p if t is None else t + p
                    vpu_slice()
                a_out[rows, cols] = t

    @pl.when(k % 2 == 0)
    def _():
        stages(a0_sc, a1_sc, w1_sc, w0_sc)

    @pl.when(k % 2 == 1)
    def _():
        stages(a1_sc, a0_sc, w0_sc, w1_sc)

    @pl.when(jnp.logical_and(k >= 2, c_exp == n_exp_tiles - 1))
    def _():
        y = x_ref[...] + acc_sc[...].T
        if final_norm:
            y = _rms(y, fg_ref[...])
        o_ref[...] = y


def _peer_dense(x2, ht, re, cnt, e1, u_all, v_all, layer, final_g, final_norm, tb=1024):
    n, d = x2.shape
    n_exp = u_all.shape[1]
    eb = P_ROWS_PER_STEP * P_NKEYS
    assert eb == PACK_BLOCK, "V^T arrives tiled by the packing kernel's block"
    tb = min(tb, n)
    ne = n_exp // eb
    n_tiles = (n // tb) * ne
    last = n_tiles - 1

    def pair_a(k):
        return jnp.minimum(k, last)

    def pair_b(k):
        return jnp.clip(k - 1, 0, last)

    def pair_c(k):
        return jnp.clip(k - 2, 0, last)

    tiles_per_row_tile = PREP_SLABS * LANES // tb
    big = pl.BlockSpec((tb // LANES, P_NKEYS // BF16_ROWS, P_HEADS, BF16_ROWS, LANES),
                       lambda k: (pair_b(k) // ne, 0, 0, 0, 0))
    small = pl.BlockSpec((1, P_HEADS, P_ROWS_PER_STEP, PREP_SLABS, LANES),
                         lambda k: (pair_b(k) // ne // tiles_per_row_tile, 0, pair_b(k) % ne, 0, 0))
    return pl.pallas_call(
        functools.partial(_peer_dense_kernel, n_exp_tiles=ne, n_tiles=n_tiles, final_norm=final_norm),
        out_shape=jax.ShapeDtypeStruct((n, d), F32),
        grid=(n_tiles + 2,),
        in_specs=[
            pl.BlockSpec((d // 2, tb), lambda k: (0, pair_a(k) // ne)),
            big, small, small,
            pl.BlockSpec((eb // 2, d), lambda k: (pair_a(k) % ne, 0)),
            pl.BlockSpec((None, d // 2, eb), lambda k: (pair_c(k) % ne, 0, 0)),
            pl.BlockSpec((tb, d), lambda k: (pair_c(k) // ne, 0), pipeline_mode=pl.Buffered(1)),
            pl.BlockSpec((1, d), lambda k: (0, 0)),
        ],
        out_specs=pl.BlockSpec((tb, d), lambda k: (pair_c(k) // ne, 0)),
        scratch_shapes=[pltpu.VMEM((eb, tb), F32), pltpu.VMEM((eb, tb), F32),
                        pltpu.VMEM((eb, tb), BF16), pltpu.VMEM((eb, tb), BF16),
                        pltpu.VMEM((d, tb), F32)],
        compiler_params=pltpu.CompilerParams(
            dimension_semantics=("arbitrary",), vmem_limit_bytes=VMEM_LIMIT_BYTES),
        name="peer_dense",
    )(ht, re, cnt, e1, _pack_row_pairs(u_all, layer=layer),
      _pack_row_pairs(v_all, transpose=True, layer=layer, tiled=True), x2, final_g.reshape(1, d))


def _peer(x2, g, w_q, k1, k2, u_all, v_all, layer, final_g, final_norm):
    ht, re, cnt, e1 = _peer_prep(x2, g, w_q, k1, k2)
    return _peer_dense(x2, ht, re, cnt, e1, u_all, v_all, layer, final_g, final_norm)


def kernel(x, norm_mix, norm_ffn, final_norm, a_w_in, a_b_in, a_ln_g, a_ln_b, a_w_s, a_b_s, a_w_out,
           b_w_in, b_w_g2, b_b_g, b_norm, b_w_out, p_w_q, p_k1, p_k2, p_u, p_v):
    b, l, d = x.shape
    depth = norm_mix.shape[0]
    x2 = x.reshape(b * l, d)
    for i in range(depth):
        j = i // 2
        if i % 2 == 0:
            x2 = _mixer_a(x2, norm_mix[i], a_w_in[j], a_b_in[j], a_ln_g[j], a_ln_b[j],
                          a_w_s[j], a_b_s[j], a_w_out[j])
        else:
            x2 = _mixer_b(x2.reshape(b, l, d), norm_mix[i], b_w_in[j], b_w_g2[j], b_b_g[j],
                          b_norm[j], b_w_out[j]).reshape(b * l, d)
        x2 = _peer(x2, norm_ffn[i], p_w_q[i], p_k1[i], p_k2[i], p_u, p_v, i,
                   final_norm, i == depth - 1)
    return x2.reshape(b, l, d)
```

```python
import functools
import math

import jax
import jax.numpy as jnp
from jax import lax
from jax.experimental import pallas as pl
from jax.experimental.pallas import tpu as pltpu

F32 = jnp.float32
BF16 = jnp.bfloat16

NORM_EPS = 1e-6

LANES = 128
SUBLANES = 8
BF16_ROWS = 2 * SUBLANES
MXU_COLS = 256
PACK_BLOCK = 1024
M_SPLIT = 1
VMEM_LIMIT_BYTES = 56 * 1024 * 1024

A_GROUPS = 8
A_CHUNK = 128
B_HEADS = 4
B_GATE_RANK = 16
B_GATE_NORM = 16.0
B_CHUNK = 64
B_SUB = 16
NEG_EXPONENT = -1e30
P_HEADS = 8
P_NKEYS = 128
P_TOPK = 16
P_ROWS_PER_STEP = 8
PREP_SLABS = SUBLANES


def _rms(x, g):
    return x * lax.rsqrt(jnp.mean(x * x, axis=-1, keepdims=True) + NORM_EPS) * g


def _gelu(x):
    return 0.5 * x * (1.0 + lax.erf(x * (1.0 / math.sqrt(2.0))))


def _split_bf16(x):
    hi = x.astype(BF16)
    lo = (x - hi.astype(F32)).astype(BF16)
    return hi, lo


def _pack_kernel(x_ref, o_ref, *, transpose):
    x = x_ref[...]
    if transpose:
        x = x.T
    o_ref[...] = pltpu.bitcast(x.astype(BF16), jnp.uint32)


def _pack_row_pairs(w, transpose=False, layer=None, tiled=False):
    rows, cols = w.shape[-2:]
    blk = min(PACK_BLOCK, rows)
    if layer is None:
        in_spec = pl.BlockSpec((blk, cols), lambda i: (i, 0))
    else:
        in_spec = pl.BlockSpec((None, blk, cols), lambda i: (layer, i, 0))
    if transpose and tiled:
        out_shape = jax.ShapeDtypeStruct((rows // blk, cols // 2, blk), jnp.uint32)
        out_spec = pl.BlockSpec((None, cols // 2, blk), lambda i: (i, 0, 0))
    elif transpose:
        out_shape = jax.ShapeDtypeStruct((cols // 2, rows), jnp.uint32)
        out_spec = pl.BlockSpec((cols // 2, blk), lambda i: (0, i))
    else:
        out_shape = jax.ShapeDtypeStruct((rows // 2, cols), jnp.uint32)
        out_spec = pl.BlockSpec((blk // 2, cols), lambda i: (i, 0))
    return pl.pallas_call(
        functools.partial(_pack_kernel, transpose=transpose),
        out_shape=out_shape,
        grid=(rows // blk,),
        in_specs=[in_spec],
        out_specs=out_spec,
        compiler_params=pltpu.CompilerParams(
            dimension_semantics=("arbitrary",), vmem_limit_bytes=VMEM_LIMIT_BYTES),
        name="pack_bf16_pairs",
    )(w)


def _const_spec(shape):
    nd = len(shape)
    return pl.BlockSpec(shape, lambda *_: (0,) * nd)


def _mixer_a_kernel(x_ref, g_ref, win_ref, bin_ref, lng_ref, lnb_ref, ws_ref, bs_ref, wout_ref,
                    o_ref, y_sc):
    tm = x_ref.shape[0]
    width = lng_ref.shape[1]
    gdim = width // A_GROUPS
    x = x_ref[...]
    h = _rms(x, g_ref[...]).astype(BF16)
    z = _gelu(jnp.dot(h, pltpu.bitcast(win_ref[...], BF16), preferred_element_type=F32) + bin_ref[...])
    u = z[:, :width]
    v = z[:, width:]
    mu = jnp.mean(v, axis=-1, keepdims=True)
    vc = v - mu
    var = jnp.mean(vc * vc, axis=-1, keepdims=True)
    vn = (vc * lax.rsqrt(var + NORM_EPS) * lng_ref[...] + lnb_ref[...]).astype(BF16)
    for c in range(tm // A_CHUNK):
        rows = slice(c * A_CHUNK, (c + 1) * A_CHUNK)
        for g in range(A_GROUPS):
            cols = slice(g * gdim, (g + 1) * gdim)
            sv = jnp.dot(ws_ref[g], vn[rows, cols], preferred_element_type=F32) + bs_ref[:, cols]
            y_sc[rows, cols] = (u[rows, cols] * sv).astype(BF16)
    o_ref[...] = x + jnp.dot(y_sc[...], pltpu.bitcast(wout_ref[...], BF16), preferred_element_type=F32)


def _mixer_a(x2, g, w_in, b_in, ln_g, ln_b, w_s, b_s, w_out, tm=256):
    n, d = x2.shape
    width = ln_g.shape[0]
    causal = jnp.tril(jnp.ones((A_CHUNK, A_CHUNK), dtype=bool))
    ws = jnp.where(causal[None], w_s, 0.0).astype(BF16)
    bs = jnp.repeat(b_s.T, width // A_GROUPS, axis=1)
    return pl.pallas_call(
        _mixer_a_kernel,
        out_shape=jax.ShapeDtypeStruct((n, d), F32),
        grid=(n // tm,),
        in_specs=[
            pl.BlockSpec((tm, d), lambda i: (i, 0)),
            _const_spec((1, d)),
            _const_spec((d // 2, 2 * width)),
            _const_spec((1, 2 * width)),
            _const_spec((1, width)),
            _const_spec((1, width)),
            _const_spec((A_GROUPS, A_CHUNK, A_CHUNK)),
            _const_spec((A_CHUNK, width)),
            _const_spec((width // 2, d)),
        ],
        out_specs=pl.BlockSpec((tm, d), lambda i: (i, 0)),
        scratch_shapes=[pltpu.VMEM((tm, width), BF16)],
        compiler_params=pltpu.CompilerParams(
            dimension_semantics=("arbitrary",), vmem_limit_bytes=VMEM_LIMIT_BYTES),
        name="mixer_a",
    )(x2, g.reshape(1, d), _pack_row_pairs(w_in), b_in.reshape(1, -1), ln_g.reshape(1, -1),
      ln_b.reshape(1, -1), ws, bs, _pack_row_pairs(w_out))


def _gla_kernel(x_ref, g_ref, win_ref, wg2_ref, bg_ref, ng_ref, wout_ref, tri_ref,
                o_ref, st_ref, oc_sc):
    nb, tc, d = x_ref.shape
    kdim = bg_ref.shape[1]
    vdim = ng_ref.shape[1]
    dk = kdim // B_HEADS
    dv = vdim // B_HEADS

    @pl.when(pl.program_id(0) == 0)
    def _():
        st_ref[...] = jnp.zeros_like(st_ref)

    x = x_ref[...].reshape(nb * tc, d)
    h = _rms(x, g_ref[...]).astype(BF16)
    proj = jnp.dot(h, pltpu.bitcast(win_ref[...], BF16), preferred_element_type=F32)
    q = proj[:, :kdim] * (dk ** -0.5)
    k = proj[:, kdim:2 * kdim]
    v = proj[:, 2 * kdim:2 * kdim + vdim]
    r = proj[:, 2 * kdim + vdim:2 * kdim + 2 * vdim]
    glr = proj[:, 2 * kdim + 2 * vdim:]

    g_hi, g_lo = _split_bf16(glr)
    w_hi = wg2_ref[0]
    w_lo = wg2_ref[1]
    xg = (jnp.dot(g_hi, w_hi, preferred_element_type=F32)
          + jnp.dot(g_hi, w_lo, preferred_element_type=F32)
          + jnp.dot(g_lo, w_hi, preferred_element_type=F32)) + bg_ref[...]
    log_a = (jnp.minimum(xg, 0.0) - jnp.log(1.0 + jnp.exp(-jnp.abs(xg)))) * (1.0 / B_GATE_NORM)

    a_hi, a_lo = _split_bf16(log_a)
    tri = tri_ref[...]
    gcum = jnp.concatenate(
        [jnp.dot(tri, a_hi[bi * tc:(bi + 1) * tc], preferred_element_type=F32)
         + jnp.dot(tri, a_lo[bi * tc:(bi + 1) * tc], preferred_element_type=F32)
         for bi in range(nb)], axis=0)

    qd = (q * jnp.exp(gcum)).astype(BF16)
    vb = v.astype(BF16)
    row = lax.broadcasted_iota(jnp.int32, (B_CHUNK, B_CHUNK), 0)
    col = lax.broadcasted_iota(jnp.int32, (B_CHUNK, B_CHUNK), 1)
    same_sub = (row // B_SUB) == (col // B_SUB)
    causal = row >= col
    rid = lax.broadcasted_iota(jnp.int32, (B_CHUNK, kdim), 0)
    n_sub = B_CHUNK // B_SUB

    nt = (((1,), (1,)), ((), ()))
    tn = (((0,), (0,)), ((), ()))
    for c in range(tc // B_CHUNK):
        q_hat, k_diag, q_off, k_off = {}, {}, {}, {}
        for bi in range(nb):
            rows = slice(bi * tc + c * B_CHUNK, bi * tc + (c + 1) * B_CHUNK)
            g_c = gcum[rows]
            k_c = k[rows]
            gb = [g_c[B_SUB * i - 1:B_SUB * i, :] for i in range(1, n_sub)]
            gb_rows = jnp.concatenate([jnp.zeros((B_SUB, kdim), F32)]
                                      + [jnp.broadcast_to(b, (B_SUB, kdim)) for b in gb], axis=0)
            qh = q[rows] * jnp.exp(g_c - gb_rows)
            q_hat[bi] = qh.astype(BF16)
            k_diag[bi] = (k_c * jnp.exp(gb_rows - g_c)).astype(BF16)
            q_off[bi] = [jnp.where(rid // B_SUB == i, qh, 0.0).astype(BF16) for i in range(1, n_sub)]
            k_off[bi] = [(k_c * jnp.exp(jnp.where(rid < B_SUB * i, b - g_c, NEG_EXPONENT))).astype(BF16)
                         for i, b in zip(range(1, n_sub), gb)]

        units = []
        for bi in range(nb):
            rows = slice(bi * tc + c * B_CHUNK, bi * tc + (c + 1) * B_CHUNK)
            last = bi * tc + (c + 1) * B_CHUNK - 1
            for hh in range(B_HEADS):
                units.append((bi, hh, rows, last, slice(hh * dk, (hh + 1) * dk), slice(hh * dv, (hh + 1) * dv)))
        s_off = [lax.dot_general(jnp.concatenate([qo[:, kc] for qo in q_off[bi]], axis=1),
                                 jnp.concatenate([ko[:, kc] for ko in k_off[bi]], axis=1),
                                 nt, preferred_element_type=F32)
                 for bi, _, _, _, kc, _ in units]
        s_diag = [lax.dot_general(q_hat[bi][:, kc], k_diag[bi][:, kc], nt, preferred_element_type=F32)
                  for bi, _, _, _, kc, _ in units]
        scores = [so + jnp.where(same_sub, jnp.where(causal, sd, 0.0), 0.0) for so, sd in zip(s_off, s_diag)]
        inter = [lax.dot_general(qd[rows, kc], st_ref[bi, hh].astype(BF16), nt, preferred_element_type=F32)
                 for bi, hh, rows, _, kc, _ in units]
        g_last = [gcum[last:last + 1, kc] for _, _, _, last, kc, _ in units]
        ks = [(k[rows, kc] * jnp.exp(gl - gcum[rows, kc])).astype(BF16)
              for (_, _, rows, _, kc, _), gl in zip(units, g_last)]
        upd = [lax.dot_general(vb[rows, vc], kk, tn, preferred_element_type=F32)
               for (_, _, rows, _, _, vc), kk in zip(units, ks)]
        for (bi, hh, rows, _, _, vc), sc, io in zip(units, scores, inter):
            oc_sc[rows, vc] = jnp.dot(sc.astype(BF16), vb[rows, vc], preferred_element_type=F32) + io
        for (bi, hh, _, _, _, _), gl, up in zip(units, g_last, upd):
            st_ref[bi, hh] = st_ref[bi, hh] * jnp.exp(gl) + up

    ng = ng_ref[...]
    for hh in range(B_HEADS):
        vc = slice(hh * dv, (hh + 1) * dv)
        oh = oc_sc[:, vc]
        oh = oh * lax.rsqrt(jnp.mean(oh * oh, axis=-1, keepdims=True) + NORM_EPS) * ng[:, vc]
        rh = r[:, vc]
        oc_sc[:, vc] = oh * (rh / (1.0 + jnp.exp(-rh)))
    y = x + jnp.dot(oc_sc[...].astype(BF16), pltpu.bitcast(wout_ref[...], BF16),
                    preferred_element_type=F32)
    o_ref[...] = y.reshape(nb, tc, d)


def _mixer_b(x3, g, w_in, w_g2, b_g, norm_g, w_out, tc=128):
    b, l, d = x3.shape
    kdim = w_g2.shape[1]
    vdim = w_out.shape[0]
    main = 2 * kdim + 2 * vdim
    w_in_p = _pack_row_pairs(jnp.concatenate(
        [w_in, jnp.zeros((d, LANES - B_GATE_RANK), w_in.dtype)], axis=1))
    w_g2_p = jnp.concatenate([w_g2, jnp.zeros((LANES - B_GATE_RANK, kdim), w_g2.dtype)], axis=0)
    hi = w_g2_p.astype(BF16)
    lo = (w_g2_p - hi.astype(F32)).astype(BF16)
    wg2 = jnp.stack([hi, lo])
    ng = jnp.tile(norm_g, B_HEADS).reshape(1, vdim)
    idx = jnp.arange(tc)
    tri = ((idx[:, None] >= idx[None, :]) &
           (idx[:, None] // B_CHUNK == idx[None, :] // B_CHUNK)).astype(BF16)
    return pl.pallas_call(
        _gla_kernel,
        out_shape=jax.ShapeDtypeStruct((b, l, d), F32),
        grid=(l // tc,),
        in_specs=[
            pl.BlockSpec((b, tc, d), lambda j: (0, j, 0)),
            _const_spec((1, d)),
            _const_spec((d // 2, main + LANES)),
            _const_spec((2, LANES, kdim)),
            _const_spec((1, kdim)),
            _const_spec((1, vdim)),
            _const_spec((vdim // 2, d)),
            _const_spec((tc, tc)),
        ],
        out_specs=pl.BlockSpec((b, tc, d), lambda j: (0, j, 0)),
        scratch_shapes=[pltpu.VMEM((b, B_HEADS, vdim // B_HEADS, kdim // B_HEADS), F32),
                        pltpu.VMEM((b * tc, vdim), F32)],
        compiler_params=pltpu.CompilerParams(
            dimension_semantics=("arbitrary",), vmem_limit_bytes=VMEM_LIMIT_BYTES),
        name="mixer_b_gla",
    )(x3, g.reshape(1, d), w_in_p, wg2, b_g.reshape(1, kdim), ng, _pack_row_pairs(w_out), tri)


def _sort16_pairs():
    pairs = []

    def merge(lo, n, r):
        step = 2 * r
        if step < n:
            merge(lo, n, step)
            merge(lo + r, n, step)
            pairs.extend((i, i + r) for i in range(lo + r, lo + n - r, step))
        else:
            pairs.append((lo, lo + r))

    def sort(lo, n):
        if n > 1:
            sort(lo, n // 2)
            sort(lo + n // 2, n // 2)
            merge(lo, n, 1)

    sort(0, P_TOPK)
    return pairs


def _sort16(vals):
    vals = list(vals)
    for i, j in _sort16_pairs():
        vals[i], vals[j] = jnp.maximum(vals[i], vals[j]), jnp.minimum(vals[i], vals[j])
    return vals


def _merge_top16(a, b):
    c = list(a)
    for m, bv in enumerate(b):
        i = P_TOPK - 1 - m
        c[i] = jnp.maximum(c[i], bv)
    dist = P_TOPK // 2
    while dist:
        for i in range(P_TOPK):
            if not i & dist:
                c[i], c[i + dist] = jnp.maximum(c[i], c[i + dist]), jnp.minimum(c[i], c[i + dist])
        dist //= 2
    return c


def _top16_of_keys(s_sc, lo, hi):
    if hi - lo == P_TOPK:
        return _sort16([s_sc[k * SUBLANES:(k + 1) * SUBLANES, :] for k in range(lo, hi)])
    mid = (lo + hi) // 2
    return _merge_top16(_top16_of_keys(s_sc, lo, mid), _top16_of_keys(s_sc, mid, hi))


def _count_true_prefix(pred, thr):
    steps = (8, 4, 2, 1)
    masks = []

    def pick(ms, cands):
        if not ms:
            return cands[0]
        half = len(cands) // 2
        return jnp.where(ms[0], pick(ms[1:], cands[half:]), pick(ms[1:], cands[:half]))

    for lvl, step in enumerate(steps):
        cands = []
        for combo in range(2 ** lvl):
            taken = sum(s for bit, s in enumerate(steps[:lvl]) if (combo >> (lvl - 1 - bit)) & 1)
            cands.append(thr[taken + step - 1])
        masks.append(pred(pick(masks, cands)))
    count = None
    for m, step in zip(masks, steps):
        term = jnp.where(m, float(step), 0.0)
        count = term if count is None else count + term
    return jnp.where(pred(thr[P_TOPK - 1]), float(P_TOPK), count)


def _bf16_bits(x):
    return pltpu.bitcast(x.astype(BF16).astype(F32), jnp.uint32)


def _bf16_pair_bits(x):
    bits = _bf16_bits(x)
    return bits | (bits >> 16)


def _to_key_major(s, s_sc):
    for g in range(s.shape[0] // SUBLANES):
        for c in range(PREP_SLABS):
            s_sc[pl.ds(g * SUBLANES * PREP_SLABS + c, SUBLANES, stride=PREP_SLABS), :] = (
                s[g * SUBLANES:(g + 1) * SUBLANES, c * LANES:(c + 1) * LANES])


def _peer_prep_kernel(x_ref, g_ref, wq_ref, k1_ref, k2_ref,
                      ht_ref, re_ref, c_ref, e1_ref,
                      ht_sc, qt_sc, s1_sc, s2_sc, rw_sc, ew_sc):
    hh = pl.program_id(1)
    half = 2 * k1_ref.shape[0]

    @pl.when(hh == 0)
    def _():
        h = _rms(x_ref[...], g_ref[...])
        ht = h.T.astype(BF16)
        ht_ref[...] = pltpu.bitcast(ht, jnp.uint32)
        ht_sc[...] = ht

    qt_sc[...] = jnp.dot(pltpu.bitcast(wq_ref[0], BF16), ht_sc[...],
                         preferred_element_type=F32).astype(BF16)
    s1 = jnp.dot(pltpu.bitcast(k1_ref[...], BF16), qt_sc[:half, :], preferred_element_type=F32)
    s2 = jnp.dot(pltpu.bitcast(k2_ref[...], BF16), qt_sc[half:, :], preferred_element_type=F32)
    _to_key_major(s1, s1_sc)
    _to_key_major(s2, s2_sc)

    v1 = _top16_of_keys(s1_sc, 0, P_NKEYS)
    v2 = _top16_of_keys(s2_sc, 0, P_NKEYS)

    top = [v1[0] + v2[b] for b in range(P_TOPK)]
    for a in range(1, P_TOPK // 2):
        top = _merge_top16(top, [v1[a] + v2[b] for b in range(P_TOPK // (a + 1))])
    top = _merge_top16(top, [v1[a] + v2[0] for a in range(P_TOPK // 2, P_TOPK)])
    tau = top[P_TOPK - 1]
    z = None
    for m in range(P_TOPK):
        term = jnp.exp(top[m] - top[0])
        z = term if z is None else z + term
    inv_z = 1.0 / z

    for p in range(P_NKEYS // 2):
        rank_bits = []
        e2_bits = []
        for k in (2 * p, 2 * p + 1):
            tile = slice(k * SUBLANES, (k + 1) * SUBLANES)
            x2 = s2_sc[tile, :]
            rank_bits.append(pltpu.bitcast(_count_true_prefix(lambda t, x2=x2: t > x2, v2), jnp.uint32))
            e2_bits.append(_bf16_bits(jnp.exp(x2 - v2[0])))
            x1 = s1_sc[tile, :]
            cnt = _count_true_prefix(lambda t, x1=x1: x1 + t >= tau, v2)
            cnt_bits = pltpu.bitcast(cnt, jnp.uint32)
            c_ref[0, 0, k] = cnt_bits | (cnt_bits >> 16)
            e1_ref[0, 0, k] = _bf16_pair_bits(jnp.exp(x1 - v1[0]) * inv_z)
        words = slice(p * SUBLANES, (p + 1) * SUBLANES)
        rw_sc[words, :] = (rank_bits[0] >> 16) | rank_bits[1]
        ew_sc[words, :] = (e2_bits[0] >> 16) | e2_bits[1]

    for jc in range(P_NKEYS // BF16_ROWS):
        for c in range(PREP_SLABS):
            rows = pl.ds(jc * SUBLANES * PREP_SLABS + c, SUBLANES, stride=PREP_SLABS)
            re_ref[c, jc, 0, 0:SUBLANES, :] = rw_sc[rows, :]
            re_ref[c, jc, 0, SUBLANES:BF16_ROWS, :] = ew_sc[rows, :]


def _peer_prep(x2, g, w_q, k1, k2):
    n, d = x2.shape
    tt = PREP_SLABS * LANES
    qdim = w_q.shape[1] // P_HEADS
    chunks = P_NKEYS // BF16_ROWS
    re_shape = jax.ShapeDtypeStruct((n // LANES, chunks, P_HEADS, BF16_ROWS, LANES), jnp.uint32)
    re_spec = pl.BlockSpec((PREP_SLABS, chunks, 1, BF16_ROWS, LANES), lambda t, h: (t, 0, h, 0, 0))
    row_shape = jax.ShapeDtypeStruct((n // tt, P_HEADS, P_NKEYS, PREP_SLABS, LANES), jnp.uint32)
    row_spec = pl.BlockSpec((1, 1, P_NKEYS, PREP_SLABS, LANES), lambda t, h: (t, h, 0, 0, 0))
    wq = _pack_row_pairs(w_q, transpose=True).reshape(P_HEADS, qdim // 2, d)
    return pl.pallas_call(
        _peer_prep_kernel,
        out_shape=(jax.ShapeDtypeStruct((d // 2, n), jnp.uint32), re_shape, row_shape, row_shape),
        grid=(n // tt, P_HEADS),
        in_specs=[
            pl.BlockSpec((tt, d), lambda t, h: (t, 0)),
            _const_spec((1, d)),
            pl.BlockSpec((1, qdim // 2, d), lambda t, h: (h, 0, 0)),
            _const_spec((P_NKEYS // 2, k1.shape[1])),
            _const_spec((P_NKEYS // 2, k2.shape[1])),
        ],
        out_specs=(pl.BlockSpec((d // 2, tt), lambda t, h: (0, t)), re_spec, row_spec, row_spec),
        scratch_shapes=[pltpu.VMEM((d, tt), BF16),
                        pltpu.VMEM((qdim, tt), BF16),
                        pltpu.VMEM((P_NKEYS * PREP_SLABS, LANES), F32),
                        pltpu.VMEM((P_NKEYS * PREP_SLABS, LANES), F32),
                        pltpu.VMEM((P_NKEYS // 2 * PREP_SLABS, LANES), jnp.uint32),
                        pltpu.VMEM((P_NKEYS // 2 * PREP_SLABS, LANES), jnp.uint32)],
        compiler_params=pltpu.CompilerParams(
            dimension_semantics=("arbitrary", "arbitrary"), vmem_limit_bytes=VMEM_LIMIT_BYTES),
        name="peer_prep",
    )(x2, g.reshape(1, d), wq, _pack_row_pairs(k1), _pack_row_pairs(k2))


def _gate_times_act(re_ref, c_ref, e1_ref, a_ref, w_ref, slab0, blocks):
    for key1_rows, lc in blocks:
        lanes = slice(lc * LANES, (lc + 1) * LANES)
        slab = pl.ds(slab0 + lc, 1)

        def row_tile(ref, hh, i):
            return pltpu.bitcast(jnp.broadcast_to(ref[0, hh, i, slab, :], (SUBLANES, LANES)), BF16)

        cnt = {(i, hh): row_tile(c_ref, hh, i) for i in key1_rows for hh in range(P_HEADS)}
        e1 = {(i, hh): row_tile(e1_ref, hh, i) for i in key1_rows for hh in range(P_HEADS)}
        for jc in range(P_NKEYS // BF16_ROWS):
            gate = {i: None for i in key1_rows}
            for hh in range(P_HEADS):
                rank2 = pltpu.bitcast(re_ref[lc, jc, hh, 0:SUBLANES, :], BF16)
                e2 = pltpu.bitcast(re_ref[lc, jc, hh, SUBLANES:BF16_ROWS, :], BF16)
                for i in key1_rows:
                    term = jnp.where(rank2 < cnt[i, hh], e2, 0.0) * e1[i, hh]
                    gate[i] = term if gate[i] is None else gate[i] + term
            for i in key1_rows:
                rows = slice(i * P_NKEYS + jc * BF16_ROWS, i * P_NKEYS + (jc + 1) * BF16_ROWS)
                w_ref[rows, lanes] = gate[i] * _gelu(a_ref[rows, lanes]).astype(BF16)


def _peer_dense_kernel(ht_ref, re_ref, c_ref, e1_ref, u_ref, vt_ref, x_ref, fg_ref,
                       o_ref, a0_sc, a1_sc, w0_sc, w1_sc, acc_sc, *, n_exp_tiles, n_tiles, final_norm):
    k = pl.program_id(0)
    c_pair = jnp.clip(k - 2, 0, n_tiles - 1)
    c_exp = c_pair % n_exp_tiles

    @pl.when(k == 0)
    def _():
        a1_sc[...] = jnp.zeros_like(a1_sc)
        w0_sc[...] = jnp.zeros_like(w0_sc)

    @pl.when(c_exp == 0)
    def _():
        acc_sc[...] = jnp.zeros_like(acc_sc)

    def stages(a_out, a_in, w_out, w_in):
        tb = ht_ref.shape[1]
        eb = a_in.shape[0]
        d = acc_sc.shape[0]
        n_chunks = tb // MXU_COLS
        blocks = [((i, i + 1), lc) for i in range(0, P_ROWS_PER_STEP, 2) for lc in range(tb // LANES)]
        n_units = n_chunks * M_SPLIT * (eb // MXU_COLS + d // MXU_COLS)
        units_per_block = n_units // len(blocks)
        todo = iter(range(n_units))
        tiles_per_row_tile = PREP_SLABS * LANES // tb
        if tiles_per_row_tile == 1:
            slab0 = 0
        else:
            b_tok = jnp.clip(k - 1, 0, n_tiles - 1) // n_exp_tiles
            slab0 = (b_tok % tiles_per_row_tile) * (tb // LANES)

        def vpu_slice():
            unit = next(todo)
            if unit % units_per_block == 0:
                b0 = unit // units_per_block
                _gate_times_act(re_ref, c_ref, e1_ref, a_in, w_out, slab0, blocks[b0:b0 + 1])

        for n in range(n_chunks):
            cols = slice(n * MXU_COLS, (n + 1) * MXU_COLS)
            for mh in range(M_SPLIT):
                rows = slice(mh * d // M_SPLIT, (mh + 1) * d // M_SPLIT)
                words = slice(mh * d // (2 * M_SPLIT), (mh + 1) * d // (2 * M_SPLIT))
                t = acc_sc[rows, cols]
                for kt in range(eb // MXU_COLS):
                    ks = slice(kt * MXU_COLS, (kt + 1) * MXU_COLS)
                    t = t + jnp.dot(pltpu.bitcast(vt_ref[words, ks], BF16), w_in[ks, cols],
                                    preferred_element_type=F32)
                    vpu_slice()
                acc_sc[rows, cols] = t
            for mh in range(M_SPLIT):
                rows = slice(mh * eb // M_SPLIT, (mh + 1) * eb // M_SPLIT)
                words = slice(mh * eb // (2 * M_SPLIT), (mh + 1) * eb // (2 * M_SPLIT))
                t = None
                for kt in range(d // MXU_COLS):
                    ks = slice(kt * MXU_COLS, (kt + 1) * MXU_COLS)
                    kw = slice(kt * MXU_COLS // 2, (kt + 1) * MXU_COLS // 2)
                    p = jnp.dot(pltpu.bitcast(u_ref[words, ks], BF16), pltpu.bitcast(ht_ref[kw, cols], BF16),
                                preferred_element_type=F32)
                    t = p if t is None else t + p
                    vpu_slice()
                a_out[rows, cols] = t

    @pl.when(k % 2 == 0)
    def _():
        stages(a0_sc, a1_sc, w1_sc, w0_sc)

    @pl.when(k % 2 == 1)
    def _():
        stages(a1_sc, a0_sc, w0_sc, w1_sc)

    @pl.when(jnp.logical_and(k >= 2, c_exp == n_exp_tiles - 1))
    def _():
        y = x_ref[...] + acc_sc[...].T
        if final_norm:
            y = _rms(y, fg_ref[...])
        o_ref[...] = y


def _peer_dense(x2, ht, re, cnt, e1, u_all, v_all, layer, final_g, final_norm, tb=1024):
    n, d = x2.shape
    n_exp = u_all.shape[1]
    eb = P_ROWS_PER_STEP * P_NKEYS
    assert eb == PACK_BLOCK, "V^T arrives tiled by the packing kernel's block"
    tb = min(tb, n)
    ne = n_exp // eb
    n_tiles = (n // tb) * ne
    last = n_tiles - 1

    def pair_a(k):
        return jnp.minimum(k, last)

    def pair_b(k):
        return jnp.clip(k - 1, 0, last)

    def pair_c(k):
        return jnp.clip(k - 2, 0, last)

    tiles_per_row_tile = PREP_SLABS * LANES // tb
    big = pl.BlockSpec((tb // LANES, P_NKEYS // BF16_ROWS, P_HEADS, BF16_ROWS, LANES),
                       lambda k: (pair_b(k) // ne, 0, 0, 0, 0))
    small = pl.BlockSpec((1, P_HEADS, P_ROWS_PER_STEP, PREP_SLABS, LANES),
                         lambda k: (pair_b(k) // ne // tiles_per_row_tile, 0, pair_b(k) % ne, 0, 0))
    return pl.pallas_call(
        functools.partial(_peer_dense_kernel, n_exp_tiles=ne, n_tiles=n_tiles, final_norm=final_norm),
        out_shape=jax.ShapeDtypeStruct((n, d), F32),
        grid=(n_tiles + 2,),
        in_specs=[
            pl.BlockSpec((d // 2, tb), lambda k: (0, pair_a(k) // ne)),
            big, small, small,
            pl.BlockSpec((eb // 2, d), lambda k: (pair_a(k) % ne, 0)),
            pl.BlockSpec((None, d // 2, eb), lambda k: (pair_c(k) % ne, 0, 0)),
            pl.BlockSpec((tb, d), lambda k: (pair_c(k) // ne, 0), pipeline_mode=pl.Buffered(1)),
            pl.BlockSpec((1, d), lambda k: (0, 0)),
        ],
        out_specs=pl.BlockSpec((tb, d), lambda k: (pair_c(k) // ne, 0)),
        scratch_shapes=[pltpu.VMEM((eb, tb), F32), pltpu.VMEM((eb, tb), F32),
                        pltpu.VMEM((eb, tb), BF16), pltpu.VMEM((eb, tb), BF16),
                        pltpu.VMEM((d, tb), F32)],
        compiler_params=pltpu.CompilerParams(
            dimension_semantics=("arbitrary",), vmem_limit_bytes=VMEM_LIMIT_BYTES),
        name="peer_dense",
    )(ht, re, cnt, e1, _pack_row_pairs(u_all, layer=layer),
      _pack_row_pairs(v_all, transpose=True, layer=layer, tiled=True), x2, final_g.reshape(1, d))


def _peer(x2, g, w_q, k1, k2, u_all, v_all, layer, final_g, final_norm):
    ht, re, cnt, e1 = _peer_prep(x2, g, w_q, k1, k2)
    return _peer_dense(x2, ht, re, cnt, e1, u_all, v_all, layer, final_g, final_norm)


def kernel(x, norm_mix, norm_ffn, final_norm, a_w_in, a_b_in, a_ln_g, a_ln_b, a_w_s, a_b_s, a_w_out,
           b_w_in, b_w_g2, b_b_g, b_norm, b_w_out, p_w_q, p_k1, p_k2, p_u, p_v):
    b, l, d = x.shape
    depth = norm_mix.shape[0]
    x2 = x.reshape(b * l, d)
    for i in range(depth):
        j = i // 2
        if i % 2 == 0:
            x2 = _mixer_a(x2, norm_mix[i], a_w_in[j], a_b_in[j], a_ln_g[j], a_ln_b[j],
                          a_w_s[j], a_b_s[j], a_w_out[j])
        else:
            x2 = _mixer_b(x2.reshape(b, l, d), norm_mix[i], b_w_in[j], b_w_g2[j], b_b_g[j],
                          b_norm[j], b_w_out[j]).reshape(b * l, d)
        x2 = _peer(x2, norm_ffn[i], p_w_q[i], p_k1[i], p_k2[i], p_u, p_v, i,
                   final_norm, i == depth - 1)
    return x2.reshape(b, l, d)
```

```python
import functools
import math

import jax
import jax.numpy as jnp
from jax import lax
from jax.experimental import pallas as pl
from jax.experimental.pallas import tpu as pltpu

F32 = jnp.float32
BF16 = jnp.bfloat16

NORM_EPS = 1e-6

LANES = 128
SUBLANES = 8
BF16_ROWS = 2 * SUBLANES
MXU_COLS = 256
PACK_BLOCK = 1024
VMEM_LIMIT_BYTES = 56 * 1024 * 1024

A_GROUPS = 8
A_CHUNK = 128
A_TOKENS = 256
B_HEADS = 4
B_GATE_RANK = 16
B_GATE_NORM = 16.0
B_CHUNK = 64
B_SUB = 16
B_TOKENS = 128
NEG_EXPONENT = -1e30
P_HEADS = 8
P_NKEYS = 128
P_TOPK = 16
P_ROWS_PER_STEP = 8
PREP_SLABS = SUBLANES
P_TOKENS = PREP_SLABS * LANES


def _rms(x, g):
    return x * lax.rsqrt(jnp.mean(x * x, axis=-1, keepdims=True) + NORM_EPS) * g


def _gelu(x):
    return 0.5 * x * (1.0 + lax.erf(x * (1.0 / math.sqrt(2.0))))


def _split_bf16(x):
    hi = x.astype(BF16)
    lo = (x - hi.astype(F32)).astype(BF16)
    return hi, lo


def _pack_kernel(x_ref, o_ref, *, transpose):
    x = x_ref[...]
    if transpose:
        x = x.T
    o_ref[...] = x.astype(BF16)


def _pack_row_pairs(w, transpose=False, layer=None, tiled=False):
    rows, cols = w.shape[-2:]
    blk = min(PACK_BLOCK, rows)
    if layer is None:
        in_spec = pl.BlockSpec((blk, cols), lambda i: (i, 0))
    else:
        in_spec = pl.BlockSpec((None, blk, cols), lambda i: (layer, i, 0))
    if transpose and tiled:
        out_shape = jax.ShapeDtypeStruct((rows // blk, cols, blk), BF16)
        out_spec = pl.BlockSpec((None, cols, blk), lambda i: (i, 0, 0))
    elif transpose:
        out_shape = jax.ShapeDtypeStruct((cols, rows), BF16)
        out_spec = pl.BlockSpec((cols, blk), lambda i: (0, i))
    else:
        out_shape = jax.ShapeDtypeStruct((rows, cols), BF16)
        out_spec = pl.BlockSpec((blk, cols), lambda i: (i, 0))
    return pl.pallas_call(
        functools.partial(_pack_kernel, transpose=transpose),
        out_shape=out_shape,
        grid=(rows // blk,),
        in_specs=[in_spec],
        out_specs=out_spec,
        compiler_params=pltpu.CompilerParams(
            dimension_semantics=("arbitrary",), vmem_limit_bytes=VMEM_LIMIT_BYTES),
        name="pack_bf16_pairs",
    )(w)


def _const_spec(shape):
    nd = len(shape)
    return pl.BlockSpec(shape, lambda *_: (0,) * nd)


def _mixer_a_kernel(x_ref, g_ref, win_ref, bin_ref, lng_ref, lnb_ref, ws_ref, bs_ref, wout_ref,
                    o_ref, y_sc):
    tm = x_ref.shape[0]
    width = lng_ref.shape[1]
    gdim = width // A_GROUPS
    x = x_ref[...]
    h = _rms(x, g_ref[...]).astype(BF16)
    z = _gelu(jnp.dot(h, win_ref[...], preferred_element_type=F32) + bin_ref[...])
    u = z[:, :width]
    v = z[:, width:]
    mu = jnp.mean(v, axis=-1, keepdims=True)
    vc = v - mu
    var = jnp.mean(vc * vc, axis=-1, keepdims=True)
    vn = (vc * lax.rsqrt(var + NORM_EPS) * lng_ref[...] + lnb_ref[...]).astype(BF16)
    for c in range(tm // A_CHUNK):
        rows = slice(c * A_CHUNK, (c + 1) * A_CHUNK)
        for g in range(A_GROUPS):
            cols = slice(g * gdim, (g + 1) * gdim)
            sv = jnp.dot(ws_ref[g], vn[rows, cols], preferred_element_type=F32) + bs_ref[:, cols]
            y_sc[rows, cols] = (u[rows, cols] * sv).astype(BF16)
    o_ref[...] = x + jnp.dot(y_sc[...], wout_ref[...], preferred_element_type=F32)


def _mixer_a(x2, g, w_in, b_in, ln_g, ln_b, w_s, b_s, w_out):
    n, d = x2.shape
    tm = A_TOKENS
    width = ln_g.shape[0]
    causal = jnp.tril(jnp.ones((A_CHUNK, A_CHUNK), dtype=bool))
    ws = jnp.where(causal[None], w_s, 0.0).astype(BF16)
    bs = jnp.repeat(b_s.T, width // A_GROUPS, axis=1)
    return pl.pallas_call(
        _mixer_a_kernel,
        out_shape=jax.ShapeDtypeStruct((n, d), F32),
        grid=(n // tm,),
        in_specs=[
            pl.BlockSpec((tm, d), lambda i: (i, 0)),
            _const_spec((1, d)),
            _const_spec((d, 2 * width)),
            _const_spec((1, 2 * width)),
            _const_spec((1, width)),
            _const_spec((1, width)),
            _const_spec((A_GROUPS, A_CHUNK, A_CHUNK)),
            _const_spec((A_CHUNK, width)),
            _const_spec((width, d)),
        ],
        out_specs=pl.BlockSpec((tm, d), lambda i: (i, 0)),
        scratch_shapes=[pltpu.VMEM((tm, width), BF16)],
        compiler_params=pltpu.CompilerParams(
            dimension_semantics=("arbitrary",), vmem_limit_bytes=VMEM_LIMIT_BYTES),
        name="mixer_a",
    )(x2, g.reshape(1, d), _pack_row_pairs(w_in), b_in.reshape(1, -1), ln_g.reshape(1, -1),
      ln_b.reshape(1, -1), ws, bs, _pack_row_pairs(w_out))


def _gla_kernel(x_ref, g_ref, win_ref, wg2_ref, bg_ref, ng_ref, wout_ref, tri_ref,
                o_ref, st_ref, oc_sc):
    nb, tc, d = x_ref.shape
    kdim = bg_ref.shape[1]
    vdim = ng_ref.shape[1]
    dk = kdim // B_HEADS
    dv = vdim // B_HEADS

    @pl.when(pl.program_id(0) == 0)
    def _():
        st_ref[...] = jnp.zeros_like(st_ref)

    x = x_ref[...].reshape(nb * tc, d)
    h = _rms(x, g_ref[...]).astype(BF16)
    proj = jnp.dot(h, win_ref[...], preferred_element_type=F32)
    q = proj[:, :kdim] * (dk ** -0.5)
    k = proj[:, kdim:2 * kdim]
    v = proj[:, 2 * kdim:2 * kdim + vdim]
    r = proj[:, 2 * kdim + vdim:2 * kdim + 2 * vdim]
    glr = proj[:, 2 * kdim + 2 * vdim:]

    g_hi, g_lo = _split_bf16(glr)
    w_hi = wg2_ref[0]
    w_lo = wg2_ref[1]
    xg = (jnp.dot(g_hi, w_hi, preferred_element_type=F32)
          + jnp.dot(g_hi, w_lo, preferred_element_type=F32)
          + jnp.dot(g_lo, w_hi, preferred_element_type=F32)) + bg_ref[...]
    log_a = (jnp.minimum(xg, 0.0) - jnp.log(1.0 + jnp.exp(-jnp.abs(xg)))) * (1.0 / B_GATE_NORM)

    a_hi, a_lo = _split_bf16(log_a)
    tri = tri_ref[...]
    gcum = jnp.concatenate(
        [jnp.dot(tri, a_hi[bi * tc:(bi + 1) * tc], preferred_element_type=F32)
         + jnp.dot(tri, a_lo[bi * tc:(bi + 1) * tc], preferred_element_type=F32)
         for bi in range(nb)], axis=0)

    qd = (q * jnp.exp(gcum)).astype(BF16)
    vb = v.astype(BF16)
    row = lax.broadcasted_iota(jnp.int32, (B_CHUNK, B_CHUNK), 0)
    col = lax.broadcasted_iota(jnp.int32, (B_CHUNK, B_CHUNK), 1)
    same_sub = (row // B_SUB) == (col // B_SUB)
    causal = row >= col
    rid = lax.broadcasted_iota(jnp.int32, (B_CHUNK, kdim), 0)
    n_sub = B_CHUNK // B_SUB

    nt = (((1,), (1,)), ((), ()))
    tn = (((0,), (0,)), ((), ()))
    for c in range(tc // B_CHUNK):
        q_hat, k_diag, q_off, k_off = {}, {}, {}, {}
        for bi in range(nb):
            rows = slice(bi * tc + c * B_CHUNK, bi * tc + (c + 1) * B_CHUNK)
            g_c = gcum[rows]
            k_c = k[rows]
            gb = [g_c[B_SUB * i - 1:B_SUB * i, :] for i in range(1, n_sub)]
            gb_rows = jnp.concatenate([jnp.zeros((B_SUB, kdim), F32)]
                                      + [jnp.broadcast_to(b, (B_SUB, kdim)) for b in gb], axis=0)
            qh = q[rows] * jnp.exp(g_c - gb_rows)
            q_hat[bi] = qh.astype(BF16)
            k_diag[bi] = (k_c * jnp.exp(gb_rows - g_c)).astype(BF16)
            q_off[bi] = [jnp.where(rid // B_SUB == i, qh, 0.0).astype(BF16) for i in range(1, n_sub)]
            k_off[bi] = [(k_c * jnp.exp(jnp.where(rid < B_SUB * i, b - g_c, NEG_EXPONENT))).astype(BF16)
                         for i, b in zip(range(1, n_sub), gb)]

        units = []
        for bi in range(nb):
            rows = slice(bi * tc + c * B_CHUNK, bi * tc + (c + 1) * B_CHUNK)
            last = bi * tc + (c + 1) * B_CHUNK - 1
            for hh in range(B_HEADS):
                units.append((bi, hh, rows, last, slice(hh * dk, (hh + 1) * dk), slice(hh * dv, (hh + 1) * dv)))
        s_off = [lax.dot_general(jnp.concatenate([qo[:, kc] for qo in q_off[bi]], axis=1),
                                 jnp.concatenate([ko[:, kc] for ko in k_off[bi]], axis=1),
                                 nt, preferred_element_type=F32)
                 for bi, _, _, _, kc, _ in units]
        s_diag = [lax.dot_general(q_hat[bi][:, kc], k_diag[bi][:, kc], nt, preferred_element_type=F32)
                  for bi, _, _, _, kc, _ in units]
        scores = [so + jnp.where(same_sub, jnp.where(causal, sd, 0.0), 0.0) for so, sd in zip(s_off, s_diag)]
        inter = [lax.dot_general(qd[rows, kc], st_ref[bi, hh].astype(BF16), nt, preferred_element_type=F32)
                 for bi, hh, rows, _, kc, _ in units]
        g_last = [gcum[last:last + 1, kc] for _, _, _, last, kc, _ in units]
        ks = [(k[rows, kc] * jnp.exp(gl - gcum[rows, kc])).astype(BF16)
              for (_, _, rows, _, kc, _), gl in zip(units, g_last)]
        upd = [lax.dot_general(vb[rows, vc], kk, tn, preferred_element_type=F32)
               for (_, _, rows, _, _, vc), kk in zip(units, ks)]
        for (bi, hh, rows, _, _, vc), sc, io in zip(units, scores, inter):
            oc_sc[rows, vc] = jnp.dot(sc.astype(BF16), vb[rows, vc], preferred_element_type=F32) + io
        for (bi, hh, _, _, _, _), gl, up in zip(units, g_last, upd):
            st_ref[bi, hh] = st_ref[bi, hh] * jnp.exp(gl) + up

    ng = ng_ref[...]
    for hh in range(B_HEADS):
        vc = slice(hh * dv, (hh + 1) * dv)
        oh = oc_sc[:, vc]
        oh = oh * lax.rsqrt(jnp.mean(oh * oh, axis=-1, keepdims=True) + NORM_EPS) * ng[:, vc]
        rh = r[:, vc]
        oc_sc[:, vc] = oh * (rh / (1.0 + jnp.exp(-rh)))
    y = x + jnp.dot(oc_sc[...].astype(BF16), wout_ref[...], preferred_element_type=F32)
    o_ref[...] = y.reshape(nb, tc, d)


def _mixer_b(x3, g, w_in, w_g2, b_g, norm_g, w_out):
    b, l, d = x3.shape
    tc = B_TOKENS
    kdim = w_g2.shape[1]
    vdim = w_out.shape[0]
    main = 2 * kdim + 2 * vdim
    w_in_p = _pack_row_pairs(jnp.concatenate(
        [w_in, jnp.zeros((d, LANES - B_GATE_RANK), w_in.dtype)], axis=1))
    w_g2_p = jnp.concatenate([w_g2, jnp.zeros((LANES - B_GATE_RANK, kdim), w_g2.dtype)], axis=0)
    hi = w_g2_p.astype(BF16)
    lo = (w_g2_p - hi.astype(F32)).astype(BF16)
    wg2 = jnp.stack([hi, lo])
    ng = jnp.tile(norm_g, B_HEADS).reshape(1, vdim)
    idx = jnp.arange(tc)
    tri = ((idx[:, None] >= idx[None, :]) &
           (idx[:, None] // B_CHUNK == idx[None, :] // B_CHUNK)).astype(BF16)
    return pl.pallas_call(
        _gla_kernel,
        out_shape=jax.ShapeDtypeStruct((b, l, d), F32),
        grid=(l // tc,),
        in_specs=[
            pl.BlockSpec((b, tc, d), lambda j: (0, j, 0)),
            _const_spec((1, d)),
            _const_spec((d, main + LANES)),
            _const_spec((2, LANES, kdim)),
            _const_spec((1, kdim)),
            _const_spec((1, vdim)),
            _const_spec((vdim, d)),
            _const_spec((tc, tc)),
        ],
        out_specs=pl.BlockSpec((b, tc, d), lambda j: (0, j, 0)),
        scratch_shapes=[pltpu.VMEM((b, B_HEADS, vdim // B_HEADS, kdim // B_HEADS), F32),
                        pltpu.VMEM((b * tc, vdim), F32)],
        compiler_params=pltpu.CompilerParams(
            dimension_semantics=("arbitrary",), vmem_limit_bytes=VMEM_LIMIT_BYTES),
        name="mixer_b_gla",
    )(x3, g.reshape(1, d), w_in_p, wg2, b_g.reshape(1, kdim), ng, _pack_row_pairs(w_out), tri)


def _sort16_pairs():
    pairs = []

    def merge(lo, n, r):
        step = 2 * r
        if step < n:
            merge(lo, n, step)
            merge(lo + r, n, step)
            pairs.extend((i, i + r) for i in range(lo + r, lo + n - r, step))
        else:
            pairs.append((lo, lo + r))

    def sort(lo, n):
        if n > 1:
            sort(lo, n // 2)
            sort(lo + n // 2, n // 2)
            merge(lo, n, 1)

    sort(0, P_TOPK)
    return pairs


def _sort16(vals):
    vals = list(vals)
    for i, j in _sort16_pairs():
        vals[i], vals[j] = jnp.maximum(vals[i], vals[j]), jnp.minimum(vals[i], vals[j])
    return vals


def _merge_top16(a, b):
    c = list(a)
    for m, bv in enumerate(b):
        i = P_TOPK - 1 - m
        c[i] = jnp.maximum(c[i], bv)
    dist = P_TOPK // 2
    while dist:
        for i in range(P_TOPK):
            if not i & dist:
                c[i], c[i + dist] = jnp.maximum(c[i], c[i + dist]), jnp.minimum(c[i], c[i + dist])
        dist //= 2
    return c


def _top16_of_keys(s_sc, lo, hi):
    if hi - lo == P_TOPK:
        return _sort16([s_sc[k * SUBLANES:(k + 1) * SUBLANES, :] for k in range(lo, hi)])
    mid = (lo + hi) // 2
    return _merge_top16(_top16_of_keys(s_sc, lo, mid), _top16_of_keys(s_sc, mid, hi))


def _count_true_prefix(pred, thr):
    steps = (8, 4, 2, 1)
    masks = []

    def pick(ms, cands):
        if not ms:
            return cands[0]
        half = len(cands) // 2
        return jnp.where(ms[0], pick(ms[1:], cands[half:]), pick(ms[1:], cands[:half]))

    for lvl, step in enumerate(steps):
        cands = []
        for combo in range(2 ** lvl):
            taken = sum(s for bit, s in enumerate(steps[:lvl]) if (combo >> (lvl - 1 - bit)) & 1)
            cands.append(thr[taken + step - 1])
        masks.append(pred(pick(masks, cands)))
    count = None
    for m, step in zip(masks, steps):
        term = jnp.where(m, float(step), 0.0)
        count = term if count is None else count + term
    return jnp.where(pred(thr[P_TOPK - 1]), float(P_TOPK), count)


def _to_key_major(s, s_sc):
    for g in range(s.shape[0] // SUBLANES):
        for c in range(PREP_SLABS):
            s_sc[pl.ds(g * SUBLANES * PREP_SLABS + c, SUBLANES, stride=PREP_SLABS), :] = (
                s[g * SUBLANES:(g + 1) * SUBLANES, c * LANES:(c + 1) * LANES])


def _peer_prep_kernel(x_ref, g_ref, wq_ref, k1_ref, k2_ref,
                      ht_ref, re_ref, c_ref, e1_ref,
                      ht_sc, qt_sc, s1_sc, s2_sc, rk_sc, e2_sc):
    hh = pl.program_id(1)
    half = k1_ref.shape[1]

    @pl.when(hh == 0)
    def _():
        h = _rms(x_ref[...], g_ref[...])
        ht = h.T.astype(BF16)
        ht_ref[...] = ht
        ht_sc[...] = ht

    qt_sc[...] = jnp.dot(wq_ref[0], ht_sc[...], preferred_element_type=F32).astype(BF16)
    s1 = jnp.dot(k1_ref[...], qt_sc[:half, :], preferred_element_type=F32)
    s2 = jnp.dot(k2_ref[...], qt_sc[half:, :], preferred_element_type=F32)
    _to_key_major(s1, s1_sc)
    _to_key_major(s2, s2_sc)

    v1 = _top16_of_keys(s1_sc, 0, P_NKEYS)
    v2 = _top16_of_keys(s2_sc, 0, P_NKEYS)

    top = [v1[0] + v2[b] for b in range(P_TOPK)]
    for a in range(1, P_TOPK // 2):
        top = _merge_top16(top, [v1[a] + v2[b] for b in range(P_TOPK // (a + 1))])
    top = _merge_top16(top, [v1[a] + v2[0] for a in range(P_TOPK // 2, P_TOPK)])
    tau = top[P_TOPK - 1]
    z = None
    for m in range(P_TOPK):
        term = jnp.exp(top[m] - top[0])
        z = term if z is None else z + term
    inv_z = 1.0 / z

    for k in range(P_NKEYS):
        tile = slice(k * SUBLANES, (k + 1) * SUBLANES)
        x2 = s2_sc[tile, :]
        rk_sc[tile, :] = _count_true_prefix(lambda t, x2=x2: t > x2, v2)
        e2_sc[tile, :] = jnp.exp(x2 - v2[0])
        x1 = s1_sc[tile, :]
        c_ref[0, 0, k] = _count_true_prefix(lambda t, x1=x1: x1 + t >= tau, v2)
        e1_ref[0, 0, k] = jnp.exp(x1 - v1[0]) * inv_z

    def key_tile(src_sc, jc, c):
        halves = [src_sc[pl.ds((jc * BF16_ROWS + r0) * PREP_SLABS + c, SUBLANES, stride=PREP_SLABS), :]
                  for r0 in (0, SUBLANES)]
        return jnp.concatenate(halves, axis=0).astype(BF16)

    for jc in range(P_NKEYS // BF16_ROWS):
        for c in range(PREP_SLABS):
            re_ref[c, jc, 0, 0:BF16_ROWS, :] = key_tile(rk_sc, jc, c)
            re_ref[c, jc, 0, BF16_ROWS:2 * BF16_ROWS, :] = key_tile(e2_sc, jc, c)


def _peer_prep(x2, g, w_q, k1, k2):
    n, d = x2.shape
    tt = P_TOKENS
    qdim = w_q.shape[1] // P_HEADS
    chunks = P_NKEYS // BF16_ROWS
    re_shape = jax.ShapeDtypeStruct((n // LANES, chunks, P_HEADS, 2 * BF16_ROWS, LANES), BF16)
    re_spec = pl.BlockSpec((PREP_SLABS, chunks, 1, 2 * BF16_ROWS, LANES), lambda t, h: (t, 0, h, 0, 0))
    row_shape = jax.ShapeDtypeStruct((n // tt, P_HEADS, P_NKEYS, PREP_SLABS, LANES), F32)
    row_spec = pl.BlockSpec((1, 1, P_NKEYS, PREP_SLABS, LANES), lambda t, h: (t, h, 0, 0, 0))
    wq = _pack_row_pairs(w_q, transpose=True).reshape(P_HEADS, qdim, d)
    return pl.pallas_call(
        _peer_prep_kernel,
        out_shape=(jax.ShapeDtypeStruct((d, n), BF16), re_shape, row_shape, row_shape),
        grid=(n // tt, P_HEADS),
        in_specs=[
            pl.BlockSpec((tt, d), lambda t, h: (t, 0)),
            _const_spec((1, d)),
            pl.BlockSpec((1, qdim, d), lambda t, h: (h, 0, 0)),
            _const_spec(k1.shape),
            _const_spec(k2.shape),
        ],
        out_specs=(pl.BlockSpec((d, tt), lambda t, h: (0, t)), re_spec, row_spec, row_spec),
        scratch_shapes=[pltpu.VMEM((d, tt), BF16),
                        pltpu.VMEM((qdim, tt), BF16),
                        pltpu.VMEM((P_NKEYS * PREP_SLABS, LANES), F32),
                        pltpu.VMEM((P_NKEYS * PREP_SLABS, LANES), F32),
                        pltpu.VMEM((P_NKEYS * PREP_SLABS, LANES), F32),
                        pltpu.VMEM((P_NKEYS * PREP_SLABS, LANES), F32)],
        compiler_params=pltpu.CompilerParams(
            dimension_semantics=("arbitrary", "arbitrary"), vmem_limit_bytes=VMEM_LIMIT_BYTES),
        name="peer_prep",
    )(x2, g.reshape(1, d), wq, _pack_row_pairs(k1), _pack_row_pairs(k2))


def _gate_times_act(re_ref, c_ref, e1_ref, a_ref, w_ref, blocks):
    for key1_rows, lc in blocks:
        lanes = slice(lc * LANES, (lc + 1) * LANES)

        def row_tile(ref, hh, i):
            return jnp.broadcast_to(ref[0, hh, i, lc:lc + 1, :], (BF16_ROWS, LANES)).astype(BF16)

        cnt = {(i, hh): row_tile(c_ref, hh, i) for i in key1_rows for hh in range(P_HEADS)}
        e1 = {(i, hh): row_tile(e1_ref, hh, i) for i in key1_rows for hh in range(P_HEADS)}
        for jc in range(P_NKEYS // BF16_ROWS):
            gate = {i: None for i in key1_rows}
            for hh in range(P_HEADS):
                row0 = ((lc * (P_NKEYS // BF16_ROWS) + jc) * P_HEADS + hh) * 2 * BF16_ROWS
                rank2 = re_ref[row0:row0 + BF16_ROWS, :]
                e2 = re_ref[row0 + BF16_ROWS:row0 + 2 * BF16_ROWS, :]
                for i in key1_rows:
                    term = jnp.where(rank2 < cnt[i, hh], e2, 0.0) * e1[i, hh]
                    gate[i] = term if gate[i] is None else gate[i] + term
            for i in key1_rows:
                rows = slice(i * P_NKEYS + jc * BF16_ROWS, i * P_NKEYS + (jc + 1) * BF16_ROWS)
                w_ref[rows, lanes] = gate[i] * _gelu(a_ref[rows, lanes]).astype(BF16)


def _peer_dense_kernel(ht_ref, re_ref, c_ref, e1_ref, u_ref, vt_ref, x_ref, fg_ref,
                       o_ref, a0_sc, a1_sc, w0_sc, w1_sc, acc_sc, re_sc, *, n_exp_tiles, n_tiles, final_norm):
    k = pl.program_id(0)
    c_pair = jnp.clip(k - 2, 0, n_tiles - 1)
    c_exp = c_pair % n_exp_tiles

    @pl.when(k == 0)
    def _():
        a1_sc[...] = jnp.zeros_like(a1_sc)
        w0_sc[...] = jnp.zeros_like(w0_sc)

    @pl.when(c_exp == 0)
    def _():
        acc_sc[...] = jnp.zeros_like(acc_sc)

    @pl.when(jnp.clip(k - 1, 0, n_tiles - 1) % n_exp_tiles == 0)
    def _():
        tile_rows = re_ref.shape[3]
        for lc in range(re_ref.shape[0]):
            for jc in range(re_ref.shape[1]):
                for hh in range(re_ref.shape[2]):
                    row0 = ((lc * re_ref.shape[1] + jc) * re_ref.shape[2] + hh) * tile_rows
                    re_sc[row0:row0 + tile_rows, :] = re_ref[lc, jc, hh]

    def stages(a_out, a_in, w_out, w_in):
        tb = ht_ref.shape[1]
        eb = a_in.shape[0]
        d = acc_sc.shape[0]
        blocks = [((i, i + 1), lc) for i in range(0, P_ROWS_PER_STEP, 2) for lc in range(tb // LANES)]
        n_units = (tb // MXU_COLS) * (eb // MXU_COLS + d // MXU_COLS)
        units_per_block = n_units // len(blocks)
        units = iter(range(n_units))

        def vpu_slice():
            unit = next(units)
            if unit % units_per_block == 0:
                b0 = unit // units_per_block
                _gate_times_act(re_sc, c_ref, e1_ref, a_in, w_out, blocks[b0:b0 + 1])

        for n in range(tb // MXU_COLS):
            cols = slice(n * MXU_COLS, (n + 1) * MXU_COLS)
            t = acc_sc[:, cols]
            for kt in range(eb // MXU_COLS):
                ks = slice(kt * MXU_COLS, (kt + 1) * MXU_COLS)
                t = t + jnp.dot(vt_ref[:, ks], w_in[ks, cols], preferred_element_type=F32)
                vpu_slice()
            acc_sc[:, cols] = t
            t = None
            for kt in range(d // MXU_COLS):
                ks = slice(kt * MXU_COLS, (kt + 1) * MXU_COLS)
                p = jnp.dot(u_ref[:, ks], ht_ref[ks, cols], preferred_element_type=F32)
                t = p if t is None else t + p
                vpu_slice()
            a_out[:, cols] = t

    @pl.when(k % 2 == 0)
    def _():
        stages(a0_sc, a1_sc, w1_sc, w0_sc)

    @pl.when(k % 2 == 1)
    def _():
        stages(a1_sc, a0_sc, w0_sc, w1_sc)

    @pl.when(jnp.logical_and(k >= 2, c_exp == n_exp_tiles - 1))
    def _():
        y = x_ref[...] + acc_sc[...].T
        if final_norm:
            y = _rms(y, fg_ref[...])
        o_ref[...] = y


def _peer_dense(x2, ht, re, cnt, e1, u_all, v_all, layer, final_g, final_norm):
    n, d = x2.shape
    n_exp = u_all.shape[1]
    eb = P_ROWS_PER_STEP * P_NKEYS
    assert eb == PACK_BLOCK, "V^T arrives tiled by the packing kernel's block"
    tb = P_TOKENS
    ne = n_exp // eb
    n_tiles = (n // tb) * ne
    last = n_tiles - 1

    def pair_a(k):
        return jnp.minimum(k, last)

    def pair_b(k):
        return jnp.clip(k - 1, 0, last)

    def pair_c(k):
        return jnp.clip(k - 2, 0, last)

    re_tiles = (tb // LANES) * (P_NKEYS // BF16_ROWS) * P_HEADS
    big = pl.BlockSpec((tb // LANES, P_NKEYS // BF16_ROWS, P_HEADS, 2 * BF16_ROWS, LANES),
                       lambda k: (pair_b(k) // ne, 0, 0, 0, 0), pipeline_mode=pl.Buffered(1))
    small = pl.BlockSpec((1, P_HEADS, P_ROWS_PER_STEP, PREP_SLABS, LANES),
                         lambda k: (pair_b(k) // ne, 0, pair_b(k) % ne, 0, 0))
    return pl.pallas_call(
        functools.partial(_peer_dense_kernel, n_exp_tiles=ne, n_tiles=n_tiles, final_norm=final_norm),
        out_shape=jax.ShapeDtypeStruct((n, d), F32),
        grid=(n_tiles + 2,),
        in_specs=[
            pl.BlockSpec((d, tb), lambda k: (0, pair_a(k) // ne)),
            big, small, small,
            pl.BlockSpec((eb, d), lambda k: (pair_a(k) % ne, 0)),
            pl.BlockSpec((None, d, eb), lambda k: (pair_c(k) % ne, 0, 0)),
            pl.BlockSpec((tb, d), lambda k: (pair_c(k) // ne, 0), pipeline_mode=pl.Buffered(1)),
            pl.BlockSpec((1, d), lambda k: (0, 0)),
        ],
        out_specs=pl.BlockSpec((tb, d), lambda k: (pair_c(k) // ne, 0)),
        scratch_shapes=[pltpu.VMEM((eb, tb), F32), pltpu.VMEM((eb, tb), F32),
                        pltpu.VMEM((eb, tb), BF16), pltpu.VMEM((eb, tb), BF16),
                        pltpu.VMEM((d, tb), F32),
                        pltpu.VMEM((re_tiles * 2 * BF16_ROWS, LANES), BF16)],
        compiler_params=pltpu.CompilerParams(
            dimension_semantics=("arbitrary",), vmem_limit_bytes=VMEM_LIMIT_BYTES),
        name="peer_dense",
    )(ht, re, cnt, e1, _pack_row_pairs(u_all, layer=layer),
      _pack_row_pairs(v_all, transpose=True, layer=layer, tiled=True), x2, final_g.reshape(1, d))


def _peer(x2, g, w_q, k1, k2, u_all, v_all, layer, final_g, final_norm):
    ht, re, cnt, e1 = _peer_prep(x2, g, w_q, k1, k2)
    return _peer_dense(x2, ht, re, cnt, e1, u_all, v_all, layer, final_g, final_norm)


def kernel(x, norm_mix, norm_ffn, final_norm, a_w_in, a_b_in, a_ln_g, a_ln_b, a_w_s, a_b_s, a_w_out,
           b_w_in, b_w_g2, b_b_g, b_norm, b_w_out, p_w_q, p_k1, p_k2, p_u, p_v):
    b, l, d = x.shape
    depth = norm_mix.shape[0]
    assert (b * l) % P_TOKENS == 0 and (b * l) % A_TOKENS == 0 and l % B_TOKENS == 0, (b, l)
    assert p_k1.shape[1] == P_NKEYS and p_u.shape[1] == P_NKEYS * P_NKEYS, (p_k1.shape, p_u.shape)
    x2 = x.reshape(b * l, d)
    for i in range(depth):
        j = i // 2
        if i % 2 == 0:
            x2 = _mixer_a(x2, norm_mix[i], a_w_in[j], a_b_in[j], a_ln_g[j], a_ln_b[j],
                          a_w_s[j], a_b_s[j], a_w_out[j])
        else:
            x2 = _mixer_b(x2.reshape(b, l, d), norm_mix[i], b_w_in[j], b_w_g2[j], b_b_g[j],
                          b_norm[j], b_w_out[j]).reshape(b * l, d)
        x2 = _peer(x2, norm_ffn[i], p_w_q[i], p_k1[i], p_k2[i], p_u, p_v, i,
                   final_norm, i == depth - 1)
    return x2.reshape(b, l, d)
```

```python
import functools
import math

import jax
import jax.numpy as jnp
from jax import lax
from jax.experimental import pallas as pl
from jax.experimental.pallas import tpu as pltpu

F32 = jnp.float32
BF16 = jnp.bfloat16

NORM_EPS = 1e-6

LANES = 128
SUBLANES = 8
BF16_ROWS = 2 * SUBLANES
MXU_COLS = 256
PACK_BLOCK = 1024
VMEM_LIMIT_BYTES = 56 * 1024 * 1024

A_GROUPS = 8
A_CHUNK = 128
A_TOKENS = 256
B_HEADS = 4
B_GATE_RANK = 16
B_GATE_NORM = 16.0
B_CHUNK = 64
B_SUB = 16
B_TOKENS = 128
NEG_EXPONENT = -1e30
P_HEADS = 8
P_NKEYS = 128
P_TOPK = 16
P_ROWS_PER_STEP = 8
PREP_SLABS = SUBLANES
P_TOKENS = PREP_SLABS * LANES


def _rms(x, g):
    return x * lax.rsqrt(jnp.mean(x * x, axis=-1, keepdims=True) + NORM_EPS) * g


def _gelu(x):
    return 0.5 * x * (1.0 + lax.erf(x * (1.0 / math.sqrt(2.0))))


def _split_bf16(x):
    hi = x.astype(BF16)
    lo = (x - hi.astype(F32)).astype(BF16)
    return hi, lo


def _pack_kernel(x_ref, o_ref, *, transpose):
    x = x_ref[...]
    if transpose:
        x = x.T
    o_ref[...] = x.astype(BF16)


def _bf16_copy(w, transpose=False, layer=None, tiled=False):
    rows, cols = w.shape[-2:]
    blk = min(PACK_BLOCK, rows)
    if layer is None:
        in_spec = pl.BlockSpec((blk, cols), lambda i: (i, 0))
    else:
        in_spec = pl.BlockSpec((None, blk, cols), lambda i: (layer, i, 0))
    if transpose and tiled:
        out_shape = jax.ShapeDtypeStruct((rows // blk, cols, blk), BF16)
        out_spec = pl.BlockSpec((None, cols, blk), lambda i: (i, 0, 0))
    elif transpose:
        out_shape = jax.ShapeDtypeStruct((cols, rows), BF16)
        out_spec = pl.BlockSpec((cols, blk), lambda i: (0, i))
    else:
        out_shape = jax.ShapeDtypeStruct((rows, cols), BF16)
        out_spec = pl.BlockSpec((blk, cols), lambda i: (i, 0))
    return pl.pallas_call(
        functools.partial(_pack_kernel, transpose=transpose),
        out_shape=out_shape,
        grid=(rows // blk,),
        in_specs=[in_spec],
        out_specs=out_spec,
        compiler_params=pltpu.CompilerParams(
            dimension_semantics=("arbitrary",), vmem_limit_bytes=VMEM_LIMIT_BYTES),
        name="bf16_copy",
    )(w)


def _const_spec(shape):
    nd = len(shape)
    return pl.BlockSpec(shape, lambda *_: (0,) * nd)


def _mixer_a_kernel(x_ref, g_ref, win_ref, bin_ref, lng_ref, lnb_ref, ws_ref, bs_ref, wout_ref,
                    o_ref, y_sc):
    tm = x_ref.shape[0]
    width = lng_ref.shape[1]
    gdim = width // A_GROUPS
    x = x_ref[...]
    h = _rms(x, g_ref[...]).astype(BF16)
    z = _gelu(jnp.dot(h, win_ref[...], preferred_element_type=F32) + bin_ref[...])
    u = z[:, :width]
    v = z[:, width:]
    mu = jnp.mean(v, axis=-1, keepdims=True)
    vc = v - mu
    var = jnp.mean(vc * vc, axis=-1, keepdims=True)
    vn = (vc * lax.rsqrt(var + NORM_EPS) * lng_ref[...] + lnb_ref[...]).astype(BF16)
    for c in range(tm // A_CHUNK):
        rows = slice(c * A_CHUNK, (c + 1) * A_CHUNK)
        for g in range(A_GROUPS):
            cols = slice(g * gdim, (g + 1) * gdim)
            sv = jnp.dot(ws_ref[g], vn[rows, cols], preferred_element_type=F32) + bs_ref[:, cols]
            y_sc[rows, cols] = (u[rows, cols] * sv).astype(BF16)
    o_ref[...] = x + jnp.dot(y_sc[...], wout_ref[...], preferred_element_type=F32)


def _mixer_a(x2, g, w_in, b_in, ln_g, ln_b, w_s, b_s, w_out):
    n, d = x2.shape
    tm = A_TOKENS
    width = ln_g.shape[0]
    causal = jnp.tril(jnp.ones((A_CHUNK, A_CHUNK), dtype=bool))
    ws = jnp.where(causal[None], w_s, 0.0).astype(BF16)
    bs = jnp.repeat(b_s.T, width // A_GROUPS, axis=1)
    return pl.pallas_call(
        _mixer_a_kernel,
        out_shape=jax.ShapeDtypeStruct((n, d), F32),
        grid=(n // tm,),
        in_specs=[
            pl.BlockSpec((tm, d), lambda i: (i, 0)),
            _const_spec((1, d)),
            _const_spec((d, 2 * width)),
            _const_spec((1, 2 * width)),
            _const_spec((1, width)),
            _const_spec((1, width)),
            _const_spec((A_GROUPS, A_CHUNK, A_CHUNK)),
            _const_spec((A_CHUNK, width)),
            _const_spec((width, d)),
        ],
        out_specs=pl.BlockSpec((tm, d), lambda i: (i, 0)),
        scratch_shapes=[pltpu.VMEM((tm, width), BF16)],
        compiler_params=pltpu.CompilerParams(
            dimension_semantics=("arbitrary",), vmem_limit_bytes=VMEM_LIMIT_BYTES),
        name="mixer_a",
    )(x2, g.reshape(1, d), _bf16_copy(w_in), b_in.reshape(1, -1), ln_g.reshape(1, -1),
      ln_b.reshape(1, -1), ws, bs, _bf16_copy(w_out))


def _gla_kernel(x_ref, g_ref, win_ref, wg2_ref, bg_ref, ng_ref, wout_ref, tri_ref,
                o_ref, st_ref, oc_sc):
    nb, tc, d = x_ref.shape
    kdim = bg_ref.shape[1]
    vdim = ng_ref.shape[1]
    dk = kdim // B_HEADS
    dv = vdim // B_HEADS

    @pl.when(pl.program_id(0) == 0)
    def _():
        st_ref[...] = jnp.zeros_like(st_ref)

    x = x_ref[...].reshape(nb * tc, d)
    h = _rms(x, g_ref[...]).astype(BF16)
    proj = jnp.dot(h, win_ref[...], preferred_element_type=F32)
    q = proj[:, :kdim] * (dk ** -0.5)
    k = proj[:, kdim:2 * kdim]
    v = proj[:, 2 * kdim:2 * kdim + vdim]
    r = proj[:, 2 * kdim + vdim:2 * kdim + 2 * vdim]
    glr = proj[:, 2 * kdim + 2 * vdim:]

    g_hi, g_lo = _split_bf16(glr)
    w_hi = wg2_ref[0]
    w_lo = wg2_ref[1]
    xg = (jnp.dot(g_hi, w_hi, preferred_element_type=F32)
          + jnp.dot(g_hi, w_lo, preferred_element_type=F32)
          + jnp.dot(g_lo, w_hi, preferred_element_type=F32)) + bg_ref[...]
    log_a = (jnp.minimum(xg, 0.0) - jnp.log(1.0 + jnp.exp(-jnp.abs(xg)))) * (1.0 / B_GATE_NORM)

    a_hi, a_lo = _split_bf16(log_a)
    tri = tri_ref[...]
    gcum = jnp.concatenate(
        [jnp.dot(tri, a_hi[bi * tc:(bi + 1) * tc], preferred_element_type=F32)
         + jnp.dot(tri, a_lo[bi * tc:(bi + 1) * tc], preferred_element_type=F32)
         for bi in range(nb)], axis=0)

    qd = (q * jnp.exp(gcum)).astype(BF16)
    vb = v.astype(BF16)
    row = lax.broadcasted_iota(jnp.int32, (B_CHUNK, B_CHUNK), 0)
    col = lax.broadcasted_iota(jnp.int32, (B_CHUNK, B_CHUNK), 1)
    same_sub = (row // B_SUB) == (col // B_SUB)
    causal = row >= col
    rid = lax.broadcasted_iota(jnp.int32, (B_CHUNK, kdim), 0)
    n_sub = B_CHUNK // B_SUB

    nt = (((1,), (1,)), ((), ()))
    tn = (((0,), (0,)), ((), ()))
    for c in range(tc // B_CHUNK):
        q_hat, k_diag, q_off, k_off = {}, {}, {}, {}
        for bi in range(nb):
            rows = slice(bi * tc + c * B_CHUNK, bi * tc + (c + 1) * B_CHUNK)
            g_c = gcum[rows]
            k_c = k[rows]
            gb = [g_c[B_SUB * i - 1:B_SUB * i, :] for i in range(1, n_sub)]
            gb_rows = jnp.concatenate([jnp.zeros((B_SUB, kdim), F32)]
                                      + [jnp.broadcast_to(b, (B_SUB, kdim)) for b in gb], axis=0)
            qh = q[rows] * jnp.exp(g_c - gb_rows)
            q_hat[bi] = qh.astype(BF16)
            k_diag[bi] = (k_c * jnp.exp(gb_rows - g_c)).astype(BF16)
            q_off[bi] = [jnp.where(rid // B_SUB == i, qh, 0.0).astype(BF16) for i in range(1, n_sub)]
            k_off[bi] = [(k_c * jnp.exp(jnp.where(rid < B_SUB * i, b - g_c, NEG_EXPONENT))).astype(BF16)
                         for i, b in zip(range(1, n_sub), gb)]

        units = []
        for bi in range(nb):
            rows = slice(bi * tc + c * B_CHUNK, bi * tc + (c + 1) * B_CHUNK)
            last = bi * tc + (c + 1) * B_CHUNK - 1
            for hh in range(B_HEADS):
                units.append((bi, hh, rows, last, slice(hh * dk, (hh + 1) * dk), slice(hh * dv, (hh + 1) * dv)))
        s_off = [lax.dot_general(jnp.concatenate([qo[:, kc] for qo in q_off[bi]], axis=1),
                                 jnp.concatenate([ko[:, kc] for ko in k_off[bi]], axis=1),
                                 nt, preferred_element_type=F32)
                 for bi, _, _, _, kc, _ in units]
        s_diag = [lax.dot_general(q_hat[bi][:, kc], k_diag[bi][:, kc], nt, preferred_element_type=F32)
                  for bi, _, _, _, kc, _ in units]
        scores = [so + jnp.where(same_sub, jnp.where(causal, sd, 0.0), 0.0) for so, sd in zip(s_off, s_diag)]
        inter = [lax.dot_general(qd[rows, kc], st_ref[bi, hh].astype(BF16), nt, preferred_element_type=F32)
                 for bi, hh, rows, _, kc, _ in units]
        g_last = [gcum[last:last + 1, kc] for _, _, _, last, kc, _ in units]
        ks = [(k[rows, kc] * jnp.exp(gl - gcum[rows, kc])).astype(BF16)
              for (_, _, rows, _, kc, _), gl in zip(units, g_last)]
        upd = [lax.dot_general(vb[rows, vc], kk, tn, preferred_element_type=F32)
               for (_, _, rows, _, _, vc), kk in zip(units, ks)]
        for (bi, hh, rows, _, _, vc), sc, io in zip(units, scores, inter):
            oc_sc[rows, vc] = jnp.dot(sc.astype(BF16), vb[rows, vc], preferred_element_type=F32) + io
        for (bi, hh, _, _, _, _), gl, up in zip(units, g_last, upd):
            st_ref[bi, hh] = st_ref[bi, hh] * jnp.exp(gl) + up

    ng = ng_ref[...]
    for hh in range(B_HEADS):
        vc = slice(hh * dv, (hh + 1) * dv)
        oh = oc_sc[:, vc]
        oh = oh * lax.rsqrt(jnp.mean(oh * oh, axis=-1, keepdims=True) + NORM_EPS) * ng[:, vc]
        rh = r[:, vc]
        oc_sc[:, vc] = oh * (rh / (1.0 + jnp.exp(-rh)))
    y = x + jnp.dot(oc_sc[...].astype(BF16), wout_ref[...], preferred_element_type=F32)
    o_ref[...] = y.reshape(nb, tc, d)


def _mixer_b(x3, g, w_in, w_g2, b_g, norm_g, w_out):
    b, l, d = x3.shape
    tc = B_TOKENS
    kdim = w_g2.shape[1]
    vdim = w_out.shape[0]
    main = 2 * kdim + 2 * vdim
    w_in_p = _bf16_copy(jnp.concatenate(
        [w_in, jnp.zeros((d, LANES - B_GATE_RANK), w_in.dtype)], axis=1))
    w_g2_p = jnp.concatenate([w_g2, jnp.zeros((LANES - B_GATE_RANK, kdim), w_g2.dtype)], axis=0)
    hi = w_g2_p.astype(BF16)
    lo = (w_g2_p - hi.astype(F32)).astype(BF16)
    wg2 = jnp.stack([hi, lo])
    ng = jnp.tile(norm_g, B_HEADS).reshape(1, vdim)
    idx = jnp.arange(tc)
    tri = ((idx[:, None] >= idx[None, :]) &
           (idx[:, None] // B_CHUNK == idx[None, :] // B_CHUNK)).astype(BF16)
    return pl.pallas_call(
        _gla_kernel,
        out_shape=jax.ShapeDtypeStruct((b, l, d), F32),
        grid=(l // tc,),
        in_specs=[
            pl.BlockSpec((b, tc, d), lambda j: (0, j, 0)),
            _const_spec((1, d)),
            _const_spec((d, main + LANES)),
            _const_spec((2, LANES, kdim)),
            _const_spec((1, kdim)),
            _const_spec((1, vdim)),
            _const_spec((vdim, d)),
            _const_spec((tc, tc)),
        ],
        out_specs=pl.BlockSpec((b, tc, d), lambda j: (0, j, 0)),
        scratch_shapes=[pltpu.VMEM((b, B_HEADS, vdim // B_HEADS, kdim // B_HEADS), F32),
                        pltpu.VMEM((b * tc, vdim), F32)],
        compiler_params=pltpu.CompilerParams(
            dimension_semantics=("arbitrary",), vmem_limit_bytes=VMEM_LIMIT_BYTES),
        name="mixer_b_gla",
    )(x3, g.reshape(1, d), w_in_p, wg2, b_g.reshape(1, kdim), ng, _bf16_copy(w_out), tri)


def _sort16_pairs():
    pairs = []

    def merge(lo, n, r):
        step = 2 * r
        if step < n:
            merge(lo, n, step)
            merge(lo + r, n, step)
            pairs.extend((i, i + r) for i in range(lo + r, lo + n - r, step))
        else:
            pairs.append((lo, lo + r))

    def sort(lo, n):
        if n > 1:
            sort(lo, n // 2)
            sort(lo + n // 2, n // 2)
            merge(lo, n, 1)

    sort(0, P_TOPK)
    return pairs


def _sort16(vals):
    vals = list(vals)
    for i, j in _sort16_pairs():
        vals[i], vals[j] = jnp.maximum(vals[i], vals[j]), jnp.minimum(vals[i], vals[j])
    return vals


def _merge_top16(a, b):
    c = list(a)
    for m, bv in enumerate(b):
        i = P_TOPK - 1 - m
        c[i] = jnp.maximum(c[i], bv)
    dist = P_TOPK // 2
    while dist:
        for i in range(P_TOPK):
            if not i & dist:
                c[i], c[i + dist] = jnp.maximum(c[i], c[i + dist]), jnp.minimum(c[i], c[i + dist])
        dist //= 2
    return c


def _top16_of_keys(s_sc, lo, hi):
    if hi - lo == P_TOPK:
        return _sort16([s_sc[k * SUBLANES:(k + 1) * SUBLANES, :] for k in range(lo, hi)])
    mid = (lo + hi) // 2
    return _merge_top16(_top16_of_keys(s_sc, lo, mid), _top16_of_keys(s_sc, mid, hi))


def _count_true_prefix(pred, thr):
    steps = (8, 4, 2, 1)
    masks = []

    def pick(ms, cands):
        if not ms:
            return cands[0]
        half = len(cands) // 2
        return jnp.where(ms[0], pick(ms[1:], cands[half:]), pick(ms[1:], cands[:half]))

    for lvl, step in enumerate(steps):
        cands = []
        for combo in range(2 ** lvl):
            taken = sum(s for bit, s in enumerate(steps[:lvl]) if (combo >> (lvl - 1 - bit)) & 1)
            cands.append(thr[taken + step - 1])
        masks.append(pred(pick(masks, cands)))
    count = None
    for m, step in zip(masks, steps):
        term = jnp.where(m, float(step), 0.0)
        count = term if count is None else count + term
    return jnp.where(pred(thr[P_TOPK - 1]), float(P_TOPK), count)


def _to_key_major(s, s_sc):
    for g in range(s.shape[0] // SUBLANES):
        for c in range(PREP_SLABS):
            s_sc[pl.ds(g * SUBLANES * PREP_SLABS + c, SUBLANES, stride=PREP_SLABS), :] = (
                s[g * SUBLANES:(g + 1) * SUBLANES, c * LANES:(c + 1) * LANES])


def _peer_prep_kernel(x_ref, g_ref, wq_ref, k1_ref, k2_ref,
                      ht_ref, re_ref, c_ref, e1_ref,
                      ht_sc, qt_sc, s1_sc, s2_sc, rk_sc, e2_sc):
    hh = pl.program_id(1)
    half = k1_ref.shape[1]

    @pl.when(hh == 0)
    def _():
        h = _rms(x_ref[...], g_ref[...])
        ht = h.T.astype(BF16)
        ht_ref[...] = ht
        ht_sc[...] = ht

    qt_sc[...] = jnp.dot(wq_ref[0], ht_sc[...], preferred_element_type=F32).astype(BF16)
    s1 = jnp.dot(k1_ref[...], qt_sc[:half, :], preferred_element_type=F32)
    s2 = jnp.dot(k2_ref[...], qt_sc[half:, :], preferred_element_type=F32)
    _to_key_major(s1, s1_sc)
    _to_key_major(s2, s2_sc)

    v1 = _top16_of_keys(s1_sc, 0, P_NKEYS)
    v2 = _top16_of_keys(s2_sc, 0, P_NKEYS)

    top = [v1[0] + v2[b] for b in range(P_TOPK)]
    for a in range(1, P_TOPK // 2):
        top = _merge_top16(top, [v1[a] + v2[b] for b in range(P_TOPK // (a + 1))])
    top = _merge_top16(top, [v1[a] + v2[0] for a in range(P_TOPK // 2, P_TOPK)])
    tau = top[P_TOPK - 1]
    z = None
    for m in range(P_TOPK):
        term = jnp.exp(top[m] - top[0])
        z = term if z is None else z + term
    inv_z = 1.0 / z

    for k in range(P_NKEYS):
        tile = slice(k * SUBLANES, (k + 1) * SUBLANES)
        x2 = s2_sc[tile, :]
        rk_sc[tile, :] = _count_true_prefix(lambda t, x2=x2: t > x2, v2)
        e2_sc[tile, :] = jnp.exp(x2 - v2[0])
        x1 = s1_sc[tile, :]
        c_ref[0, 0, k] = _count_true_prefix(lambda t, x1=x1: x1 + t >= tau, v2)
        e1_ref[0, 0, k] = jnp.exp(x1 - v1[0]) * inv_z

    def key_tile(src_sc, jc, c):
        halves = [src_sc[pl.ds((jc * BF16_ROWS + r0) * PREP_SLABS + c, SUBLANES, stride=PREP_SLABS), :]
                  for r0 in (0, SUBLANES)]
        return jnp.concatenate(halves, axis=0).astype(BF16)

    for jc in range(P_NKEYS // BF16_ROWS):
        for c in range(PREP_SLABS):
            re_ref[c, jc, 0, 0:BF16_ROWS, :] = key_tile(rk_sc, jc, c)
            re_ref[c, jc, 0, BF16_ROWS:2 * BF16_ROWS, :] = key_tile(e2_sc, jc, c)


def _peer_prep(x2, g, w_q, k1, k2):
    n, d = x2.shape
    tt = P_TOKENS
    qdim = w_q.shape[1] // P_HEADS
    chunks = P_NKEYS // BF16_ROWS
    re_shape = jax.ShapeDtypeStruct((n // LANES, chunks, P_HEADS, 2 * BF16_ROWS, LANES), BF16)
    re_spec = pl.BlockSpec((PREP_SLABS, chunks, 1, 2 * BF16_ROWS, LANES), lambda t, h: (t, 0, h, 0, 0))
    row_shape = jax.ShapeDtypeStruct((n // tt, P_HEADS, P_NKEYS, PREP_SLABS, LANES), F32)
    row_spec = pl.BlockSpec((1, 1, P_NKEYS, PREP_SLABS, LANES), lambda t, h: (t, h, 0, 0, 0))
    wq = _bf16_copy(w_q, transpose=True).reshape(P_HEADS, qdim, d)
    return pl.pallas_call(
        _peer_prep_kernel,
        out_shape=(jax.ShapeDtypeStruct((d, n), BF16), re_shape, row_shape, row_shape),
        grid=(n // tt, P_HEADS),
        in_specs=[
            pl.BlockSpec((tt, d), lambda t, h: (t, 0)),
            _const_spec((1, d)),
            pl.BlockSpec((1, qdim, d), lambda t, h: (h, 0, 0)),
            _const_spec(k1.shape),
            _const_spec(k2.shape),
        ],
        out_specs=(pl.BlockSpec((d, tt), lambda t, h: (0, t)), re_spec, row_spec, row_spec),
        scratch_shapes=[pltpu.VMEM((d, tt), BF16),
                        pltpu.VMEM((qdim, tt), BF16),
                        pltpu.VMEM((P_NKEYS * PREP_SLABS, LANES), F32),
                        pltpu.VMEM((P_NKEYS * PREP_SLABS, LANES), F32),
                        pltpu.VMEM((P_NKEYS * PREP_SLABS, LANES), F32),
                        pltpu.VMEM((P_NKEYS * PREP_SLABS, LANES), F32)],
        compiler_params=pltpu.CompilerParams(
            dimension_semantics=("arbitrary", "arbitrary"), vmem_limit_bytes=VMEM_LIMIT_BYTES),
        name="peer_prep",
    )(x2, g.reshape(1, d), wq, _bf16_copy(k1), _bf16_copy(k2))


def _gate_times_act(re_ref, c_ref, e1_ref, a_ref, w_ref, blocks):
    for key1_rows, lc in blocks:
        lanes = slice(lc * LANES, (lc + 1) * LANES)

        def row_tile(ref, hh, i):
            return jnp.broadcast_to(ref[0, hh, i, lc:lc + 1, :], (BF16_ROWS, LANES)).astype(BF16)

        cnt = {(i, hh): row_tile(c_ref, hh, i) for i in key1_rows for hh in range(P_HEADS)}
        e1 = {(i, hh): row_tile(e1_ref, hh, i) for i in key1_rows for hh in range(P_HEADS)}
        for jc in range(P_NKEYS // BF16_ROWS):
            gate = {i: None for i in key1_rows}
            for hh in range(P_HEADS):
                row0 = ((lc * (P_NKEYS // BF16_ROWS) + jc) * P_HEADS + hh) * 2 * BF16_ROWS
                rank2 = re_ref[row0:row0 + BF16_ROWS, :]
                e2 = re_ref[row0 + BF16_ROWS:row0 + 2 * BF16_ROWS, :]
                for i in key1_rows:
                    term = jnp.where(rank2 < cnt[i, hh], e2, 0.0) * e1[i, hh]
                    gate[i] = term if gate[i] is None else gate[i] + term
            for i in key1_rows:
                rows = slice(i * P_NKEYS + jc * BF16_ROWS, i * P_NKEYS + (jc + 1) * BF16_ROWS)
                w_ref[rows, lanes] = gate[i] * _gelu(a_ref[rows, lanes]).astype(BF16)


def _peer_dense_kernel(ht_ref, re_ref, c_ref, e1_ref, u_ref, vt_ref, x_ref, fg_ref,
                       o_ref, a0_sc, a1_sc, w0_sc, w1_sc, acc_sc, re_sc, *, n_exp_tiles, n_tiles, final_norm):
    k = pl.program_id(0)
    c_pair = jnp.clip(k - 2, 0, n_tiles - 1)
    c_exp = c_pair % n_exp_tiles

    @pl.when(k == 0)
    def _():
        a1_sc[...] = jnp.zeros_like(a1_sc)
        w0_sc[...] = jnp.zeros_like(w0_sc)

    @pl.when(c_exp == 0)
    def _():
        acc_sc[...] = jnp.zeros_like(acc_sc)

    @pl.when(jnp.clip(k - 1, 0, n_tiles - 1) % n_exp_tiles == 0)
    def _():
        tile_rows = re_ref.shape[3]
        for lc in range(re_ref.shape[0]):
            for jc in range(re_ref.shape[1]):
                for hh in range(re_ref.shape[2]):
                    row0 = ((lc * re_ref.shape[1] + jc) * re_ref.shape[2] + hh) * tile_rows
                    re_sc[row0:row0 + tile_rows, :] = re_ref[lc, jc, hh]

    def stages(a_out, a_in, w_out, w_in):
        tb = ht_ref.shape[1]
        eb = a_in.shape[0]
        d = acc_sc.shape[0]
        blocks = [((i, i + 1), lc) for i in range(0, P_ROWS_PER_STEP, 2) for lc in range(tb // LANES)]
        n_units = (tb // MXU_COLS) * (eb // MXU_COLS + d // MXU_COLS)
        units_per_block = n_units // len(blocks)
        units = iter(range(n_units))

        def vpu_slice():
            unit = next(units)
            if unit % units_per_block == 0:
                b0 = unit // units_per_block
                _gate_times_act(re_sc, c_ref, e1_ref, a_in, w_out, blocks[b0:b0 + 1])

        for n in range(tb // MXU_COLS):
            cols = slice(n * MXU_COLS, (n + 1) * MXU_COLS)
            t = acc_sc[:, cols]
            for kt in range(eb // MXU_COLS):
                ks = slice(kt * MXU_COLS, (kt + 1) * MXU_COLS)
                t = t + jnp.dot(vt_ref[:, ks], w_in[ks, cols], preferred_element_type=F32)
                vpu_slice()
            acc_sc[:, cols] = t
            t = None
            for kt in range(d // MXU_COLS):
                ks = slice(kt * MXU_COLS, (kt + 1) * MXU_COLS)
                p = jnp.dot(u_ref[:, ks], ht_ref[ks, cols], preferred_element_type=F32)
                t = p if t is None else t + p
                vpu_slice()
            a_out[:, cols] = t

    @pl.when(k % 2 == 0)
    def _():
        stages(a0_sc, a1_sc, w1_sc, w0_sc)

    @pl.when(k % 2 == 1)
    def _():
        stages(a1_sc, a0_sc, w0_sc, w1_sc)

    @pl.when(jnp.logical_and(k >= 2, c_exp == n_exp_tiles - 1))
    def _():
        y = x_ref[...] + acc_sc[...].T
        if final_norm:
            y = _rms(y, fg_ref[...])
        o_ref[...] = y


def _peer_dense(x2, ht, re, cnt, e1, u_all, v_all, layer, final_g, final_norm):
    n, d = x2.shape
    n_exp = u_all.shape[1]
    eb = P_ROWS_PER_STEP * P_NKEYS
    assert eb == PACK_BLOCK, "V^T arrives tiled by the packing kernel's block"
    tb = P_TOKENS
    ne = n_exp // eb
    n_tiles = (n // tb) * ne
    last = n_tiles - 1

    def pair_a(k):
        return jnp.minimum(k, last)

    def pair_b(k):
        return jnp.clip(k - 1, 0, last)

    def pair_c(k):
        return jnp.clip(k - 2, 0, last)

    re_tiles = (tb // LANES) * (P_NKEYS // BF16_ROWS) * P_HEADS
    big = pl.BlockSpec((tb // LANES, P_NKEYS // BF16_ROWS, P_HEADS, 2 * BF16_ROWS, LANES),
                       lambda k: (pair_b(k) // ne, 0, 0, 0, 0), pipeline_mode=pl.Buffered(1))
    small = pl.BlockSpec((1, P_HEADS, P_ROWS_PER_STEP, PREP_SLABS, LANES),
                         lambda k: (pair_b(k) // ne, 0, pair_b(k) % ne, 0, 0))
    return pl.pallas_call(
        functools.partial(_peer_dense_kernel, n_exp_tiles=ne, n_tiles=n_tiles, final_norm=final_norm),
        out_shape=jax.ShapeDtypeStruct((n, d), F32),
        grid=(n_tiles + 2,),
        in_specs=[
            pl.BlockSpec((d, tb), lambda k: (0, pair_a(k) // ne)),
            big, small, small,
            pl.BlockSpec((eb, d), lambda k: (pair_a(k) % ne, 0)),
            pl.BlockSpec((None, d, eb), lambda k: (pair_c(k) % ne, 0, 0)),
            pl.BlockSpec((tb, d), lambda k: (pair_c(k) // ne, 0), pipeline_mode=pl.Buffered(1)),
            pl.BlockSpec((1, d), lambda k: (0, 0)),
        ],
        out_specs=pl.BlockSpec((tb, d), lambda k: (pair_c(k) // ne, 0)),
        scratch_shapes=[pltpu.VMEM((eb, tb), F32), pltpu.VMEM((eb, tb), F32),
                        pltpu.VMEM((eb, tb), BF16), pltpu.VMEM((eb, tb), BF16),
                        pltpu.VMEM((d, tb), F32),
                        pltpu.VMEM((re_tiles * 2 * BF16_ROWS, LANES), BF16)],
        compiler_params=pltpu.CompilerParams(
            dimension_semantics=("arbitrary",), vmem_limit_bytes=VMEM_LIMIT_BYTES),
        name="peer_dense",
    )(ht, re, cnt, e1, _bf16_copy(u_all, layer=layer),
      _bf16_copy(v_all, transpose=True, layer=layer, tiled=True), x2, final_g.reshape(1, d))


def _peer(x2, g, w_q, k1, k2, u_all, v_all, layer, final_g, final_norm):
    ht, re, cnt, e1 = _peer_prep(x2, g, w_q, k1, k2)
    return _peer_dense(x2, ht, re, cnt, e1, u_all, v_all, layer, final_g, final_norm)


def kernel(x, norm_mix, norm_ffn, final_norm, a_w_in, a_b_in, a_ln_g, a_ln_b, a_w_s, a_b_s, a_w_out,
           b_w_in, b_w_g2, b_b_g, b_norm, b_w_out, p_w_q, p_k1, p_k2, p_u, p_v):
    b, l, d = x.shape
    depth = norm_mix.shape[0]
    assert (b * l) % P_TOKENS == 0 and (b * l) % A_TOKENS == 0 and l % B_TOKENS == 0, (b, l)
    assert p_k1.shape[1] == P_NKEYS and p_u.shape[1] == P_NKEYS * P_NKEYS, (p_k1.shape, p_u.shape)
    x2 = x.reshape(b * l, d)
    for i in range(depth):
        j = i // 2
        if i % 2 == 0:
            x2 = _mixer_a(x2, norm_mix[i], a_w_in[j], a_b_in[j], a_ln_g[j], a_ln_b[j],
                          a_w_s[j], a_b_s[j], a_w_out[j])
        else:
            x2 = _mixer_b(x2.reshape(b, l, d), norm_mix[i], b_w_in[j], b_w_g2[j], b_b_g[j],
                          b_norm[j], b_w_out[j]).reshape(b * l, d)
        x2 = _peer(x2, norm_ffn[i], p_w_q[i], p_k1[i], p_k2[i], p_u, p_v, i,
                   final_norm, i == depth - 1)
    return x2.reshape(b, l, d)
```

```python
import functools
import math

import jax
import jax.numpy as jnp
from jax import lax
from jax.experimental import pallas as pl
from jax.experimental.pallas import tpu as pltpu

F32 = jnp.float32
BF16 = jnp.bfloat16

NORM_EPS = 1e-6

LANES = 128
SUBLANES = 8
BF16_ROWS = 2 * SUBLANES
MXU_COLS = 256
PACK_BLOCK = 1024
VMEM_LIMIT_BYTES = 58 * 1024 * 1024

A_GROUPS = 8
A_CHUNK = 128
A_TOKENS = 256
B_HEADS = 4
B_GATE_RANK = 16
B_GATE_NORM = 16.0
B_CHUNK = 64
B_SUB = 16
B_TOKENS = 128
NEG_EXPONENT = -1e30
P_HEADS = 8
P_NKEYS = 128
P_TOPK = 16
P_ROWS_PER_STEP = 8
PREP_SLABS = SUBLANES
P_TOKENS = PREP_SLABS * LANES


def _rms(x, g):
    return x * lax.rsqrt(jnp.mean(x * x, axis=-1, keepdims=True) + NORM_EPS) * g


def _gelu(x):
    return 0.5 * x * (1.0 + lax.erf(x * (1.0 / math.sqrt(2.0))))


def _split_bf16(x):
    hi = x.astype(BF16)
    lo = (x - hi.astype(F32)).astype(BF16)
    return hi, lo


def _pack_kernel(x_ref, o_ref, *, transpose):
    x = x_ref[...]
    if transpose:
        x = x.T
    o_ref[...] = x.astype(BF16)


def _bf16_copy(w, transpose=False, layer=None, tiled=False):
    rows, cols = w.shape[-2:]
    blk = min(PACK_BLOCK, rows)
    if layer is None:
        in_spec = pl.BlockSpec((blk, cols), lambda i: (i, 0))
    else:
        in_spec = pl.BlockSpec((None, blk, cols), lambda i: (layer, i, 0))
    if transpose and tiled:
        out_shape = jax.ShapeDtypeStruct((rows // blk, cols, blk), BF16)
        out_spec = pl.BlockSpec((None, cols, blk), lambda i: (i, 0, 0))
    elif transpose:
        out_shape = jax.ShapeDtypeStruct((cols, rows), BF16)
        out_spec = pl.BlockSpec((cols, blk), lambda i: (0, i))
    else:
        out_shape = jax.ShapeDtypeStruct((rows, cols), BF16)
        out_spec = pl.BlockSpec((blk, cols), lambda i: (i, 0))
    return pl.pallas_call(
        functools.partial(_pack_kernel, transpose=transpose),
        out_shape=out_shape,
        grid=(rows // blk,),
        in_specs=[in_spec],
        out_specs=out_spec,
        compiler_params=pltpu.CompilerParams(
            dimension_semantics=("arbitrary",), vmem_limit_bytes=VMEM_LIMIT_BYTES),
        name="bf16_copy",
    )(w)


def _const_spec(shape):
    nd = len(shape)
    return pl.BlockSpec(shape, lambda *_: (0,) * nd)


def _mixer_a_kernel(x_ref, g_ref, win_ref, bin_ref, lng_ref, lnb_ref, ws_ref, bs_ref, wout_ref,
                    o_ref, y_sc):
    tm = x_ref.shape[0]
    width = lng_ref.shape[1]
    gdim = width // A_GROUPS
    x = x_ref[...]
    h = _rms(x, g_ref[...]).astype(BF16)
    z = _gelu(jnp.dot(h, win_ref[...], preferred_element_type=F32) + bin_ref[...])
    u = z[:, :width]
    v = z[:, width:]
    mu = jnp.mean(v, axis=-1, keepdims=True)
    vc = v - mu
    var = jnp.mean(vc * vc, axis=-1, keepdims=True)
    vn = (vc * lax.rsqrt(var + NORM_EPS) * lng_ref[...] + lnb_ref[...]).astype(BF16)
    for c in range(tm // A_CHUNK):
        rows = slice(c * A_CHUNK, (c + 1) * A_CHUNK)
        for g in range(A_GROUPS):
            cols = slice(g * gdim, (g + 1) * gdim)
            sv = jnp.dot(ws_ref[g], vn[rows, cols], preferred_element_type=F32) + bs_ref[:, cols]
            y_sc[rows, cols] = (u[rows, cols] * sv).astype(BF16)
    o_ref[...] = x + jnp.dot(y_sc[...], wout_ref[...], preferred_element_type=F32)


def _mixer_a(x2, g, w_in, b_in, ln_g, ln_b, w_s, b_s, w_out):
    n, d = x2.shape
    tm = A_TOKENS
    width = ln_g.shape[0]
    causal = jnp.tril(jnp.ones((A_CHUNK, A_CHUNK), dtype=bool))
    ws = jnp.where(causal[None], w_s, 0.0).astype(BF16)
    bs = jnp.repeat(b_s.T, width // A_GROUPS, axis=1)
    return pl.pallas_call(
        _mixer_a_kernel,
        out_shape=jax.ShapeDtypeStruct((n, d), F32),
        grid=(n // tm,),
        in_specs=[
            pl.BlockSpec((tm, d), lambda i: (i, 0)),
            _const_spec((1, d)),
            _const_spec((d, 2 * width)),
            _const_spec((1, 2 * width)),
            _const_spec((1, width)),
            _const_spec((1, width)),
            _const_spec((A_GROUPS, A_CHUNK, A_CHUNK)),
            _const_spec((A_CHUNK, width)),
            _const_spec((width, d)),
        ],
        out_specs=pl.BlockSpec((tm, d), lambda i: (i, 0)),
        scratch_shapes=[pltpu.VMEM((tm, width), BF16)],
        compiler_params=pltpu.CompilerParams(
            dimension_semantics=("arbitrary",), vmem_limit_bytes=VMEM_LIMIT_BYTES),
        name="mixer_a",
    )(x2, g.reshape(1, d), _bf16_copy(w_in), b_in.reshape(1, -1), ln_g.reshape(1, -1),
      ln_b.reshape(1, -1), ws, bs, _bf16_copy(w_out))


def _gla_kernel(x_ref, g_ref, win_ref, wg2_ref, bg_ref, ng_ref, wout_ref, tri_ref,
                o_ref, st_ref, oc_sc):
    nb, tc, d = x_ref.shape
    kdim = bg_ref.shape[1]
    vdim = ng_ref.shape[1]
    dk = kdim // B_HEADS
    dv = vdim // B_HEADS

    @pl.when(pl.program_id(0) == 0)
    def _():
        st_ref[...] = jnp.zeros_like(st_ref)

    x = x_ref[...].reshape(nb * tc, d)
    h = _rms(x, g_ref[...]).astype(BF16)
    proj = jnp.dot(h, win_ref[...], preferred_element_type=F32)
    q = proj[:, :kdim] * (dk ** -0.5)
    k = proj[:, kdim:2 * kdim]
    v = proj[:, 2 * kdim:2 * kdim + vdim]
    r = proj[:, 2 * kdim + vdim:2 * kdim + 2 * vdim]
    glr = proj[:, 2 * kdim + 2 * vdim:]

    g_hi, g_lo = _split_bf16(glr)
    w_hi = wg2_ref[0]
    w_lo = wg2_ref[1]
    xg = (jnp.dot(g_hi, w_hi, preferred_element_type=F32)
          + jnp.dot(g_hi, w_lo, preferred_element_type=F32)
          + jnp.dot(g_lo, w_hi, preferred_element_type=F32)) + bg_ref[...]
    log_a = (jnp.minimum(xg, 0.0) - jnp.log(1.0 + jnp.exp(-jnp.abs(xg)))) * (1.0 / B_GATE_NORM)

    a_hi, a_lo = _split_bf16(log_a)
    tri = tri_ref[...]
    gcum = jnp.concatenate(
        [jnp.dot(tri, a_hi[bi * tc:(bi + 1) * tc], preferred_element_type=F32)
         + jnp.dot(tri, a_lo[bi * tc:(bi + 1) * tc], preferred_element_type=F32)
         for bi in range(nb)], axis=0)

    qd = (q * jnp.exp(gcum)).astype(BF16)
    vb = v.astype(BF16)
    row = lax.broadcasted_iota(jnp.int32, (B_CHUNK, B_CHUNK), 0)
    col = lax.broadcasted_iota(jnp.int32, (B_CHUNK, B_CHUNK), 1)
    same_sub = (row // B_SUB) == (col // B_SUB)
    causal = row >= col
    rid = lax.broadcasted_iota(jnp.int32, (B_CHUNK, kdim), 0)
    n_sub = B_CHUNK // B_SUB

    nt = (((1,), (1,)), ((), ()))
    tn = (((0,), (0,)), ((), ()))
    for c in range(tc // B_CHUNK):
        q_hat, k_diag, q_off, k_off = {}, {}, {}, {}
        for bi in range(nb):
            rows = slice(bi * tc + c * B_CHUNK, bi * tc + (c + 1) * B_CHUNK)
            g_c = gcum[rows]
            k_c = k[rows]
            gb = [g_c[B_SUB * i - 1:B_SUB * i, :] for i in range(1, n_sub)]
            gb_rows = jnp.concatenate([jnp.zeros((B_SUB, kdim), F32)]
                                      + [jnp.broadcast_to(b, (B_SUB, kdim)) for b in gb], axis=0)
            qh = q[rows] * jnp.exp(g_c - gb_rows)
            q_hat[bi] = qh.astype(BF16)
            k_diag[bi] = (k_c * jnp.exp(gb_rows - g_c)).astype(BF16)
            q_off[bi] = [jnp.where(rid // B_SUB == i, qh, 0.0).astype(BF16) for i in range(1, n_sub)]
            k_off[bi] = [(k_c * jnp.exp(jnp.where(rid < B_SUB * i, b - g_c, NEG_EXPONENT))).astype(BF16)
                         for i, b in zip(range(1, n_sub), gb)]

        units = []
        for bi in range(nb):
            rows = slice(bi * tc + c * B_CHUNK, bi * tc + (c + 1) * B_CHUNK)
            last = bi * tc + (c + 1) * B_CHUNK - 1
            for hh in range(B_HEADS):
                units.append((bi, hh, rows, last, slice(hh * dk, (hh + 1) * dk), slice(hh * dv, (hh + 1) * dv)))
        s_off = [lax.dot_general(jnp.concatenate([qo[:, kc] for qo in q_off[bi]], axis=1),
                                 jnp.concatenate([ko[:, kc] for ko in k_off[bi]], axis=1),
                                 nt, preferred_element_type=F32)
                 for bi, _, _, _, kc, _ in units]
        s_diag = [lax.dot_general(q_hat[bi][:, kc], k_diag[bi][:, kc], nt, preferred_element_type=F32)
                  for bi, _, _, _, kc, _ in units]
        scores = [so + jnp.where(same_sub, jnp.where(causal, sd, 0.0), 0.0) for so, sd in zip(s_off, s_diag)]
        inter = [lax.dot_general(qd[rows, kc], st_ref[bi, hh].astype(BF16), nt, preferred_element_type=F32)
                 for bi, hh, rows, _, kc, _ in units]
        g_last = [gcum[last:last + 1, kc] for _, _, _, last, kc, _ in units]
        ks = [(k[rows, kc] * jnp.exp(gl - gcum[rows, kc])).astype(BF16)
              for (_, _, rows, _, kc, _), gl in zip(units, g_last)]
        upd = [lax.dot_general(vb[rows, vc], kk, tn, preferred_element_type=F32)
               for (_, _, rows, _, _, vc), kk in zip(units, ks)]
        for (bi, hh, rows, _, _, vc), sc, io in zip(units, scores, inter):
            oc_sc[rows, vc] = jnp.dot(sc.astype(BF16), vb[rows, vc], preferred_element_type=F32) + io
        for (bi, hh, _, _, _, _), gl, up in zip(units, g_last, upd):
            st_ref[bi, hh] = st_ref[bi, hh] * jnp.exp(gl) + up

    ng = ng_ref[...]
    for hh in range(B_HEADS):
        vc = slice(hh * dv, (hh + 1) * dv)
        oh = oc_sc[:, vc]
        oh = oh * lax.rsqrt(jnp.mean(oh * oh, axis=-1, keepdims=True) + NORM_EPS) * ng[:, vc]
        rh = r[:, vc]
        oc_sc[:, vc] = oh * (rh / (1.0 + jnp.exp(-rh)))
    y = x + jnp.dot(oc_sc[...].astype(BF16), wout_ref[...], preferred_element_type=F32)
    o_ref[...] = y.reshape(nb, tc, d)


def _mixer_b(x3, g, w_in, w_g2, b_g, norm_g, w_out):
    b, l, d = x3.shape
    tc = B_TOKENS
    kdim = w_g2.shape[1]
    vdim = w_out.shape[0]
    main = 2 * kdim + 2 * vdim
    w_in_p = _bf16_copy(jnp.concatenate(
        [w_in, jnp.zeros((d, LANES - B_GATE_RANK), w_in.dtype)], axis=1))
    w_g2_p = jnp.concatenate([w_g2, jnp.zeros((LANES - B_GATE_RANK, kdim), w_g2.dtype)], axis=0)
    hi = w_g2_p.astype(BF16)
    lo = (w_g2_p - hi.astype(F32)).astype(BF16)
    wg2 = jnp.stack([hi, lo])
    ng = jnp.tile(norm_g, B_HEADS).reshape(1, vdim)
    idx = jnp.arange(tc)
    tri = ((idx[:, None] >= idx[None, :]) &
           (idx[:, None] // B_CHUNK == idx[None, :] // B_CHUNK)).astype(BF16)
    return pl.pallas_call(
        _gla_kernel,
        out_shape=jax.ShapeDtypeStruct((b, l, d), F32),
        grid=(l // tc,),
        in_specs=[
            pl.BlockSpec((b, tc, d), lambda j: (0, j, 0)),
            _const_spec((1, d)),
            _const_spec((d, main + LANES)),
            _const_spec((2, LANES, kdim)),
            _const_spec((1, kdim)),
            _const_spec((1, vdim)),
            _const_spec((vdim, d)),
            _const_spec((tc, tc)),
        ],
        out_specs=pl.BlockSpec((b, tc, d), lambda j: (0, j, 0)),
        scratch_shapes=[pltpu.VMEM((b, B_HEADS, vdim // B_HEADS, kdim // B_HEADS), F32),
                        pltpu.VMEM((b * tc, vdim), F32)],
        compiler_params=pltpu.CompilerParams(
            dimension_semantics=("arbitrary",), vmem_limit_bytes=VMEM_LIMIT_BYTES),
        name="mixer_b_gla",
    )(x3, g.reshape(1, d), w_in_p, wg2, b_g.reshape(1, kdim), ng, _bf16_copy(w_out), tri)


def _sort16_pairs():
    pairs = []

    def merge(lo, n, r):
        step = 2 * r
        if step < n:
            merge(lo, n, step)
            merge(lo + r, n, step)
            pairs.extend((i, i + r) for i in range(lo + r, lo + n - r, step))
        else:
            pairs.append((lo, lo + r))

    def sort(lo, n):
        if n > 1:
            sort(lo, n // 2)
            sort(lo + n // 2, n // 2)
            merge(lo, n, 1)

    sort(0, P_TOPK)
    return pairs


def _sort16(vals):
    vals = list(vals)
    for i, j in _sort16_pairs():
        vals[i], vals[j] = jnp.maximum(vals[i], vals[j]), jnp.minimum(vals[i], vals[j])
    return vals


def _merge_top16(a, b):
    c = list(a)
    for m, bv in enumerate(b):
        i = P_TOPK - 1 - m
        c[i] = jnp.maximum(c[i], bv)
    dist = P_TOPK // 2
    while dist:
        for i in range(P_TOPK):
            if not i & dist:
                c[i], c[i + dist] = jnp.maximum(c[i], c[i + dist]), jnp.minimum(c[i], c[i + dist])
        dist //= 2
    return c


def _top16_of_keys(s_sc, lo, hi):
    if hi - lo == P_TOPK:
        return _sort16([s_sc[k * SUBLANES:(k + 1) * SUBLANES, :] for k in range(lo, hi)])
    mid = (lo + hi) // 2
    return _merge_top16(_top16_of_keys(s_sc, lo, mid), _top16_of_keys(s_sc, mid, hi))


def _count_true_prefix(pred, thr):
    steps = (8, 4, 2, 1)
    masks = []

    def pick(ms, cands):
        if not ms:
            return cands[0]
        half = len(cands) // 2
        return jnp.where(ms[0], pick(ms[1:], cands[half:]), pick(ms[1:], cands[:half]))

    for lvl, step in enumerate(steps):
        cands = []
        for combo in range(2 ** lvl):
            taken = sum(s for bit, s in enumerate(steps[:lvl]) if (combo >> (lvl - 1 - bit)) & 1)
            cands.append(thr[taken + step - 1])
        masks.append(pred(pick(masks, cands)))
    count = None
    for m, step in zip(masks, steps):
        term = jnp.where(m, float(step), 0.0)
        count = term if count is None else count + term
    return jnp.where(pred(thr[P_TOPK - 1]), float(P_TOPK), count)


def _to_key_major(s, s_sc):
    for g in range(s.shape[0] // SUBLANES):
        for c in range(PREP_SLABS):
            s_sc[pl.ds(g * SUBLANES * PREP_SLABS + c, SUBLANES, stride=PREP_SLABS), :] = (
                s[g * SUBLANES:(g + 1) * SUBLANES, c * LANES:(c + 1) * LANES])


def _peer_prep_kernel(x_ref, g_ref, wq_ref, k1_ref, k2_ref,
                      ht_ref, re_ref, c_ref, e1_ref,
                      ht_sc, qt_sc, s1_sc, s2_sc, rk_sc, e2_sc):
    hh = pl.program_id(1)
    half = k1_ref.shape[1]

    @pl.when(hh == 0)
    def _():
        h = _rms(x_ref[...], g_ref[...])
        ht = h.T.astype(BF16)
        ht_ref[...] = ht
        ht_sc[...] = ht

    qt_sc[...] = jnp.dot(wq_ref[0], ht_sc[...], preferred_element_type=F32).astype(BF16)
    s1 = jnp.dot(k1_ref[...], qt_sc[:half, :], preferred_element_type=F32)
    s2 = jnp.dot(k2_ref[...], qt_sc[half:, :], preferred_element_type=F32)
    _to_key_major(s1, s1_sc)
    _to_key_major(s2, s2_sc)

    v1 = _top16_of_keys(s1_sc, 0, P_NKEYS)
    v2 = _top16_of_keys(s2_sc, 0, P_NKEYS)

    top = [v1[0] + v2[b] for b in range(P_TOPK)]
    for a in range(1, P_TOPK // 2):
        top = _merge_top16(top, [v1[a] + v2[b] for b in range(P_TOPK // (a + 1))])
    top = _merge_top16(top, [v1[a] + v2[0] for a in range(P_TOPK // 2, P_TOPK)])
    tau = top[P_TOPK - 1]
    z = None
    for m in range(P_TOPK):
        term = jnp.exp(top[m] - top[0])
        z = term if z is None else z + term
    inv_z = 1.0 / z

    for k in range(P_NKEYS):
        tile = slice(k * SUBLANES, (k + 1) * SUBLANES)
        x2 = s2_sc[tile, :]
        rk_sc[tile, :] = _count_true_prefix(lambda t, x2=x2: t > x2, v2)
        e2_sc[tile, :] = jnp.exp(x2 - v2[0])
        x1 = s1_sc[tile, :]
        c_ref[0, 0, k] = _count_true_prefix(lambda t, x1=x1: x1 + t >= tau, v2)
        e1_ref[0, 0, k] = jnp.exp(x1 - v1[0]) * inv_z

    def key_tile(src_sc, jc, c):
        halves = [src_sc[pl.ds((jc * BF16_ROWS + r0) * PREP_SLABS + c, SUBLANES, stride=PREP_SLABS), :]
                  for r0 in (0, SUBLANES)]
        return jnp.concatenate(halves, axis=0).astype(BF16)

    for jc in range(P_NKEYS // BF16_ROWS):
        for c in range(PREP_SLABS):
            re_ref[c, jc, 0, 0:BF16_ROWS, :] = key_tile(rk_sc, jc, c)
            re_ref[c, jc, 0, BF16_ROWS:2 * BF16_ROWS, :] = key_tile(e2_sc, jc, c)


def _peer_prep(x2, g, w_q, k1, k2):
    n, d = x2.shape
    tt = P_TOKENS
    qdim = w_q.shape[1] // P_HEADS
    chunks = P_NKEYS // BF16_ROWS
    re_shape = jax.ShapeDtypeStruct((n // LANES, chunks, P_HEADS, 2 * BF16_ROWS, LANES), BF16)
    re_spec = pl.BlockSpec((PREP_SLABS, chunks, 1, 2 * BF16_ROWS, LANES), lambda t, h: (t, 0, h, 0, 0))
    row_shape = jax.ShapeDtypeStruct((n // tt, P_HEADS, P_NKEYS, PREP_SLABS, LANES), F32)
    row_spec = pl.BlockSpec((1, 1, P_NKEYS, PREP_SLABS, LANES), lambda t, h: (t, h, 0, 0, 0))
    wq = _bf16_copy(w_q, transpose=True).reshape(P_HEADS, qdim, d)
    return pl.pallas_call(
        _peer_prep_kernel,
        out_shape=(jax.ShapeDtypeStruct((d, n), BF16), re_shape, row_shape, row_shape),
        grid=(n // tt, P_HEADS),
        in_specs=[
            pl.BlockSpec((tt, d), lambda t, h: (t, 0)),
            _const_spec((1, d)),
            pl.BlockSpec((1, qdim, d), lambda t, h: (h, 0, 0)),
            _const_spec(k1.shape),
            _const_spec(k2.shape),
        ],
        out_specs=(pl.BlockSpec((d, tt), lambda t, h: (0, t)), re_spec, row_spec, row_spec),
        scratch_shapes=[pltpu.VMEM((d, tt), BF16),
                        pltpu.VMEM((qdim, tt), BF16),
                        pltpu.VMEM((P_NKEYS * PREP_SLABS, LANES), F32),
                        pltpu.VMEM((P_NKEYS * PREP_SLABS, LANES), F32),
                        pltpu.VMEM((P_NKEYS * PREP_SLABS, LANES), F32),
                        pltpu.VMEM((P_NKEYS * PREP_SLABS, LANES), F32)],
        compiler_params=pltpu.CompilerParams(
            dimension_semantics=("arbitrary", "arbitrary"), vmem_limit_bytes=VMEM_LIMIT_BYTES),
        name="peer_prep",
    )(x2, g.reshape(1, d), wq, _bf16_copy(k1), _bf16_copy(k2))


def _gate_times_act(re_ref, c_ref, e1_ref, a_ref, w_ref, blocks):
    for key1_rows, lc in blocks:
        lanes = slice(lc * LANES, (lc + 1) * LANES)

        def row_tile(ref, hh, i):
            return jnp.broadcast_to(ref[0, hh, i, lc:lc + 1, :], (BF16_ROWS, LANES)).astype(BF16)

        cnt = {(i, hh): row_tile(c_ref, hh, i) for i in key1_rows for hh in range(P_HEADS)}
        e1 = {(i, hh): row_tile(e1_ref, hh, i) for i in key1_rows for hh in range(P_HEADS)}
        for jc in range(P_NKEYS // BF16_ROWS):
            gate = {i: None for i in key1_rows}
            for hh in range(P_HEADS):
                row0 = ((lc * (P_NKEYS // BF16_ROWS) + jc) * P_HEADS + hh) * 2 * BF16_ROWS
                rank2 = re_ref[row0:row0 + BF16_ROWS, :]
                e2 = re_ref[row0 + BF16_ROWS:row0 + 2 * BF16_ROWS, :]
                for i in key1_rows:
                    term = jnp.where(rank2 < cnt[i, hh], e2, 0.0) * e1[i, hh]
                    gate[i] = term if gate[i] is None else gate[i] + term
            for i in key1_rows:
                rows = slice(i * P_NKEYS + jc * BF16_ROWS, i * P_NKEYS + (jc + 1) * BF16_ROWS)
                w_ref[rows, lanes] = gate[i] * _gelu(a_ref[rows, lanes]).astype(BF16)


def _peer_dense_kernel(ht_ref, re_ref, c_ref, e1_ref, u_ref, vt_ref, x_ref, fg_ref,
                       o_ref, a0_sc, a1_sc, w0_sc, w1_sc, acc_sc, re_sc, u_sc, vt_sc,
                       *, n_exp_tiles, n_tiles, final_norm):
    k = pl.program_id(0)
    c_pair = jnp.clip(k - 2, 0, n_tiles - 1)
    c_exp = c_pair % n_exp_tiles

    @pl.when(k == 0)
    def _():
        a1_sc[...] = jnp.zeros_like(a1_sc)
        w0_sc[...] = jnp.zeros_like(w0_sc)

    @pl.when(c_exp == 0)
    def _():
        acc_sc[...] = jnp.zeros_like(acc_sc)

    @pl.when(jnp.clip(k - 1, 0, n_tiles - 1) % n_exp_tiles == 0)
    def _():
        tile_rows = re_ref.shape[3]
        for lc in range(re_ref.shape[0]):
            for jc in range(re_ref.shape[1]):
                for hh in range(re_ref.shape[2]):
                    row0 = ((lc * re_ref.shape[1] + jc) * re_ref.shape[2] + hh) * tile_rows
                    re_sc[row0:row0 + tile_rows, :] = re_ref[lc, jc, hh]

    def stages(a_out, a_in, w_out, w_in):
        tb = ht_ref.shape[1]
        eb = a_in.shape[0]
        d = acc_sc.shape[0]
        blocks = [((i, i + 1), lc) for i in range(0, P_ROWS_PER_STEP, 2) for lc in range(tb // LANES)]
        n_units = (tb // MXU_COLS) * (eb // MXU_COLS + d // MXU_COLS)
        units_per_block = n_units // len(blocks)
        units = iter(range(n_units))
        u_sc[...] = u_ref[...]
        vt_sc[...] = vt_ref[...]

        def vpu_slice():
            unit = next(units)
            if unit % units_per_block == 0:
                b0 = unit // units_per_block
                _gate_times_act(re_sc, c_ref, e1_ref, a_in, w_out, blocks[b0:b0 + 1])

        for n in range(tb // MXU_COLS):
            cols = slice(n * MXU_COLS, (n + 1) * MXU_COLS)
            t = acc_sc[:, cols]
            for kt in range(eb // MXU_COLS):
                ks = slice(kt * MXU_COLS, (kt + 1) * MXU_COLS)
                t = t + jnp.dot(vt_sc[:, ks], w_in[ks, cols], preferred_element_type=F32)
                vpu_slice()
            acc_sc[:, cols] = t
            t = None
            for kt in range(d // MXU_COLS):
                ks = slice(kt * MXU_COLS, (kt + 1) * MXU_COLS)
                p = jnp.dot(u_sc[:, ks], ht_ref[ks, cols], preferred_element_type=F32)
                t = p if t is None else t + p
                vpu_slice()
            a_out[:, cols] = t

    @pl.when(k % 2 == 0)
    def _():
        stages(a0_sc, a1_sc, w1_sc, w0_sc)

    @pl.when(k % 2 == 1)
    def _():
        stages(a1_sc, a0_sc, w0_sc, w1_sc)

    @pl.when(jnp.logical_and(k >= 2, c_exp == n_exp_tiles - 1))
    def _():
        y = x_ref[...] + acc_sc[...].T
        if final_norm:
            y = _rms(y, fg_ref[...])
        o_ref[...] = y


def _peer_dense(x2, ht, re, cnt, e1, u_all, v_all, layer, final_g, final_norm):
    n, d = x2.shape
    n_exp = u_all.shape[1]
    eb = P_ROWS_PER_STEP * P_NKEYS
    assert eb == PACK_BLOCK, "V^T arrives tiled by the packing kernel's block"
    tb = P_TOKENS
    ne = n_exp // eb
    n_tiles = (n // tb) * ne
    last = n_tiles - 1

    def pair_a(k):
        return jnp.minimum(k, last)

    def pair_b(k):
        return jnp.clip(k - 1, 0, last)

    def pair_c(k):
        return jnp.clip(k - 2, 0, last)

    re_tiles = (tb // LANES) * (P_NKEYS // BF16_ROWS) * P_HEADS
    big = pl.BlockSpec((tb // LANES, P_NKEYS // BF16_ROWS, P_HEADS, 2 * BF16_ROWS, LANES),
                       lambda k: (pair_b(k) // ne, 0, 0, 0, 0), pipeline_mode=pl.Buffered(1))
    small = pl.BlockSpec((1, P_HEADS, P_ROWS_PER_STEP, PREP_SLABS, LANES),
                         lambda k: (pair_b(k) // ne, 0, pair_b(k) % ne, 0, 0))
    return pl.pallas_call(
        functools.partial(_peer_dense_kernel, n_exp_tiles=ne, n_tiles=n_tiles, final_norm=final_norm),
        out_shape=jax.ShapeDtypeStruct((n, d), F32),
        grid=(n_tiles + 2,),
        in_specs=[
            pl.BlockSpec((d, tb), lambda k: (0, pair_a(k) // ne)),
            big, small, small,
            pl.BlockSpec((eb, d), lambda k: (pair_a(k) % ne, 0)),
            pl.BlockSpec((None, d, eb), lambda k: (pair_c(k) % ne, 0, 0)),
            pl.BlockSpec((tb, d), lambda k: (pair_c(k) // ne, 0), pipeline_mode=pl.Buffered(1)),
            pl.BlockSpec((1, d), lambda k: (0, 0)),
        ],
        out_specs=pl.BlockSpec((tb, d), lambda k: (pair_c(k) // ne, 0)),
        scratch_shapes=[pltpu.VMEM((eb, tb), F32), pltpu.VMEM((eb, tb), F32),
                        pltpu.VMEM((eb, tb), BF16), pltpu.VMEM((eb, tb), BF16),
                        pltpu.VMEM((d, tb), F32),
                        pltpu.VMEM((re_tiles * 2 * BF16_ROWS, LANES), BF16),
                        pltpu.VMEM((eb, d), BF16), pltpu.VMEM((d, eb), BF16)],
        compiler_params=pltpu.CompilerParams(
            dimension_semantics=("arbitrary",), vmem_limit_bytes=VMEM_LIMIT_BYTES),
        name="peer_dense",
    )(ht, re, cnt, e1, _bf16_copy(u_all, layer=layer),
      _bf16_copy(v_all, transpose=True, layer=layer, tiled=True), x2, final_g.reshape(1, d))


def _peer(x2, g, w_q, k1, k2, u_all, v_all, layer, final_g, final_norm):
    ht, re, cnt, e1 = _peer_prep(x2, g, w_q, k1, k2)
    return _peer_dense(x2, ht, re, cnt, e1, u_all, v_all, layer, final_g, final_norm)


def kernel(x, norm_mix, norm_ffn, final_norm, a_w_in, a_b_in, a_ln_g, a_ln_b, a_w_s, a_b_s, a_w_out,
           b_w_in, b_w_g2, b_b_g, b_norm, b_w_out, p_w_q, p_k1, p_k2, p_u, p_v):
    b, l, d = x.shape
    depth = norm_mix.shape[0]
    assert (b * l) % P_TOKENS == 0 and (b * l) % A_TOKENS == 0 and l % B_TOKENS == 0, (b, l)
    assert p_k1.shape[1] == P_NKEYS and p_u.shape[1] == P_NKEYS * P_NKEYS, (p_k1.shape, p_u.shape)
    x2 = x.reshape(b * l, d)
    for i in range(depth):
        j = i // 2
        if i % 2 == 0:
            x2 = _mixer_a(x2, norm_mix[i], a_w_in[j], a_b_in[j], a_ln_g[j], a_ln_b[j],
                          a_w_s[j], a_b_s[j], a_w_out[j])
        else:
            x2 = _mixer_b(x2.reshape(b, l, d), norm_mix[i], b_w_in[j], b_w_g2[j], b_b_g[j],
                          b_norm[j], b_w_out[j]).reshape(b * l, d)
        x2 = _peer(x2, norm_ffn[i], p_w_q[i], p_k1[i], p_k2[i], p_u, p_v, i,
                   final_norm, i == depth - 1)
    return x2.reshape(b, l, d)
```
